```python
import jax, jax.numpy as jnp
from jax import lax
import numpy as np

D_MODEL = 1024
BATCH = 4
SEQ = 8192
DEPTH = 1
DEC_BATCH = 8
DEC_SEQ = 4096
PAST_LEN = 128

N_META = 16
N_HEADS = 8
QK_NOPE_DIM = 128
QK_ROPE_DIM = 64
QK_DIM = QK_NOPE_DIM + QK_ROPE_DIM
V_HEAD_DIM = 128
Q_LORA_RANK = 384
KV_LORA_RANK = 256
ROPE_THETA = 10000.0
Q_BLOCK = 128
N_FNET_GROUPS = 4
FNET_GROUP_DIM = 128
FNET_DIM = N_FNET_GROUPS * FNET_GROUP_DIM
N_BRANCHES = 2
IN_DIM = Q_LORA_RANK + KV_LORA_RANK + QK_ROPE_DIM + FNET_DIM + N_BRANCHES * D_MODEL
N_EXPERT_GROUPS = 4
EXPERTS_PER_GROUP = 8
N_EXPERTS = N_EXPERT_GROUPS * EXPERTS_PER_GROUP
TOP_K_IN_GROUP = 2
D_FF_EXPERT = 256
EPS = 1e-6

kernel_name = "hybrid_mla_fnet_hiermoe_encoder"


def rmsnorm(x, g):
    xf = x.astype(jnp.float32)
    y = xf * lax.rsqrt(jnp.mean(xf * xf, axis=-1, keepdims=True) + EPS)
    return (y * g.astype(jnp.float32)).astype(x.dtype)


def rope_tables(L, dtype):
    half = QK_ROPE_DIM // 2
    inv_freq = 1.0 / (ROPE_THETA ** (jnp.arange(half, dtype=jnp.float32) / half))
    ang = jnp.arange(L, dtype=jnp.float32)[:, None] * inv_freq[None, :]
    return jnp.cos(ang).astype(dtype), jnp.sin(ang).astype(dtype)


def apply_rope(x, cos, sin):
    x1, x2 = jnp.split(x, 2, axis=-1)
    return jnp.concatenate([x1 * cos - x2 * sin, x2 * cos + x1 * sin], axis=-1)


def mla_attention(c_q, c_kv, k_r, cos, sin, g_q, w_uq, g_kv, w_uk, w_uv):
    B, L, _ = c_q.shape
    q = jnp.einsum('blr,rhd->blhd', rmsnorm(c_q, g_q), w_uq)
    q_nope, q_rope = q[..., :QK_NOPE_DIM], q[..., QK_NOPE_DIM:]
    q_rope = apply_rope(q_rope, cos[:, None, :], sin[:, None, :])
    ckv = rmsnorm(c_kv, g_kv)
    k_nope = jnp.einsum('blr,rhd->blhd', ckv, w_uk)
    v = jnp.einsum('blr,rhd->blhd', ckv, w_uv)
    k_rope = apply_rope(k_r, cos, sin)
    scale = 1.0 / float(np.sqrt(QK_DIM))
    nb = -(-L // Q_BLOCK)
    Lp = nb * Q_BLOCK
    pad = ((0, 0), (0, Lp - L), (0, 0), (0, 0))
    qn_b = jnp.pad(q_nope, pad).reshape(B, nb, Q_BLOCK, N_HEADS, QK_NOPE_DIM).transpose(1, 0, 2, 3, 4)
    qr_b = jnp.pad(q_rope, pad).reshape(B, nb, Q_BLOCK, N_HEADS, QK_ROPE_DIM).transpose(1, 0, 2, 3, 4)

    def attend(blk):
        qn, qr = blk
        s = (jnp.einsum('bqhd,bkhd->bhqk', qn, k_nope)
             + jnp.einsum('bqhd,bkd->bhqk', qr, k_rope)).astype(jnp.float32) * scale
        p = jax.nn.softmax(s, axis=-1).astype(v.dtype)
        return jnp.einsum('bhqk,bkhd->bqhd', p, v)

    o = lax.map(attend, (qn_b, qr_b))
    o = o.transpose(1, 0, 2, 3, 4).reshape(B, Lp, N_HEADS * V_HEAD_DIM)
    return o[:, :L]


def fnet_mix(u):
    B, L, _ = u.shape
    uf = u.astype(jnp.float32).reshape(B, L, N_FNET_GROUPS, FNET_GROUP_DIM)
    f = jnp.real(jnp.fft.fft2(uf, axes=(1, 3), norm='ortho'))
    return f.reshape(B, L, FNET_DIM).astype(u.dtype)


def mixer_layer(h, cos, sin, norm_attn, w_in, g_q, w_uq, g_kv, w_uk, w_uv, w_o_mla, w_fo, w_out):
    B, L, _ = h.shape
    hn = rmsnorm(h, norm_attn)
    z = jnp.einsum('bld,de->ble', hn, w_in)
    s1 = Q_LORA_RANK
    s2 = s1 + KV_LORA_RANK
    s3 = s2 + QK_ROPE_DIM
    s4 = s3 + FNET_DIM
    c_q, c_kv, k_r, u_f, g_logits = z[..., :s1], z[..., s1:s2], z[..., s2:s3], z[..., s3:s4], z[..., s4:]
    att = mla_attention(c_q, c_kv, k_r, cos, sin, g_q, w_uq, g_kv, w_uk, w_uv)
    a = jnp.einsum('ble,ed->bld', att, w_o_mla)
    b = jnp.einsum('ble,ed->bld', fnet_mix(u_f), w_fo)
    gates = jax.nn.sigmoid(g_logits.astype(jnp.float32)).reshape(B, L, N_BRANCHES, D_MODEL).astype(h.dtype)
    merged = gates[:, :, 0] * a + gates[:, :, 1] * b
    return jnp.einsum('bld,de->ble', merged, w_out)


def hier_moe(hn, w_rg, b_rg, w_re, b_re, w_eg, w_eu, w_ed):
    B, L, D = hn.shape
    t = hn.reshape(B * L, D)
    n = t.shape[0]
    g_prob = jax.nn.softmax((t @ w_rg + b_rg).astype(jnp.float32), axis=-1)
    g_idx = jnp.argmax(g_prob, axis=-1)
    g_w = jnp.take_along_axis(g_prob, g_idx[:, None], axis=1)[:, 0]
    e_logit = (t @ w_re + b_re).astype(jnp.float32).reshape(n, N_EXPERT_GROUPS, EXPERTS_PER_GROUP)
    e_sel = jnp.take_along_axis(e_logit, g_idx[:, None, None], axis=1)[:, 0]
    e_prob = jax.nn.softmax(e_sel, axis=-1)
    top_w, top_i = lax.top_k(e_prob, TOP_K_IN_GROUP)
    top_w = top_w / jnp.sum(top_w, axis=-1, keepdims=True)
    w_tok = g_w[:, None] * top_w
    comb_in = jnp.sum(jax.nn.one_hot(top_i, EXPERTS_PER_GROUP, dtype=jnp.float32) * w_tok[..., None], axis=1)
    combine = (jax.nn.one_hot(g_idx, N_EXPERT_GROUPS, dtype=jnp.float32)[:, :, None] * comb_in[:, None, :]).astype(t.dtype)
    out = jnp.zeros_like(t)
    for grp in range(N_EXPERT_GROUPS):
        gt = jnp.einsum('nd,edf->nef', t, w_eg[grp])
        up = jnp.einsum('nd,edf->nef', t, w_eu[grp])
        act = jax.nn.silu(gt) * up * combine[:, grp, :, None]
        out = out + jnp.einsum('nef,efd->nd', act, w_ed[grp])
    return out.reshape(B, L, D)


def trunk(x, meta_tokens, norm_attn, w_in, g_q, w_uq, g_kv, w_uk, w_uv, w_o_mla, w_fo, w_out,
          norm_moe, w_router_group, b_router_group, w_router_expert, b_router_expert,
          w_exp_gate, w_exp_up, w_exp_down, norm_final):
    B, S, D = x.shape
    meta = jnp.broadcast_to(meta_tokens.astype(x.dtype)[None], (B, N_META, D))
    h = jnp.concatenate([meta, x], axis=1)
    cos, sin = rope_tables(S + N_META, h.dtype)
    for layer in range(DEPTH):
        h = h + mixer_layer(h, cos, sin, norm_attn[layer], w_in[layer], g_q[layer], w_uq[layer],
                            g_kv[layer], w_uk[layer], w_uv[layer], w_o_mla[layer], w_fo[layer], w_out[layer])
        h = h + hier_moe(rmsnorm(h, norm_moe[layer]), w_router_group[layer], b_router_group[layer],
                         w_router_expert[layer], b_router_expert[layer],
                         w_exp_gate[layer], w_exp_up[layer], w_exp_down[layer])
    h = rmsnorm(h, norm_final)
    return h[:, N_META:]


def setup_inputs(seed: int = 0) -> dict:
    key = jax.random.key(seed)
    ks = jax.random.split(key, 24)

    def nrm(k, shape, scale):
        return jax.random.normal(k, shape, jnp.float32) * scale

    def gain(k, shape):
        return 1.0 + 0.02 * jax.random.normal(k, shape, jnp.float32)

    return {
        "x_prompt": nrm(ks[0], (BATCH, SEQ, D_MODEL), 1.0),
        "x_sample": nrm(ks[1], (DEC_BATCH, DEC_SEQ, D_MODEL), 1.0),
        "meta_tokens": nrm(ks[2], (N_META, D_MODEL), 1.0),
        "norm_attn": gain(ks[3], (DEPTH, D_MODEL)),
        "w_in": nrm(ks[4], (DEPTH, D_MODEL, IN_DIM), D_MODEL ** -0.5),
        "g_q": gain(ks[5], (DEPTH, Q_LORA_RANK)),
        "w_uq": nrm(ks[6], (DEPTH, Q_LORA_RANK, N_HEADS, QK_DIM), Q_LORA_RANK ** -0.5),
        "g_kv": gain(ks[7], (DEPTH, KV_LORA_RANK)),
        "w_uk": nrm(ks[8], (DEPTH, KV_LORA_RANK, N_HEADS, QK_NOPE_DIM), KV_LORA_RANK ** -0.5),
        "w_uv": nrm(ks[9], (DEPTH, KV_LORA_RANK, N_HEADS, V_HEAD_DIM), KV_LORA_RANK ** -0.5),
        "w_o_mla": nrm(ks[10], (DEPTH, N_HEADS * V_HEAD_DIM, D_MODEL), (N_HEADS * V_HEAD_DIM) ** -0.5),
        "w_fo": nrm(ks[11], (DEPTH, FNET_DIM, D_MODEL), FNET_DIM ** -0.5),
        "w_out": nrm(ks[12], (DEPTH, D_MODEL, D_MODEL), D_MODEL ** -0.5),
        "norm_moe": gain(ks[13], (DEPTH, D_MODEL)),
        "w_router_group": nrm(ks[14], (DEPTH, D_MODEL, N_EXPERT_GROUPS), D_MODEL ** -0.5),
        "b_router_group": nrm(ks[15], (DEPTH, N_EXPERT_GROUPS), 0.01),
        "w_router_expert": nrm(ks[16], (DEPTH, D_MODEL, N_EXPERTS), D_MODEL ** -0.5),
        "b_router_expert": nrm(ks[17], (DEPTH, N_EXPERTS), 0.01),
        "w_exp_gate": nrm(ks[18], (DEPTH, N_EXPERT_GROUPS, EXPERTS_PER_GROUP, D_MODEL, D_FF_EXPERT), D_MODEL ** -0.5),
        "w_exp_up": nrm(ks[19], (DEPTH, N_EXPERT_GROUPS, EXPERTS_PER_GROUP, D_MODEL, D_FF_EXPERT), D_MODEL ** -0.5),
        "w_exp_down": nrm(ks[20], (DEPTH, N_EXPERT_GROUPS, EXPERTS_PER_GROUP, D_FF_EXPERT, D_MODEL), D_FF_EXPERT ** -0.5),
        "norm_final": gain(ks[21], (D_MODEL,)),
    }


def reference(x_prompt, x_sample, meta_tokens, norm_attn, w_in, g_q, w_uq, g_kv, w_uk, w_uv,
              w_o_mla, w_fo, w_out, norm_moe, w_router_group, b_router_group, w_router_expert,
              b_router_expert, w_exp_gate, w_exp_up, w_exp_down, norm_final):
    y_prompt = trunk(x_prompt, meta_tokens, norm_attn, w_in, g_q, w_uq, g_kv, w_uk, w_uv, w_o_mla, w_fo, w_out,
                     norm_moe, w_router_group, b_router_group, w_router_expert, b_router_expert,
                     w_exp_gate, w_exp_up, w_exp_down, norm_final)
    y_sample = trunk(x_sample, meta_tokens, norm_attn, w_in, g_q, w_uq, g_kv, w_uk, w_uv, w_o_mla, w_fo, w_out,
                     norm_moe, w_router_group, b_router_group, w_router_expert, b_router_expert,
                     w_exp_gate, w_exp_up, w_exp_down, norm_final)
    return (y_prompt, y_sample)
```

```python
import functools
import math

import jax
import jax.numpy as jnp
from jax import lax
from jax.experimental import pallas as pl
from jax.experimental.pallas import tpu as pltpu

D_MODEL = 1024
N_META = 16
N_HEADS = 8
QK_NOPE_DIM = 128
QK_ROPE_DIM = 64
ROPE_HALF = QK_ROPE_DIM // 2
QK_DIM = QK_NOPE_DIM + QK_ROPE_DIM
V_HEAD_DIM = 128
Q_LORA_RANK = 384
KV_LORA_RANK = 256
ROPE_THETA = 10000.0
N_FNET_GROUPS = 4
FNET_GROUP_DIM = 128
FNET_DIM = N_FNET_GROUPS * FNET_GROUP_DIM
N_EXPERT_GROUPS = 4
EXPERTS_PER_GROUP = 8
N_EXPERTS = N_EXPERT_GROUPS * EXPERTS_PER_GROUP
D_FF_EXPERT = 256
EPS = 1e-6

LANES = 128
META_PAD = 128
VMEM_LIMIT_BYTES = 56 * 1024 * 1024
NEG_BIG = -1e30

_CQ0, _CQ1 = 0, Q_LORA_RANK
_CKV0, _CKV1 = _CQ1, _CQ1 + KV_LORA_RANK
_UF0, _UF1 = _CKV1, _CKV1 + FNET_DIM
_G0, _G1 = _UF1, _UF1 + 2 * D_MODEL
_KR0, _KR1 = _G1, _G1 + LANES
IN_DIM_PADDED = _KR1

F32 = jnp.float32
BF16 = jnp.bfloat16


def _dot(a, b):
    return jnp.dot(a, b, preferred_element_type=F32)


def _rms(x, g):
    ms = jnp.mean(x * x, axis=-1, keepdims=True)
    return x * lax.rsqrt(ms + EPS) * g


def _params(semantics):
    return pltpu.CompilerParams(dimension_semantics=semantics, vmem_limit_bytes=VMEM_LIMIT_BYTES)


def _in_proj_body(x_ref, na_ref, win_ref, gq_ref, wuqT_ref, gkv_ref, wuk_ref, wuvT_ref,
                  cosn_ref, sinn_ref, cosT_ref, sinT_ref, cc_ref, sc_ref,
                  qT_ref, k_ref, vT_ref, a_ref, b_ref, gate_ref):
    x = x_ref[0]
    hb = _rms(x, na_ref[...]).astype(BF16)

    cqn = _rms(_dot(hb, win_ref[:, _CQ0:_CQ1]), gq_ref[...])
    qT = _dot(wuqT_ref[...], cqn.T.astype(BF16))
    cT = cosT_ref[...]
    sT = sinT_ref[...]
    scale = 1.0 / math.sqrt(QK_DIM)
    for h in range(N_HEADS):
        r0 = h * QK_DIM
        r1 = r0 + QK_NOPE_DIM
        r2 = r1 + ROPE_HALF
        x1 = qT[r1:r2]
        x2 = qT[r2:r0 + QK_DIM]
        qT_ref[0, h, 0:QK_NOPE_DIM, :] = (qT[r0:r1] * scale).astype(BF16)
        qT_ref[0, h, QK_NOPE_DIM:QK_NOPE_DIM + ROPE_HALF, :] = ((x1 * cT - x2 * sT) * scale).astype(BF16)
        qT_ref[0, h, QK_NOPE_DIM + ROPE_HALF:QK_DIM, :] = ((x2 * cT + x1 * sT) * scale).astype(BF16)

    ckvn = _rms(_dot(hb, win_ref[:, _CKV0:_CKV1]), gkv_ref[...])
    kn = _dot(ckvn.astype(BF16), wuk_ref[...])
    kr = _dot(hb, win_ref[:, _KR0:_KR1])
    lane = lax.broadcasted_iota(jnp.int32, kr.shape, 1)
    kr_rot = jnp.where(lane < ROPE_HALF, -pltpu.roll(kr, LANES - ROPE_HALF, 1), pltpu.roll(kr, ROPE_HALF, 1))
    krope = (kr * cosn_ref[...] + kr_rot * sinn_ref[...]).astype(BF16)
    for h in range(N_HEADS):
        k_ref[0, h, :, 0:QK_NOPE_DIM] = kn[:, h * QK_NOPE_DIM:(h + 1) * QK_NOPE_DIM].astype(BF16)
        k_ref[0, h, :, QK_NOPE_DIM:QK_DIM] = krope[:, 0:QK_ROPE_DIM]
    vT = _dot(wuvT_ref[...], ckvn.T.astype(BF16))
    for h in range(N_HEADS):
        vT_ref[0, h, 0] = vT[h * V_HEAD_DIM:(h + 1) * V_HEAD_DIM].astype(BF16)

    ub = _dot(hb, win_ref[:, _UF0:_UF1]).astype(BF16)
    a_ref[...] = _dot(ub, cc_ref[...]).astype(BF16)
    b_ref[...] = _dot(ub, sc_ref[...]).astype(BF16)

    gl = _dot(hb, win_ref[:, _G0:_G1])
    gate_ref[0] = (1.0 / (1.0 + jnp.exp(-gl))).astype(BF16)


def _in_proj(x, w, rope, ts):
    B, S, _ = x.shape
    nt = S // ts
    const = lambda shape: pl.BlockSpec(shape, lambda b, t: (0,) * len(shape))
    out_shape = (
        jax.ShapeDtypeStruct((B, N_HEADS, QK_DIM, S), BF16),
        jax.ShapeDtypeStruct((B, N_HEADS, S, QK_DIM), BF16),
        jax.ShapeDtypeStruct((B, N_HEADS, nt, V_HEAD_DIM, ts), BF16),
        jax.ShapeDtypeStruct((S, B * FNET_DIM), BF16),
        jax.ShapeDtypeStruct((S, B * FNET_DIM), BF16),
        jax.ShapeDtypeStruct((B, S, 2 * D_MODEL), BF16),
    )
    in_specs = [
        pl.BlockSpec((1, ts, D_MODEL), lambda b, t: (b, t, 0)),
        const((1, D_MODEL)),
        const((D_MODEL, IN_DIM_PADDED)),
        const((1, Q_LORA_RANK)),
        const((N_HEADS * QK_DIM, Q_LORA_RANK)),
        const((1, KV_LORA_RANK)),
        const((KV_LORA_RANK, N_HEADS * QK_NOPE_DIM)),
        const((N_HEADS * V_HEAD_DIM, KV_LORA_RANK)),
        pl.BlockSpec((ts, LANES), lambda b, t: (t, 0)),
        pl.BlockSpec((ts, LANES), lambda b, t: (t, 0)),
        pl.BlockSpec((ROPE_HALF, ts), lambda b, t: (0, t)),
        pl.BlockSpec((ROPE_HALF, ts), lambda b, t: (0, t)),
        const((FNET_DIM, FNET_DIM)),
        const((FNET_DIM, FNET_DIM)),
    ]
    out_specs = (
        pl.BlockSpec((1, N_HEADS, QK_DIM, ts), lambda b, t: (b, 0, 0, t)),
        pl.BlockSpec((1, N_HEADS, ts, QK_DIM), lambda b, t: (b, 0, t, 0)),
        pl.BlockSpec((1, N_HEADS, 1, V_HEAD_DIM, ts), lambda b, t: (b, 0, t, 0, 0)),
        pl.BlockSpec((ts, FNET_DIM), lambda b, t: (t, b)),
        pl.BlockSpec((ts, FNET_DIM), lambda b, t: (t, b)),
        pl.BlockSpec((1, ts, 2 * D_MODEL), lambda b, t: (b, t, 0)),
    )
    return pl.pallas_call(
        _in_proj_body, grid=(B, nt), in_specs=in_specs, out_specs=out_specs, out_shape=out_shape,
        compiler_params=_params(("parallel", "parallel")), name="in_proj",
    )(x, w["norm_attn"], w["w_in"], w["g_q"], w["w_uqT"], w["g_kv"], w["w_uk"], w["w_uvT"],
      rope["cos_nat"], rope["sin_nat"], rope["cosT"], rope["sinT"], w["cc"], w["sc"])


def _attn_body(qT_ref, k_ref, vT_ref, km_ref, vmT_ref, o_ref, *, nk, tk):
    q = qT_ref[0, 0]
    s = _dot(km_ref[0], q)
    row = lax.broadcasted_iota(jnp.int32, s.shape, 0)
    s = jnp.where(row < N_META, s, NEG_BIG)
    m = jnp.max(s, axis=0, keepdims=True)
    p = jnp.exp(s - m)
    l = jnp.sum(p, axis=0, keepdims=True)
    acc = _dot(vmT_ref[0], p.astype(BF16))

    def step(j, carry):
        m, l, acc = carry
        kc = k_ref[0, 0, pl.ds(pl.multiple_of(j * tk, tk), tk), :]
        s = _dot(kc, q)
        m_new = jnp.maximum(m, jnp.max(s, axis=0, keepdims=True))
        alpha = jnp.exp(m - m_new)
        p = jnp.exp(s - m_new)
        l = alpha * l + jnp.sum(p, axis=0, keepdims=True)
        acc = alpha * acc + _dot(vT_ref[0, 0, j], p.astype(BF16))
        return m_new, l, acc

    m, l, acc = lax.fori_loop(0, nk, step, (m, l, acc))
    o_ref[0] = (acc / l).T.astype(BF16)


def _attention(qT, k, vT, k_meta, vT_meta, tq):
    B, H, _, S = qT.shape
    nk, tk = vT.shape[2], vT.shape[4]
    return pl.pallas_call(
        functools.partial(_attn_body, nk=nk, tk=tk),
        grid=(B, H, S // tq),
        in_specs=[
            pl.BlockSpec((1, 1, QK_DIM, tq), lambda b, h, i: (b, h, 0, i)),
            pl.BlockSpec((1, 1, S, QK_DIM), lambda b, h, i: (b, h, 0, 0)),
            pl.BlockSpec((1, 1, nk, V_HEAD_DIM, tk), lambda b, h, i: (b, h, 0, 0, 0)),
            pl.BlockSpec((1, META_PAD, QK_DIM), lambda b, h, i: (h, 0, 0)),
            pl.BlockSpec((1, V_HEAD_DIM, META_PAD), lambda b, h, i: (h, 0, 0)),
        ],
        out_specs=pl.BlockSpec((1, tq, V_HEAD_DIM), lambda b, h, i: (b, i, h)),
        out_shape=jax.ShapeDtypeStruct((B, S, H * V_HEAD_DIM), BF16),
        compiler_params=_params(("parallel", "parallel", "parallel")), name="attention",
    )(qT, k, vT, k_meta, vT_meta)


def _fnet_body(a_ref, b_ref, er_ref, ei_ref, pr_ref, pi_ref, wm_ref, abm_ref, o_ref, acc_ref, *, reps):
    ci = pl.program_id(2)

    @pl.when(ci == 0)
    def _():
        mt = _dot(wm_ref[...], abm_ref[...])
        acc_ref[...] = jnp.concatenate([mt] * reps, axis=1)

    lane = lax.broadcasted_iota(jnp.int32, pr_ref.shape, 1)
    sel = lane == ci
    pr = jnp.sum(jnp.where(sel, pr_ref[...], 0.0), axis=1, keepdims=True)
    pi = jnp.sum(jnp.where(sel, pi_ref[...], 0.0), axis=1, keepdims=True)
    er = er_ref[...]
    ei = ei_ref[...]
    wr = (pr * er - pi * ei).astype(BF16)
    wi = (pr * ei + pi * er).astype(BF16)
    acc_ref[...] += _dot(wr, a_ref[...]) + _dot(wi, b_ref[...])

    @pl.when(ci == pl.num_programs(2) - 1)
    def _():
        o_ref[...] = acc_ref[...].astype(BF16)


def _fnet(a, b, tabs, abm, tm, tkf, tn):
    S, NB = a.shape
    nkt = S // tkf
    return pl.pallas_call(
        functools.partial(_fnet_body, reps=tn // FNET_DIM),
        grid=(NB // tn, S // tm, nkt),
        in_specs=[
            pl.BlockSpec((tkf, tn), lambda n, r, c: (c, n)),
            pl.BlockSpec((tkf, tn), lambda n, r, c: (c, n)),
            pl.BlockSpec((tm, tkf), lambda n, r, c: (r, 0)),
            pl.BlockSpec((tm, tkf), lambda n, r, c: (r, 0)),
            pl.BlockSpec((tm, nkt), lambda n, r, c: (r, 0)),
            pl.BlockSpec((tm, nkt), lambda n, r, c: (r, 0)),
            pl.BlockSpec((tm, 2 * N_META), lambda n, r, c: (r, 0)),
            pl.BlockSpec((2 * N_META, FNET_DIM), lambda n, r, c: (0, 0)),
        ],
        out_specs=pl.BlockSpec((tm, tn), lambda n, r, c: (r, n)),
        out_shape=jax.ShapeDtypeStruct((S, NB), BF16),
        scratch_shapes=[pltpu.VMEM((tm, tn), F32)],
        compiler_params=_params(("parallel", "parallel", "arbitrary")), name="fnet",
    )(a, b, tabs["er"], tabs["ei"], tabs["pr"], tabs["pi"], tabs["wm"], abm)


def _post_mix_body(x_ref, att_ref, f_ref, gate_ref, wo_ref, wfo_ref, wout_ref, nm_ref,
                   rhi_ref, rlo_ref, rb_ref, h1_ref, xn_ref, cw_ref):
    a = _dot(att_ref[0], wo_ref[...])
    bb = _dot(f_ref[...], wfo_ref[...])
    g = gate_ref[0].astype(F32)
    merged = g[:, :D_MODEL] * a + g[:, D_MODEL:] * bb
    h1 = x_ref[0] + _dot(merged.astype(BF16), wout_ref[...])
    h1_ref[0] = h1
    xn = _rms(h1, nm_ref[...])
    xh = xn.astype(BF16)
    xn_ref[0] = xh

    xl = (xn - xh.astype(F32)).astype(BF16)
    logits = _dot(xh, rhi_ref[...]) + (_dot(xh, rlo_ref[...]) + _dot(xl, rhi_ref[...])) + rb_ref[...]
    lane = lax.broadcasted_iota(jnp.int32, logits.shape, 1)
    lanef = lane.astype(F32)
    far = float(2 * LANES)

    gl = jnp.where((lane >= N_EXPERTS) & (lane < N_EXPERTS + N_EXPERT_GROUPS), logits, NEG_BIG)
    gmax = jnp.max(gl, axis=1, keepdims=True)
    g_w = 1.0 / jnp.sum(jnp.exp(gl - gmax), axis=1, keepdims=True)
    gidx = jnp.min(jnp.where(gl == gmax, lanef, far), axis=1, keepdims=True) - float(N_EXPERTS)

    grp = (lane >> 3).astype(F32)
    el = jnp.where((lane < N_EXPERTS) & (grp == gidx), logits, NEG_BIG)
    m1 = jnp.max(el, axis=1, keepdims=True)
    i1 = jnp.min(jnp.where(el == m1, lanef, far), axis=1, keepdims=True)
    el2 = jnp.where(lanef == i1, NEG_BIG, el)
    m2 = jnp.max(el2, axis=1, keepdims=True)
    i2 = jnp.min(jnp.where(el2 == m2, lanef, far), axis=1, keepdims=True)
    r = jnp.exp(m2 - m1)
    w1 = 1.0 / (1.0 + r)
    w2 = r / (1.0 + r)
    cw_ref[0] = g_w * jnp.where(lanef == i1, w1, jnp.where(lanef == i2, w2, 0.0))


def _post_mix(x, att, f, gates, w, ts):
    B, S, _ = x.shape
    const = lambda shape: pl.BlockSpec(shape, lambda b, t: (0,) * len(shape))
    tok = lambda width: pl.BlockSpec((1, ts, width), lambda b, t: (b, t, 0))
    return pl.pallas_call(
        _post_mix_body, grid=(B, S // ts),
        in_specs=[
            tok(D_MODEL), tok(D_MODEL),
            pl.BlockSpec((ts, FNET_DIM), lambda b, t: (t, b)),
            tok(2 * D_MODEL),
            const((D_MODEL, D_MODEL)), const((FNET_DIM, D_MODEL)), const((D_MODEL, D_MODEL)),
            const((1, D_MODEL)), const((D_MODEL, LANES)), const((D_MODEL, LANES)), const((1, LANES)),
        ],
        out_specs=(tok(D_MODEL), tok(D_MODEL), tok(LANES)),
        out_shape=(
            jax.ShapeDtypeStruct((B, S, D_MODEL), F32),
            jax.ShapeDtypeStruct((B, S, D_MODEL), BF16),
            jax.ShapeDtypeStruct((B, S, LANES), F32),
        ),
        compiler_params=_params(("parallel", "parallel")), name="post_mix",
    )(x, att, f, gates, w["w_o"], w["w_fo"], w["w_out"], w["norm_moe"], w["r_hi"], w["r_lo"], w["r_b"])


def _moe_body(xn_ref, cw_ref, h1_ref, wg_ref, wu_ref, wd_ref, nf_ref, o_ref, acc_ref):
    e = pl.program_id(1)

    @pl.when(e == 0)
    def _():
        acc_ref[...] = jnp.zeros_like(acc_ref)

    x = xn_ref[...]
    gt = _dot(x, wg_ref[0])
    up = _dot(x, wu_ref[0])
    lane = lax.broadcasted_iota(jnp.int32, cw_ref.shape, 1)
    c = jnp.sum(jnp.where(lane == e, cw_ref[...], 0.0), axis=1, keepdims=True)
    act = (gt / (1.0 + jnp.exp(-gt))) * up * c
    acc_ref[...] += _dot(act.astype(BF16), wd_ref[0])

    @pl.when(e == pl.num_programs(1) - 1)
    def _():
        o_ref[...] = _rms(h1_ref[...] + acc_ref[...], nf_ref[...])


def _moe(xn, cw, h1, w, tt):
    N = xn.shape[0]
    tok = lambda width: pl.BlockSpec((tt, width), lambda t, e: (t, 0))
    return pl.pallas_call(
        _moe_body, grid=(N // tt, N_EXPERTS),
        in_specs=[
            tok(D_MODEL), tok(LANES), tok(D_MODEL),
            pl.BlockSpec((1, D_MODEL, D_FF_EXPERT), lambda t, e: (e, 0, 0)),
            pl.BlockSpec((1, D_MODEL, D_FF_EXPERT), lambda t, e: (e, 0, 0)),
            pl.BlockSpec((1, D_FF_EXPERT, D_MODEL), lambda t, e: (e, 0, 0)),
            pl.BlockSpec((1, D_MODEL), lambda t, e: (0, 0)),
        ],
        out_specs=tok(D_MODEL),
        out_shape=jax.ShapeDtypeStruct((N, D_MODEL), F32),
        scratch_shapes=[pltpu.VMEM((tt, D_MODEL), F32)],
        compiler_params=_params(("parallel", "arbitrary")), name="moe",
    )(xn, cw, h1, w["w_eg"], w["w_eu"], w["w_ed"], w["norm_final"])


def _rope_tables(pos):
    inv_freq = 1.0 / (ROPE_THETA ** (jnp.arange(ROPE_HALF, dtype=F32) / ROPE_HALF))
    ang = pos.astype(F32)[:, None] * inv_freq[None, :]
    c, s = jnp.cos(ang), jnp.sin(ang)
    z = jnp.zeros((pos.shape[0], LANES - QK_ROPE_DIM), F32)
    return {
        "cos_nat": jnp.concatenate([c, c, z], axis=1), "sin_nat": jnp.concatenate([s, s, z], axis=1),
        "cosT": c.T, "sinT": s.T,
    }


def _unit(k, n):
    ang = k.astype(F32) * (-2.0 * math.pi / n)
    return jnp.cos(ang), jnp.sin(ang)


def _fnet_tables(S, tkf):
    L = S + N_META
    j = jnp.arange(S, dtype=jnp.int32)[:, None] + N_META
    er, ei = _unit((j * jnp.arange(tkf, dtype=jnp.int32)[None, :]) % L, L)
    col0 = jnp.arange(S // tkf, dtype=jnp.int32)[None, :] * tkf + N_META
    pr, pi = _unit((j * col0) % L, L)
    mr, mi = _unit((j * jnp.arange(N_META, dtype=jnp.int32)[None, :]) % L, L)
    return {"er": er, "ei": ei, "pr": pr, "pi": pi, "wm": jnp.concatenate([mr, mi], axis=1).astype(BF16)}


def _channel_dft(L):
    c = jnp.arange(FNET_GROUP_DIM, dtype=jnp.int32)
    cr, ci = _unit((c[:, None] * c[None, :]) % FNET_GROUP_DIM, FNET_GROUP_DIM)
    norm = 1.0 / math.sqrt(L * FNET_GROUP_DIM)
    eye = jnp.eye(N_FNET_GROUPS, dtype=F32)
    return (jnp.kron(eye, cr * norm).astype(BF16), jnp.kron(eye, -ci * norm).astype(BF16))


def _prep_weights(norm_attn, w_in, g_q, w_uq, g_kv, w_uk, w_uv, w_o_mla, w_fo, w_out, norm_moe,
                  w_rg, b_rg, w_re, b_re, w_eg, w_eu, w_ed, norm_final):
    s1 = Q_LORA_RANK
    s2 = s1 + KV_LORA_RANK
    s3 = s2 + QK_ROPE_DIM
    s4 = s3 + FNET_DIM
    w_in_p = jnp.concatenate(
        [w_in[:, :s1], w_in[:, s1:s2], w_in[:, s3:s4], w_in[:, s4:], w_in[:, s2:s3],
         jnp.zeros((D_MODEL, LANES - QK_ROPE_DIM), w_in.dtype)], axis=1)
    r = jnp.concatenate([w_re, w_rg, jnp.zeros((D_MODEL, LANES - N_EXPERTS - N_EXPERT_GROUPS), F32)], axis=1)
    r_hi = r.astype(BF16)
    r_b = jnp.concatenate([b_re, b_rg, jnp.zeros((LANES - N_EXPERTS - N_EXPERT_GROUPS,), F32)])[None, :]
    return {
        "norm_attn": norm_attn[None, :], "w_in": w_in_p.astype(BF16),
        "g_q": g_q[None, :], "w_uqT": w_uq.reshape(Q_LORA_RANK, N_HEADS * QK_DIM).T.astype(BF16),
        "g_kv": g_kv[None, :], "w_uk": w_uk.reshape(KV_LORA_RANK, N_HEADS * QK_NOPE_DIM).astype(BF16),
        "w_uvT": w_uv.reshape(KV_LORA_RANK, N_HEADS * V_HEAD_DIM).T.astype(BF16),
        "w_o": w_o_mla.astype(BF16), "w_fo": w_fo.astype(BF16), "w_out": w_out.astype(BF16),
        "norm_moe": norm_moe[None, :], "r_hi": r_hi, "r_lo": (r - r_hi.astype(F32)).astype(BF16), "r_b": r_b,
        "w_eg": w_eg.reshape(N_EXPERTS, D_MODEL, D_FF_EXPERT).astype(BF16),
        "w_eu": w_eu.reshape(N_EXPERTS, D_MODEL, D_FF_EXPERT).astype(BF16),
        "w_ed": w_ed.reshape(N_EXPERTS, D_FF_EXPERT, D_MODEL).astype(BF16),
        "norm_final": norm_final[None, :],
    }


def _tiles(B, S):
    ts = min(512, S)
    return {
        "ts": ts, "tq": min(512, S), "tm": min(1024, S), "tkf": min(512, S),
        "tn": min(2048, B * FNET_DIM), "tt": min(1024, B * S),
    }


def _trunk(x, meta, w):
    B, S, _ = x.shape
    L = S + N_META
    t = _tiles(B, S)
    wl = dict(w)
    wl["cc"], wl["sc"] = _channel_dft(L)

    meta_pad = jnp.zeros((1, META_PAD, D_MODEL), F32).at[0, :N_META].set(meta)
    _, k_m, vT_m, a_m, b_m, _ = _in_proj(meta_pad, wl, _rope_tables(jnp.arange(META_PAD)), META_PAD)
    qT, k, vT, a, b, gates = _in_proj(x, wl, _rope_tables(jnp.arange(S) + N_META), t["ts"])

    att = _attention(qT, k, vT, k_m[0], vT_m[0, :, 0], t["tq"])
    abm = jnp.concatenate([a_m[:N_META], b_m[:N_META]], axis=0)
    f = _fnet(a, b, _fnet_tables(S, t["tkf"]), abm, t["tm"], t["tkf"], t["tn"])
    h1, xn, cw = _post_mix(x, att, f, gates, wl, t["ts"])
    y = _moe(xn.reshape(B * S, D_MODEL), cw.reshape(B * S, LANES), h1.reshape(B * S, D_MODEL), wl, t["tt"])
    return y.reshape(B, S, D_MODEL)


def kernel(x_prompt, x_sample, meta_tokens, norm_attn, w_in, g_q, w_uq, g_kv, w_uk, w_uv, w_o_mla, w_fo, w_out,
           norm_moe, w_router_group, b_router_group, w_router_expert, b_router_expert, w_exp_gate, w_exp_up,
           w_exp_down, norm_final):
    w = _prep_weights(norm_attn[0], w_in[0], g_q[0], w_uq[0], g_kv[0], w_uk[0], w_uv[0], w_o_mla[0], w_fo[0],
                      w_out[0], norm_moe[0], w_router_group[0], b_router_group[0], w_router_expert[0],
                      b_router_expert[0], w_exp_gate[0], w_exp_up[0], w_exp_down[0], norm_final)
    return _trunk(x_prompt, meta_tokens, w), _trunk(x_sample, meta_tokens, w)
```

```python
import functools
import math

import jax
import jax.numpy as jnp
from jax import lax
from jax.experimental import pallas as pl
from jax.experimental.pallas import tpu as pltpu

D_MODEL = 1024
N_META = 16
N_HEADS = 8
QK_NOPE_DIM = 128
QK_ROPE_DIM = 64
ROPE_HALF = QK_ROPE_DIM // 2
QK_DIM = QK_NOPE_DIM + QK_ROPE_DIM
V_HEAD_DIM = 128
Q_LORA_RANK = 384
KV_LORA_RANK = 256
ROPE_THETA = 10000.0
N_FNET_GROUPS = 4
FNET_GROUP_DIM = 128
FNET_DIM = N_FNET_GROUPS * FNET_GROUP_DIM
N_EXPERT_GROUPS = 4
EXPERTS_PER_GROUP = 8
N_EXPERTS = N_EXPERT_GROUPS * EXPERTS_PER_GROUP
D_FF_EXPERT = 256
EPS = 1e-6

LANES = 128
META_PAD = 128
VMEM_LIMIT_BYTES = 56 * 1024 * 1024
NEG_BIG = -1e30

_CQ0, _CQ1 = 0, Q_LORA_RANK
_CKV0, _CKV1 = _CQ1, _CQ1 + KV_LORA_RANK
_UF0, _UF1 = _CKV1, _CKV1 + FNET_DIM
_G0, _G1 = _UF1, _UF1 + 2 * D_MODEL
_KR0, _KR1 = _G1, _G1 + LANES
IN_DIM_PADDED = _KR1

F32 = jnp.float32
BF16 = jnp.bfloat16


def _dot(a, b):
    return jnp.dot(a, b, preferred_element_type=F32)


def _rms(x, g):
    ms = jnp.mean(x * x, axis=-1, keepdims=True)
    return x * lax.rsqrt(ms + EPS) * g


def _params(semantics):
    return pltpu.CompilerParams(dimension_semantics=semantics, vmem_limit_bytes=VMEM_LIMIT_BYTES)


def _in_proj_body(x_ref, na_ref, win_ref, gq_ref, wuqT_ref, gkv_ref, wuk_ref, wuvT_ref,
                  cosn_ref, sinn_ref, cosT_ref, sinT_ref, cc_ref, sc_ref,
                  qT_ref, k_ref, vT_ref, a_ref, b_ref, gate_ref):
    x = x_ref[0]
    hb = _rms(x, na_ref[...]).astype(BF16)

    cqn = _rms(_dot(hb, win_ref[:, _CQ0:_CQ1]), gq_ref[...])
    qT = _dot(wuqT_ref[...], cqn.T.astype(BF16))
    cT = cosT_ref[...]
    sT = sinT_ref[...]
    scale = math.log2(math.e) / math.sqrt(QK_DIM)
    for h in range(N_HEADS):
        r0 = h * QK_DIM
        r1 = r0 + QK_NOPE_DIM
        r2 = r1 + ROPE_HALF
        x1 = qT[r1:r2]
        x2 = qT[r2:r0 + QK_DIM]
        qT_ref[0, h, 0:QK_NOPE_DIM, :] = (qT[r0:r1] * scale).astype(BF16)
        qT_ref[0, h, QK_NOPE_DIM:QK_NOPE_DIM + ROPE_HALF, :] = ((x1 * cT - x2 * sT) * scale).astype(BF16)
        qT_ref[0, h, QK_NOPE_DIM + ROPE_HALF:QK_DIM, :] = ((x2 * cT + x1 * sT) * scale).astype(BF16)

    ckvn = _rms(_dot(hb, win_ref[:, _CKV0:_CKV1]), gkv_ref[...])
    kn = _dot(ckvn.astype(BF16), wuk_ref[...])
    kr = _dot(hb, win_ref[:, _KR0:_KR1])
    lane = lax.broadcasted_iota(jnp.int32, kr.shape, 1)
    kr_rot = jnp.where(lane < ROPE_HALF, -pltpu.roll(kr, LANES - ROPE_HALF, 1), pltpu.roll(kr, ROPE_HALF, 1))
    krope = (kr * cosn_ref[...] + kr_rot * sinn_ref[...]).astype(BF16)
    for h in range(N_HEADS):
        k_ref[0, h, :, 0:QK_NOPE_DIM] = kn[:, h * QK_NOPE_DIM:(h + 1) * QK_NOPE_DIM].astype(BF16)
        k_ref[0, h, :, QK_NOPE_DIM:QK_DIM] = krope[:, 0:QK_ROPE_DIM]
    vT = _dot(wuvT_ref[...], ckvn.T.astype(BF16))
    for h in range(N_HEADS):
        vT_ref[0, h, 0] = vT[h * V_HEAD_DIM:(h + 1) * V_HEAD_DIM].astype(BF16)

    ub = _dot(hb, win_ref[:, _UF0:_UF1]).astype(BF16)
    a_ref[...] = _dot(ub, cc_ref[...]).astype(BF16)
    b_ref[...] = _dot(ub, sc_ref[...]).astype(BF16)

    gl = _dot(hb, win_ref[:, _G0:_G1])
    gate_ref[0] = (1.0 / (1.0 + jnp.exp(-gl))).astype(BF16)


def _in_proj(x, w, rope, ts):
    B, S, _ = x.shape
    nt = S // ts
    const = lambda shape: pl.BlockSpec(shape, lambda b, t: (0,) * len(shape))
    out_shape = (
        jax.ShapeDtypeStruct((B, N_HEADS, QK_DIM, S), BF16),
        jax.ShapeDtypeStruct((B, N_HEADS, S, QK_DIM), BF16),
        jax.ShapeDtypeStruct((B, N_HEADS, nt, V_HEAD_DIM, ts), BF16),
        jax.ShapeDtypeStruct((S, B * FNET_DIM), BF16),
        jax.ShapeDtypeStruct((S, B * FNET_DIM), BF16),
        jax.ShapeDtypeStruct((B, S, 2 * D_MODEL), BF16),
    )
    in_specs = [
        pl.BlockSpec((1, ts, D_MODEL), lambda b, t: (b, t, 0)),
        const((1, D_MODEL)),
        const((D_MODEL, IN_DIM_PADDED)),
        const((1, Q_LORA_RANK)),
        const((N_HEADS * QK_DIM, Q_LORA_RANK)),
        const((1, KV_LORA_RANK)),
        const((KV_LORA_RANK, N_HEADS * QK_NOPE_DIM)),
        const((N_HEADS * V_HEAD_DIM, KV_LORA_RANK)),
        pl.BlockSpec((ts, LANES), lambda b, t: (t, 0)),
        pl.BlockSpec((ts, LANES), lambda b, t: (t, 0)),
        pl.BlockSpec((ROPE_HALF, ts), lambda b, t: (0, t)),
        pl.BlockSpec((ROPE_HALF, ts), lambda b, t: (0, t)),
        const((FNET_DIM, FNET_DIM)),
        const((FNET_DIM, FNET_DIM)),
    ]
    out_specs = (
        pl.BlockSpec((1, N_HEADS, QK_DIM, ts), lambda b, t: (b, 0, 0, t)),
        pl.BlockSpec((1, N_HEADS, ts, QK_DIM), lambda b, t: (b, 0, t, 0)),
        pl.BlockSpec((1, N_HEADS, 1, V_HEAD_DIM, ts), lambda b, t: (b, 0, t, 0, 0)),
        pl.BlockSpec((ts, FNET_DIM), lambda b, t: (t, b)),
        pl.BlockSpec((ts, FNET_DIM), lambda b, t: (t, b)),
        pl.BlockSpec((1, ts, 2 * D_MODEL), lambda b, t: (b, t, 0)),
    )
    return pl.pallas_call(
        _in_proj_body, grid=(B, nt), in_specs=in_specs, out_specs=out_specs, out_shape=out_shape,
        compiler_params=_params(("parallel", "parallel")), name="in_proj",
    )(x, w["norm_attn"], w["w_in"], w["g_q"], w["w_uqT"], w["g_kv"], w["w_uk"], w["w_uvT"],
      rope["cos_nat"], rope["sin_nat"], rope["cosT"], rope["sinT"], w["cc"], w["sc"])


def _attn_body(qT_ref, k_ref, vT_ref, km_ref, vmT_ref, o_ref, s_ref, p_ref, acc_ref, *, nk, tk):
    q = qT_ref[0, 0]

    def scores(j, slot):
        s_ref[slot] = _dot(k_ref[0, 0, pl.ds(pl.multiple_of(j * tk, tk), tk), :], q)

    def values(j, slot, alpha):
        acc_ref[...] = alpha * acc_ref[...] + _dot(vT_ref[0, 0, j], p_ref[slot])

    def softmax(slot, m, l):
        s = s_ref[slot]
        m_new = jnp.maximum(m, jnp.max(s, axis=0, keepdims=True))
        alpha = jnp.exp2(m - m_new)
        p = jnp.exp2(s - m_new)
        p_ref[slot] = p.astype(BF16)
        return m_new, alpha * l + jnp.sum(p, axis=0, keepdims=True), alpha

    s = _dot(km_ref[0], q)
    row = lax.broadcasted_iota(jnp.int32, s.shape, 0)
    s = jnp.where(row < N_META, s, NEG_BIG)
    m = jnp.max(s, axis=0, keepdims=True)
    p = jnp.exp2(s - m)
    l = jnp.sum(p, axis=0, keepdims=True)
    acc_ref[...] = _dot(vmT_ref[0], p.astype(BF16))

    p_ref[1] = jnp.zeros(p_ref.shape[1:], BF16)
    scores(0, 0)

    def pair(jj, carry, last):
        m, l, alpha = carry
        a = 2 * jj
        values(jnp.maximum(a - 1, 0), 1, alpha)
        scores(a + 1, 1)
        m, l, alpha = softmax(0, m, l)
        values(a, 0, alpha)
        if not last:
            scores(a + 2, 0)
        return softmax(1, m, l)

    carry = (m, l, jnp.ones_like(m))
    carry = lax.fori_loop(0, nk // 2 - 1, functools.partial(pair, last=False), carry)
    m, l, alpha = pair(nk // 2 - 1, carry, last=True)
    values(nk - 1, 1, alpha)
    o_ref[0] = (acc_ref[...] / l).T.astype(BF16)


def _attention(qT, k, vT, k_meta, vT_meta, tq):
    B, H, _, S = qT.shape
    nk, tk = vT.shape[2], vT.shape[4]
    assert nk % 2 == 0, "the key loop is unrolled in pairs"
    return pl.pallas_call(
        functools.partial(_attn_body, nk=nk, tk=tk),
        grid=(B, H, S // tq),
        in_specs=[
            pl.BlockSpec((1, 1, QK_DIM, tq), lambda b, h, i: (b, h, 0, i)),
            pl.BlockSpec((1, 1, S, QK_DIM), lambda b, h, i: (b, h, 0, 0)),
            pl.BlockSpec((1, 1, nk, V_HEAD_DIM, tk), lambda b, h, i: (b, h, 0, 0, 0)),
            pl.BlockSpec((1, META_PAD, QK_DIM), lambda b, h, i: (h, 0, 0)),
            pl.BlockSpec((1, V_HEAD_DIM, META_PAD), lambda b, h, i: (h, 0, 0)),
        ],
        out_specs=pl.BlockSpec((1, tq, V_HEAD_DIM), lambda b, h, i: (b, i, h)),
        out_shape=jax.ShapeDtypeStruct((B, S, H * V_HEAD_DIM), BF16),
        scratch_shapes=[
            pltpu.VMEM((2, tk, tq), F32), pltpu.VMEM((2, tk, tq), BF16), pltpu.VMEM((V_HEAD_DIM, tq), F32),
        ],
        compiler_params=_params(("parallel", "parallel", "parallel")), name="attention",
    )(qT, k, vT, k_meta, vT_meta)


def _fnet_body(a_ref, b_ref, er_ref, ei_ref, pr_ref, pi_ref, wm_ref, abm_ref, o_ref, acc_ref, *, reps):
    ci = pl.program_id(2)

    @pl.when(ci == 0)
    def _():
        mt = _dot(wm_ref[...], abm_ref[...])
        acc_ref[...] = jnp.concatenate([mt] * reps, axis=1)

    lane = lax.broadcasted_iota(jnp.int32, pr_ref.shape, 1)
    sel = lane == ci
    pr = jnp.sum(jnp.where(sel, pr_ref[...], 0.0), axis=1, keepdims=True)
    pi = jnp.sum(jnp.where(sel, pi_ref[...], 0.0), axis=1, keepdims=True)
    er = er_ref[...]
    ei = ei_ref[...]
    wr = (pr * er - pi * ei).astype(BF16)
    wi = (pr * ei + pi * er).astype(BF16)
    acc_ref[...] += _dot(wr, a_ref[...]) + _dot(wi, b_ref[...])

    @pl.when(ci == pl.num_programs(2) - 1)
    def _():
        o_ref[...] = acc_ref[...].astype(BF16)


def _fnet(a, b, tabs, abm, tm, tkf, tn):
    S, NB = a.shape
    nkt = S // tkf
    return pl.pallas_call(
        functools.partial(_fnet_body, reps=tn // FNET_DIM),
        grid=(NB // tn, S // tm, nkt),
        in_specs=[
            pl.BlockSpec((tkf, tn), lambda n, r, c: (c, n)),
            pl.BlockSpec((tkf, tn), lambda n, r, c: (c, n)),
            pl.BlockSpec((tm, tkf), lambda n, r, c: (r, 0)),
            pl.BlockSpec((tm, tkf), lambda n, r, c: (r, 0)),
            pl.BlockSpec((tm, nkt), lambda n, r, c: (r, 0)),
            pl.BlockSpec((tm, nkt), lambda n, r, c: (r, 0)),
            pl.BlockSpec((tm, 2 * N_META), lambda n, r, c: (r, 0)),
            pl.BlockSpec((2 * N_META, FNET_DIM), lambda n, r, c: (0, 0)),
        ],
        out_specs=pl.BlockSpec((tm, tn), lambda n, r, c: (r, n)),
        out_shape=jax.ShapeDtypeStruct((S, NB), BF16),
        scratch_shapes=[pltpu.VMEM((tm, tn), F32)],
        compiler_params=_params(("parallel", "parallel", "arbitrary")), name="fnet",
    )(a, b, tabs["er"], tabs["ei"], tabs["pr"], tabs["pi"], tabs["wm"], abm)


def _post_mix_body(x_ref, att_ref, f_ref, gate_ref, wo_ref, wfo_ref, wout_ref, nm_ref,
                   rhi_ref, rlo_ref, rb_ref, h1_ref, xn_ref, cw_ref):
    a = _dot(att_ref[0], wo_ref[...])
    bb = _dot(f_ref[...], wfo_ref[...])
    g = gate_ref[0].astype(F32)
    merged = g[:, :D_MODEL] * a + g[:, D_MODEL:] * bb
    h1 = x_ref[0] + _dot(merged.astype(BF16), wout_ref[...])
    h1_ref[0] = h1
    xn = _rms(h1, nm_ref[...])
    xh = xn.astype(BF16)
    xn_ref[0] = xh

    xl = (xn - xh.astype(F32)).astype(BF16)
    logits = _dot(xh, rhi_ref[...]) + (_dot(xh, rlo_ref[...]) + _dot(xl, rhi_ref[...])) + rb_ref[...]
    lane = lax.broadcasted_iota(jnp.int32, logits.shape, 1)
    lanef = lane.astype(F32)
    far = float(2 * LANES)

    gl = jnp.where((lane >= N_EXPERTS) & (lane < N_EXPERTS + N_EXPERT_GROUPS), logits, NEG_BIG)
    gmax = jnp.max(gl, axis=1, keepdims=True)
    g_w = 1.0 / jnp.sum(jnp.exp(gl - gmax), axis=1, keepdims=True)
    gidx = jnp.min(jnp.where(gl == gmax, lanef, far), axis=1, keepdims=True) - float(N_EXPERTS)

    grp = (lane >> 3).astype(F32)
    el = jnp.where((lane < N_EXPERTS) & (grp == gidx), logits, NEG_BIG)
    m1 = jnp.max(el, axis=1, keepdims=True)
    i1 = jnp.min(jnp.where(el == m1, lanef, far), axis=1, keepdims=True)
    el2 = jnp.where(lanef == i1, NEG_BIG, el)
    m2 = jnp.max(el2, axis=1, keepdims=True)
    i2 = jnp.min(jnp.where(el2 == m2, lanef, far), axis=1, keepdims=True)
    r = jnp.exp(m2 - m1)
    w1 = 1.0 / (1.0 + r)
    w2 = r / (1.0 + r)
    cw_ref[0] = g_w * jnp.where(lanef == i1, w1, jnp.where(lanef == i2, w2, 0.0))


def _post_mix(x, att, f, gates, w, ts):
    B, S, _ = x.shape
    const = lambda shape: pl.BlockSpec(shape, lambda b, t: (0,) * len(shape))
    tok = lambda width: pl.BlockSpec((1, ts, width), lambda b, t: (b, t, 0))
    return pl.pallas_call(
        _post_mix_body, grid=(B, S // ts),
        in_specs=[
            tok(D_MODEL), tok(D_MODEL),
            pl.BlockSpec((ts, FNET_DIM), lambda b, t: (t, b)),
            tok(2 * D_MODEL),
            const((D_MODEL, D_MODEL)), const((FNET_DIM, D_MODEL)), const((D_MODEL, D_MODEL)),
            const((1, D_MODEL)), const((D_MODEL, LANES)), const((D_MODEL, LANES)), const((1, LANES)),
        ],
        out_specs=(tok(D_MODEL), tok(D_MODEL), tok(LANES)),
        out_shape=(
            jax.ShapeDtypeStruct((B, S, D_MODEL), F32),
            jax.ShapeDtypeStruct((B, S, D_MODEL), BF16),
            jax.ShapeDtypeStruct((B, S, LANES), F32),
        ),
        compiler_params=_params(("parallel", "parallel")), name="post_mix",
    )(x, att, f, gates, w["w_o"], w["w_fo"], w["w_out"], w["norm_moe"], w["r_hi"], w["r_lo"], w["r_b"])


def _moe_body(xn_ref, cw_ref, h1_ref, wg_ref, wu_ref, wd_ref, nf_ref, o_ref, acc_ref):
    e = pl.program_id(1)

    @pl.when(e == 0)
    def _():
        acc_ref[...] = jnp.zeros_like(acc_ref)

    x = xn_ref[...]
    gt = _dot(x, wg_ref[0])
    up = _dot(x, wu_ref[0])
    lane = lax.broadcasted_iota(jnp.int32, cw_ref.shape, 1)
    c = jnp.sum(jnp.where(lane == e, cw_ref[...], 0.0), axis=1, keepdims=True)
    act = (gt / (1.0 + jnp.exp(-gt))) * up * c
    acc_ref[...] += _dot(act.astype(BF16), wd_ref[0])

    @pl.when(e == pl.num_programs(1) - 1)
    def _():
        o_ref[...] = _rms(h1_ref[...] + acc_ref[...], nf_ref[...])


def _moe(xn, cw, h1, w, tt):
    N = xn.shape[0]
    tok = lambda width: pl.BlockSpec((tt, width), lambda t, e: (t, 0))
    return pl.pallas_call(
        _moe_body, grid=(N // tt, N_EXPERTS),
        in_specs=[
            tok(D_MODEL), tok(LANES), tok(D_MODEL),
            pl.BlockSpec((1, D_MODEL, D_FF_EXPERT), lambda t, e: (e, 0, 0)),
            pl.BlockSpec((1, D_MODEL, D_FF_EXPERT), lambda t, e: (e, 0, 0)),
            pl.BlockSpec((1, D_FF_EXPERT, D_MODEL), lambda t, e: (e, 0, 0)),
            pl.BlockSpec((1, D_MODEL), lambda t, e: (0, 0)),
        ],
        out_specs=tok(D_MODEL),
        out_shape=jax.ShapeDtypeStruct((N, D_MODEL), F32),
        scratch_shapes=[pltpu.VMEM((tt, D_MODEL), F32)],
        compiler_params=_params(("parallel", "arbitrary")), name="moe",
    )(xn, cw, h1, w["w_eg"], w["w_eu"], w["w_ed"], w["norm_final"])


def _rope_tables(pos):
    inv_freq = 1.0 / (ROPE_THETA ** (jnp.arange(ROPE_HALF, dtype=F32) / ROPE_HALF))
    ang = pos.astype(F32)[:, None] * inv_freq[None, :]
    c, s = jnp.cos(ang), jnp.sin(ang)
    z = jnp.zeros((pos.shape[0], LANES - QK_ROPE_DIM), F32)
    return {
        "cos_nat": jnp.concatenate([c, c, z], axis=1), "sin_nat": jnp.concatenate([s, s, z], axis=1),
        "cosT": c.T, "sinT": s.T,
    }


def _unit(k, n):
    ang = k.astype(F32) * (-2.0 * math.pi / n)
    return jnp.cos(ang), jnp.sin(ang)


def _fnet_tables(S, tkf):
    L = S + N_META
    j = jnp.arange(S, dtype=jnp.int32)[:, None] + N_META
    er, ei = _unit((j * jnp.arange(tkf, dtype=jnp.int32)[None, :]) % L, L)
    col0 = jnp.arange(S // tkf, dtype=jnp.int32)[None, :] * tkf + N_META
    pr, pi = _unit((j * col0) % L, L)
    mr, mi = _unit((j * jnp.arange(N_META, dtype=jnp.int32)[None, :]) % L, L)
    return {"er": er, "ei": ei, "pr": pr, "pi": pi, "wm": jnp.concatenate([mr, mi], axis=1).astype(BF16)}


def _channel_dft(L):
    c = jnp.arange(FNET_GROUP_DIM, dtype=jnp.int32)
    cr, ci = _unit((c[:, None] * c[None, :]) % FNET_GROUP_DIM, FNET_GROUP_DIM)
    norm = 1.0 / math.sqrt(L * FNET_GROUP_DIM)
    eye = jnp.eye(N_FNET_GROUPS, dtype=F32)
    return (jnp.kron(eye, cr * norm).astype(BF16), jnp.kron(eye, -ci * norm).astype(BF16))


def _prep_weights(norm_attn, w_in, g_q, w_uq, g_kv, w_uk, w_uv, w_o_mla, w_fo, w_out, norm_moe,
                  w_rg, b_rg, w_re, b_re, w_eg, w_eu, w_ed, norm_final):
    s1 = Q_LORA_RANK
    s2 = s1 + KV_LORA_RANK
    s3 = s2 + QK_ROPE_DIM
    s4 = s3 + FNET_DIM
    w_in_p = jnp.concatenate(
        [w_in[:, :s1], w_in[:, s1:s2], w_in[:, s3:s4], w_in[:, s4:], w_in[:, s2:s3],
         jnp.zeros((D_MODEL, LANES - QK_ROPE_DIM), w_in.dtype)], axis=1)
    r = jnp.concatenate([w_re, w_rg, jnp.zeros((D_MODEL, LANES - N_EXPERTS - N_EXPERT_GROUPS), F32)], axis=1)
    r_hi = r.astype(BF16)
    r_b = jnp.concatenate([b_re, b_rg, jnp.zeros((LANES - N_EXPERTS - N_EXPERT_GROUPS,), F32)])[None, :]
    return {
        "norm_attn": norm_attn[None, :], "w_in": w_in_p.astype(BF16),
        "g_q": g_q[None, :], "w_uqT": w_uq.reshape(Q_LORA_RANK, N_HEADS * QK_DIM).T.astype(BF16),
        "g_kv": g_kv[None, :], "w_uk": w_uk.reshape(KV_LORA_RANK, N_HEADS * QK_NOPE_DIM).astype(BF16),
        "w_uvT": w_uv.reshape(KV_LORA_RANK, N_HEADS * V_HEAD_DIM).T.astype(BF16),
        "w_o": w_o_mla.astype(BF16), "w_fo": w_fo.astype(BF16), "w_out": w_out.astype(BF16),
        "norm_moe": norm_moe[None, :], "r_hi": r_hi, "r_lo": (r - r_hi.astype(F32)).astype(BF16), "r_b": r_b,
        "w_eg": w_eg.reshape(N_EXPERTS, D_MODEL, D_FF_EXPERT).astype(BF16),
        "w_eu": w_eu.reshape(N_EXPERTS, D_MODEL, D_FF_EXPERT).astype(BF16),
        "w_ed": w_ed.reshape(N_EXPERTS, D_FF_EXPERT, D_MODEL).astype(BF16),
        "norm_final": norm_final[None, :],
    }


def _tiles(B, S):
    ts = min(512, S // 2)
    return {
        "ts": ts, "tq": min(512, S), "tm": min(1024, S), "tkf": min(512, S),
        "tn": min(2048, B * FNET_DIM), "tt": min(1024, B * S),
    }


def _trunk(x, meta, w):
    B, S, _ = x.shape
    L = S + N_META
    t = _tiles(B, S)
    wl = dict(w)
    wl["cc"], wl["sc"] = _channel_dft(L)

    meta_pad = jnp.zeros((1, META_PAD, D_MODEL), F32).at[0, :N_META].set(meta)
    _, k_m, vT_m, a_m, b_m, _ = _in_proj(meta_pad, wl, _rope_tables(jnp.arange(META_PAD)), META_PAD)
    qT, k, vT, a, b, gates = _in_proj(x, wl, _rope_tables(jnp.arange(S) + N_META), t["ts"])

    att = _attention(qT, k, vT, k_m[0], vT_m[0, :, 0], t["tq"])
    abm = jnp.concatenate([a_m[:N_META], b_m[:N_META]], axis=0)
    f = _fnet(a, b, _fnet_tables(S, t["tkf"]), abm, t["tm"], t["tkf"], t["tn"])
    h1, xn, cw = _post_mix(x, att, f, gates, wl, t["ts"])
    y = _moe(xn.reshape(B * S, D_MODEL), cw.reshape(B * S, LANES), h1.reshape(B * S, D_MODEL), wl, t["tt"])
    return y.reshape(B, S, D_MODEL)


def kernel(x_prompt, x_sample, meta_tokens, norm_attn, w_in, g_q, w_uq, g_kv, w_uk, w_uv, w_o_mla, w_fo, w_out,
           norm_moe, w_router_group, b_router_group, w_router_expert, b_router_expert, w_exp_gate, w_exp_up,
           w_exp_down, norm_final):
    w = _prep_weights(norm_attn[0], w_in[0], g_q[0], w_uq[0], g_kv[0], w_uk[0], w_uv[0], w_o_mla[0], w_fo[0],
                      w_out[0], norm_moe[0], w_router_group[0], b_router_group[0], w_router_expert[0],
                      b_router_expert[0], w_exp_gate[0], w_exp_up[0], w_exp_down[0], norm_final)
    return _trunk(x_prompt, meta_tokens, w), _trunk(x_sample, meta_tokens, w)
```

```python
import functools
import math

import jax
import jax.numpy as jnp
import numpy as np
from jax import lax
from jax.experimental import pallas as pl
from jax.experimental.pallas import tpu as pltpu

D_MODEL = 1024
N_META = 16
N_HEADS = 8
QK_NOPE_DIM = 128
QK_ROPE_DIM = 64
ROPE_HALF = QK_ROPE_DIM // 2
QK_DIM = QK_NOPE_DIM + QK_ROPE_DIM
V_HEAD_DIM = 128
Q_LORA_RANK = 384
KV_LORA_RANK = 256
ROPE_THETA = 10000.0
N_FNET_GROUPS = 4
FNET_GROUP_DIM = 128
FNET_DIM = N_FNET_GROUPS * FNET_GROUP_DIM
N_EXPERT_GROUPS = 4
EXPERTS_PER_GROUP = 8
N_EXPERTS = N_EXPERT_GROUPS * EXPERTS_PER_GROUP
D_FF_EXPERT = 256
EPS = 1e-6
PAIRS_PER_GROUP = EXPERTS_PER_GROUP * (EXPERTS_PER_GROUP - 1) // 2
N_BUCKETS = N_EXPERT_GROUPS * PAIRS_PER_GROUP
_PAIRS = [(a, b) for a in range(EXPERTS_PER_GROUP) for b in range(a + 1, EXPERTS_PER_GROUP)]
_BUCKET_LO = np.array([g * EXPERTS_PER_GROUP + a for g in range(N_EXPERT_GROUPS) for a, _ in _PAIRS], np.int32)
_BUCKET_HI = np.array([g * EXPERTS_PER_GROUP + b for g in range(N_EXPERT_GROUPS) for _, b in _PAIRS], np.int32)

LANES = 128
META_PAD = 128
VMEM_LIMIT_BYTES = 56 * 1024 * 1024
NEG_BIG = -1e30

_CQ0, _CQ1 = 0, Q_LORA_RANK
_CKV0, _CKV1 = _CQ1, _CQ1 + KV_LORA_RANK
_UF0, _UF1 = _CKV1, _CKV1 + FNET_DIM
_G0, _G1 = _UF1, _UF1 + 2 * D_MODEL
_KR0, _KR1 = _G1, _G1 + LANES
IN_DIM_PADDED = _KR1

F32 = jnp.float32
BF16 = jnp.bfloat16


def _dot(a, b):
    return jnp.dot(a, b, preferred_element_type=F32)


def _rms(x, g):
    ms = jnp.mean(x * x, axis=-1, keepdims=True)
    return x * lax.rsqrt(ms + EPS) * g


def _params(semantics):
    return pltpu.CompilerParams(dimension_semantics=semantics, vmem_limit_bytes=VMEM_LIMIT_BYTES)


def _in_proj_body(x_ref, na_ref, win_ref, gq_ref, wuqT_ref, gkv_ref, wuk_ref, wuvT_ref,
                  cosn_ref, sinn_ref, cosT_ref, sinT_ref, cc_ref, sc_ref,
                  qT_ref, k_ref, vT_ref, a_ref, b_ref, gate_ref):
    x = x_ref[0]
    hb = _rms(x, na_ref[...]).astype(BF16)

    cqn = _rms(_dot(hb, win_ref[:, _CQ0:_CQ1]), gq_ref[...])
    qT = _dot(wuqT_ref[...], cqn.T.astype(BF16))
    cT = cosT_ref[...]
    sT = sinT_ref[...]
    scale = math.log2(math.e) / math.sqrt(QK_DIM)
    for h in range(N_HEADS):
        r0 = h * QK_DIM
        r1 = r0 + QK_NOPE_DIM
        r2 = r1 + ROPE_HALF
        x1 = qT[r1:r2]
        x2 = qT[r2:r0 + QK_DIM]
        qT_ref[0, h, 0:QK_NOPE_DIM, :] = (qT[r0:r1] * scale).astype(BF16)
        qT_ref[0, h, QK_NOPE_DIM:QK_NOPE_DIM + ROPE_HALF, :] = ((x1 * cT - x2 * sT) * scale).astype(BF16)
        qT_ref[0, h, QK_NOPE_DIM + ROPE_HALF:QK_DIM, :] = ((x2 * cT + x1 * sT) * scale).astype(BF16)

    ckvn = _rms(_dot(hb, win_ref[:, _CKV0:_CKV1]), gkv_ref[...])
    kn = _dot(ckvn.astype(BF16), wuk_ref[...])
    kr = _dot(hb, win_ref[:, _KR0:_KR1])
    lane = lax.broadcasted_iota(jnp.int32, kr.shape, 1)
    kr_rot = jnp.where(lane < ROPE_HALF, -pltpu.roll(kr, LANES - ROPE_HALF, 1), pltpu.roll(kr, ROPE_HALF, 1))
    krope = (kr * cosn_ref[...] + kr_rot * sinn_ref[...]).astype(BF16)
    for h in range(N_HEADS):
        k_ref[0, h, :, 0:QK_NOPE_DIM] = kn[:, h * QK_NOPE_DIM:(h + 1) * QK_NOPE_DIM].astype(BF16)
        k_ref[0, h, :, QK_NOPE_DIM:QK_DIM] = krope[:, 0:QK_ROPE_DIM]
    vT = _dot(wuvT_ref[...], ckvn.T.astype(BF16))
    for h in range(N_HEADS):
        vT_ref[0, h, 0] = vT[h * V_HEAD_DIM:(h + 1) * V_HEAD_DIM].astype(BF16)

    ub = _dot(hb, win_ref[:, _UF0:_UF1]).astype(BF16)
    a_ref[...] = _dot(ub, cc_ref[...]).astype(BF16)
    b_ref[...] = _dot(ub, sc_ref[...]).astype(BF16)

    gl = _dot(hb, win_ref[:, _G0:_G1])
    gate_ref[0] = (1.0 / (1.0 + jnp.exp(-gl))).astype(BF16)


def _in_proj(x, w, rope, ts):
    B, S, _ = x.shape
    nt = S // ts
    const = lambda shape: pl.BlockSpec(shape, lambda b, t: (0,) * len(shape))
    out_shape = (
        jax.ShapeDtypeStruct((B, N_HEADS, QK_DIM, S), BF16),
        jax.ShapeDtypeStruct((B, N_HEADS, S, QK_DIM), BF16),
        jax.ShapeDtypeStruct((B, N_HEADS, nt, V_HEAD_DIM, ts), BF16),
        jax.ShapeDtypeStruct((S, B * FNET_DIM), BF16),
        jax.ShapeDtypeStruct((S, B * FNET_DIM), BF16),
        jax.ShapeDtypeStruct((B, S, 2 * D_MODEL), BF16),
    )
    in_specs = [
        pl.BlockSpec((1, ts, D_MODEL), lambda b, t: (b, t, 0)),
        const((1, D_MODEL)),
        const((D_MODEL, IN_DIM_PADDED)),
        const((1, Q_LORA_RANK)),
        const((N_HEADS * QK_DIM, Q_LORA_RANK)),
        const((1, KV_LORA_RANK)),
        const((KV_LORA_RANK, N_HEADS * QK_NOPE_DIM)),
        const((N_HEADS * V_HEAD_DIM, KV_LORA_RANK)),
        pl.BlockSpec((ts, LANES), lambda b, t: (t, 0)),
        pl.BlockSpec((ts, LANES), lambda b, t: (t, 0)),
        pl.BlockSpec((ROPE_HALF, ts), lambda b, t: (0, t)),
        pl.BlockSpec((ROPE_HALF, ts), lambda b, t: (0, t)),
        const((FNET_DIM, FNET_DIM)),
        const((FNET_DIM, FNET_DIM)),
    ]
    out_specs = (
        pl.BlockSpec((1, N_HEADS, QK_DIM, ts), lambda b, t: (b, 0, 0, t)),
        pl.BlockSpec((1, N_HEADS, ts, QK_DIM), lambda b, t: (b, 0, t, 0)),
        pl.BlockSpec((1, N_HEADS, 1, V_HEAD_DIM, ts), lambda b, t: (b, 0, t, 0, 0)),
        pl.BlockSpec((ts, FNET_DIM), lambda b, t: (t, b)),
        pl.BlockSpec((ts, FNET_DIM), lambda b, t: (t, b)),
        pl.BlockSpec((1, ts, 2 * D_MODEL), lambda b, t: (b, t, 0)),
    )
    return pl.pallas_call(
        _in_proj_body, grid=(B, nt), in_specs=in_specs, out_specs=out_specs, out_shape=out_shape,
        compiler_params=_params(("parallel", "parallel")), name="in_proj",
    )(x, w["norm_attn"], w["w_in"], w["g_q"], w["w_uqT"], w["g_kv"], w["w_uk"], w["w_uvT"],
      rope["cos_nat"], rope["sin_nat"], rope["cosT"], rope["sinT"], w["cc"], w["sc"])


def _attn_body(qT_ref, k_ref, vT_ref, km_ref, vmT_ref, o_ref, s_ref, p_ref, acc_ref, *, nk, tk):
    q = qT_ref[0, 0]

    def scores(j, slot):
        s_ref[slot] = _dot(k_ref[0, 0, pl.ds(pl.multiple_of(j * tk, tk), tk), :], q)

    def values(j, slot, alpha):
        acc_ref[...] = alpha * acc_ref[...] + _dot(vT_ref[0, 0, j], p_ref[slot])

    def softmax(slot, m, l):
        s = s_ref[slot]
        m_new = jnp.maximum(m, jnp.max(s, axis=0, keepdims=True))
        alpha = jnp.exp2(m - m_new)
        p = jnp.exp2(s - m_new)
        p_ref[slot] = p.astype(BF16)
        return m_new, alpha * l + jnp.sum(p, axis=0, keepdims=True), alpha

    s = _dot(km_ref[0], q)
    row = lax.broadcasted_iota(jnp.int32, s.shape, 0)
    s = jnp.where(row < N_META, s, NEG_BIG)
    m = jnp.max(s, axis=0, keepdims=True)
    p = jnp.exp2(s - m)
    l = jnp.sum(p, axis=0, keepdims=True)
    acc_ref[...] = _dot(vmT_ref[0], p.astype(BF16))

    p_ref[1] = jnp.zeros(p_ref.shape[1:], BF16)
    scores(0, 0)

    def pair(jj, carry, last):
        m, l, alpha = carry
        a = 2 * jj
        values(jnp.maximum(a - 1, 0), 1, alpha)
        scores(a + 1, 1)
        m, l, alpha = softmax(0, m, l)
        values(a, 0, alpha)
        if not last:
            scores(a + 2, 0)
        return softmax(1, m, l)

    carry = (m, l, jnp.ones_like(m))
    carry = lax.fori_loop(0, nk // 2 - 1, functools.partial(pair, last=False), carry)
    m, l, alpha = pair(nk // 2 - 1, carry, last=True)
    values(nk - 1, 1, alpha)
    o_ref[0] = (acc_ref[...] / l).T.astype(BF16)


def _attention(qT, k, vT, k_meta, vT_meta, tq):
    B, H, _, S = qT.shape
    nk, tk = vT.shape[2], vT.shape[4]
    assert nk % 2 == 0, "the key loop is unrolled in pairs"
    return pl.pallas_call(
        functools.partial(_attn_body, nk=nk, tk=tk),
        grid=(B, H, S // tq),
        in_specs=[
            pl.BlockSpec((1, 1, QK_DIM, tq), lambda b, h, i: (b, h, 0, i)),
            pl.BlockSpec((1, 1, S, QK_DIM), lambda b, h, i: (b, h, 0, 0)),
            pl.BlockSpec((1, 1, nk, V_HEAD_DIM, tk), lambda b, h, i: (b, h, 0, 0, 0)),
            pl.BlockSpec((1, META_PAD, QK_DIM), lambda b, h, i: (h, 0, 0)),
            pl.BlockSpec((1, V_HEAD_DIM, META_PAD), lambda b, h, i: (h, 0, 0)),
        ],
        out_specs=pl.BlockSpec((1, tq, V_HEAD_DIM), lambda b, h, i: (b, i, h)),
        out_shape=jax.ShapeDtypeStruct((B, S, H * V_HEAD_DIM), BF16),
        scratch_shapes=[
            pltpu.VMEM((2, tk, tq), F32), pltpu.VMEM((2, tk, tq), BF16), pltpu.VMEM((V_HEAD_DIM, tq), F32),
        ],
        compiler_params=_params(("parallel", "parallel", "parallel")), name="attention",
    )(qT, k, vT, k_meta, vT_meta)


def _fnet_body(a_ref, b_ref, er_ref, ei_ref, pr_ref, pi_ref, wm_ref, abm_ref, o_ref, acc_ref, *, reps):
    ci = pl.program_id(2)

    @pl.when(ci == 0)
    def _():
        mt = _dot(wm_ref[...], abm_ref[...])
        acc_ref[...] = jnp.concatenate([mt] * reps, axis=1)

    lane = lax.broadcasted_iota(jnp.int32, pr_ref.shape, 1)
    sel = lane == ci
    pr = jnp.sum(jnp.where(sel, pr_ref[...], 0.0), axis=1, keepdims=True)
    pi = jnp.sum(jnp.where(sel, pi_ref[...], 0.0), axis=1, keepdims=True)
    er = er_ref[...]
    ei = ei_ref[...]
    wr = (pr * er - pi * ei).astype(BF16)
    wi = (pr * ei + pi * er).astype(BF16)
    acc_ref[...] += _dot(wr, a_ref[...]) + _dot(wi, b_ref[...])

    @pl.when(ci == pl.num_programs(2) - 1)
    def _():
        o_ref[...] = acc_ref[...].astype(BF16)


def _fnet(a, b, tabs, abm, tm, tkf, tn):
    S, NB = a.shape
    nkt = S // tkf
    return pl.pallas_call(
        functools.partial(_fnet_body, reps=tn // FNET_DIM),
        grid=(NB // tn, S // tm, nkt),
        in_specs=[
            pl.BlockSpec((tkf, tn), lambda n, r, c: (c, n)),
            pl.BlockSpec((tkf, tn), lambda n, r, c: (c, n)),
            pl.BlockSpec((tm, tkf), lambda n, r, c: (r, 0)),
            pl.BlockSpec((tm, tkf), lambda n, r, c: (r, 0)),
            pl.BlockSpec((tm, nkt), lambda n, r, c: (r, 0)),
            pl.BlockSpec((tm, nkt), lambda n, r, c: (r, 0)),
            pl.BlockSpec((tm, 2 * N_META), lambda n, r, c: (r, 0)),
            pl.BlockSpec((2 * N_META, FNET_DIM), lambda n, r, c: (0, 0)),
        ],
        out_specs=pl.BlockSpec((tm, tn), lambda n, r, c: (r, n)),
        out_shape=jax.ShapeDtypeStruct((S, NB), BF16),
        scratch_shapes=[pltpu.VMEM((tm, tn), F32)],
        compiler_params=_params(("parallel", "parallel", "arbitrary")), name="fnet",
    )(a, b, tabs["er"], tabs["ei"], tabs["pr"], tabs["pi"], tabs["wm"], abm)


def _post_mix_body(x_ref, att_ref, f_ref, gate_ref, wo_ref, wfo_ref, wout_ref, nm_ref,
                   rhi_ref, rlo_ref, rb_ref, h1_ref, route_ref):
    a = _dot(att_ref[0], wo_ref[...])
    bb = _dot(f_ref[...], wfo_ref[...])
    g = gate_ref[0].astype(F32)
    merged = g[:, :D_MODEL] * a + g[:, D_MODEL:] * bb
    h1 = x_ref[0] + _dot(merged.astype(BF16), wout_ref[...])
    h1_ref[0] = h1
    xn = _rms(h1, nm_ref[...])
    xh = xn.astype(BF16)

    xl = (xn - xh.astype(F32)).astype(BF16)
    logits = _dot(xh, rhi_ref[...]) + (_dot(xh, rlo_ref[...]) + _dot(xl, rhi_ref[...])) + rb_ref[...]
    lane = lax.broadcasted_iota(jnp.int32, logits.shape, 1)
    lanef = lane.astype(F32)
    far = float(2 * LANES)

    gl = jnp.where((lane >= N_EXPERTS) & (lane < N_EXPERTS + N_EXPERT_GROUPS), logits, NEG_BIG)
    gmax = jnp.max(gl, axis=1, keepdims=True)
    g_w = 1.0 / jnp.sum(jnp.exp(gl - gmax), axis=1, keepdims=True)
    gidx = jnp.min(jnp.where(gl == gmax, lanef, far), axis=1, keepdims=True) - float(N_EXPERTS)

    grp = (lane >> 3).astype(F32)
    el = jnp.where((lane < N_EXPERTS) & (grp == gidx), logits, NEG_BIG)
    m1 = jnp.max(el, axis=1, keepdims=True)
    i1 = jnp.min(jnp.where(el == m1, lanef, far), axis=1, keepdims=True)
    el2 = jnp.where(lanef == i1, NEG_BIG, el)
    m2 = jnp.max(el2, axis=1, keepdims=True)
    i2 = jnp.min(jnp.where(el2 == m2, lanef, far), axis=1, keepdims=True)
    r = jnp.exp(m2 - m1)
    w1 = g_w / (1.0 + r)
    w2 = g_w * r / (1.0 + r)

    first_is_lo = i1 < i2
    a_loc = jnp.minimum(i1, i2) - gidx * EXPERTS_PER_GROUP
    b_loc = jnp.maximum(i1, i2) - gidx * EXPERTS_PER_GROUP
    pair = a_loc * (2 * EXPERTS_PER_GROUP - 1 - a_loc) * 0.5 + (b_loc - a_loc - 1.0)
    bucket = gidx * PAIRS_PER_GROUP + pair
    w_lo = jnp.where(first_is_lo, w1, w2)
    w_hi = jnp.where(first_is_lo, w2, w1)
    route_ref[0] = jnp.where(lane == 0, bucket, jnp.where(lane == 1, w_lo, jnp.where(lane == 2, w_hi, 0.0)))


def _post_mix(x, att, f, gates, w, ts):
    B, S, _ = x.shape
    const = lambda shape: pl.BlockSpec(shape, lambda b, t: (0,) * len(shape))
    tok = lambda width: pl.BlockSpec((1, ts, width), lambda b, t: (b, t, 0))
    return pl.pallas_call(
        _post_mix_body, grid=(B, S // ts),
        in_specs=[
            tok(D_MODEL), tok(D_MODEL),
            pl.BlockSpec((ts, FNET_DIM), lambda b, t: (t, b)),
            tok(2 * D_MODEL),
            const((D_MODEL, D_MODEL)), const((FNET_DIM, D_MODEL)), const((D_MODEL, D_MODEL)),
            const((1, D_MODEL)), const((D_MODEL, LANES)), const((D_MODEL, LANES)), const((1, LANES)),
        ],
        out_specs=(tok(D_MODEL), tok(LANES)),
        out_shape=(
            jax.ShapeDtypeStruct((B, S, D_MODEL), F32),
            jax.ShapeDtypeStruct((B, S, LANES), F32),
        ),
        compiler_params=_params(("parallel", "parallel")), name="post_mix",
    )(x, att, f, gates, w["w_o"], w["w_fo"], w["w_out"], w["norm_moe"], w["r_hi"], w["r_lo"], w["r_b"])


def _moe_body(lo_ref, hi_ref, nv_ref, src_ref, nxt_ref, rt_ref, h1_hbm,
              wgl_ref, wul_ref, wdl_ref, wgh_ref, wuh_ref, wdh_ref, nm_ref, nf_ref,
              out_hbm, xbuf, ybuf, gsem, ssem, *, tb):
    i = pl.program_id(0)
    n = pl.num_programs(0)
    slot = lax.rem(i, 2)
    other = 1 - slot
    cnt = nv_ref[i]

    def gather(idx_ref, s, r):
        return pltpu.make_async_copy(h1_hbm.at[idx_ref[0, 0, r]], xbuf.at[s, r], gsem.at[s])

    def scatter_wait(s, rows):
        full = pl.multiple_of(lax.shift_left(lax.shift_right_logical(rows, 3), 3), 8)

        @pl.when(full > 0)
        def _():
            pltpu.make_async_copy(ybuf.at[s, pl.ds(0, full)], out_hbm.at[pl.ds(0, full)], ssem.at[s]).wait()

        def one(r, c):
            pltpu.make_async_copy(ybuf.at[s, r], out_hbm.at[0], ssem.at[s]).wait()
            return c

        lax.fori_loop(full, rows, one, 0)

    @pl.when(i == 0)
    def _():
        for r in range(tb):
            gather(src_ref, 0, r).start()

    @pl.when((i + 1 < n) & (nv_ref[jnp.minimum(i + 1, n - 1)] > 0))
    def _():
        for r in range(tb):
            gather(nxt_ref, other, r).start()

    @pl.when(cnt > 0)
    def _():
        for r in range(tb):
            gather(src_ref, slot, r).wait()
        x = xbuf[slot]
        xb = _rms(x, nm_ref[...]).astype(BF16)
        rt = rt_ref[0]

        def expert(wg_ref, wu_ref, wd_ref, wcol):
            gt = _dot(xb, wg_ref[0])
            up = _dot(xb, wu_ref[0])
            act = (gt / (1.0 + jnp.exp(-gt))) * up * wcol
            return _dot(act.astype(BF16), wd_ref[0])

        y = expert(wgl_ref, wul_ref, wdl_ref, rt[:, 1:2]) + expert(wgh_ref, wuh_ref, wdh_ref, rt[:, 2:3])
        ybuf[slot] = _rms(x + y, nf_ref[...])

    prev = nv_ref[jnp.maximum(i - 1, 0)]

    @pl.when((i > 0) & (prev > 0))
    def _():
        scatter_wait(other, prev)

    @pl.when(cnt > 0)
    def _():
        def issue(r, c):
            pltpu.make_async_copy(ybuf.at[slot, r], out_hbm.at[src_ref[0, 0, r]], ssem.at[slot]).start()
            return c

        lax.fori_loop(0, cnt, issue, 0)

    @pl.when((i == n - 1) & (cnt > 0))
    def _():
        scatter_wait(slot, cnt)


def _moe(h1, route, w, tb):
    N = h1.shape[0]
    t = _route_tables(route, tb)
    nblk = t["src"].shape[0]
    idx_spec = lambda f: pl.BlockSpec((1, 1, tb), f, memory_space=pltpu.SMEM)
    wspec = lambda shape, which: pl.BlockSpec(
        (1,) + shape, (lambda i, lo, hi, nv: (lo[i], 0, 0)) if which == 0 else (lambda i, lo, hi, nv: (hi[i], 0, 0)))
    up_shape, down_shape = (D_MODEL, D_FF_EXPERT), (D_FF_EXPERT, D_MODEL)
    grid_spec = pltpu.PrefetchScalarGridSpec(
        num_scalar_prefetch=3, grid=(nblk,),
        in_specs=[
            idx_spec(lambda i, lo, hi, nv: (i, 0, 0)),
            idx_spec(lambda i, lo, hi, nv: (jnp.minimum(i + 1, nblk - 1), 0, 0)),
            pl.BlockSpec((1, tb, LANES), lambda i, lo, hi, nv: (i, 0, 0)),
            pl.BlockSpec(memory_space=pl.ANY),
            wspec(up_shape, 0), wspec(up_shape, 0), wspec(down_shape, 0),
            wspec(up_shape, 1), wspec(up_shape, 1), wspec(down_shape, 1),
            pl.BlockSpec((1, D_MODEL), lambda i, lo, hi, nv: (0, 0)),
            pl.BlockSpec((1, D_MODEL), lambda i, lo, hi, nv: (0, 0)),
        ],
        out_specs=pl.BlockSpec(memory_space=pl.ANY),
        scratch_shapes=[
            pltpu.VMEM((2, tb, D_MODEL), F32), pltpu.VMEM((2, tb, D_MODEL), F32),
            pltpu.SemaphoreType.DMA((2,)), pltpu.SemaphoreType.DMA((2,)),
        ],
    )
    return pl.pallas_call(
        functools.partial(_moe_body, tb=tb), grid_spec=grid_spec,
        out_shape=jax.ShapeDtypeStruct((N, D_MODEL), F32),
        compiler_params=_params(("arbitrary",)), name="moe",
    )(t["lo"], t["hi"], t["nvalid"], t["src"], t["src"], t["rt"], h1,
      w["w_eg"], w["w_eu"], w["w_ed"], w["w_eg"], w["w_eu"], w["w_ed"], w["norm_moe"], w["norm_final"])


def _route_tables(route, tb):
    N = route.shape[0]
    bucket = route[:, 0].astype(jnp.int32)
    order = jnp.argsort(bucket).astype(jnp.int32)
    sorted_b = bucket[order]
    ids = jnp.arange(N_BUCKETS, dtype=jnp.int32)
    starts = jnp.searchsorted(sorted_b, ids, side="left").astype(jnp.int32)
    counts = jnp.searchsorted(sorted_b, ids, side="right").astype(jnp.int32) - starts
    padded = ((counts + tb - 1) // tb) * tb
    pend = jnp.cumsum(padded)
    poff = pend - padded
    nblk = N // tb + N_BUCKETS
    blk_start = jnp.arange(nblk, dtype=jnp.int32) * tb
    active = blk_start < pend[-1]
    bb = jnp.minimum(jnp.searchsorted(pend, blk_start, side="right"), N_BUCKETS - 1).astype(jnp.int32)
    r0 = blk_start - poff[bb]
    nvalid = jnp.where(active, jnp.clip(counts[bb] - r0, 0, tb), 0).astype(jnp.int32)
    bb_w = jnp.where(active, bb, bb[jnp.maximum(jnp.sum(active.astype(jnp.int32)) - 1, 0)])
    rows = starts[bb][:, None] + r0[:, None] + jnp.arange(tb, dtype=jnp.int32)[None, :]
    src = order[jnp.clip(rows, 0, N - 1)]
    return {
        "lo": jnp.asarray(_BUCKET_LO)[bb_w], "hi": jnp.asarray(_BUCKET_HI)[bb_w], "nvalid": nvalid,
        "src": src[:, None, :], "rt": route[src],
    }


def _rope_tables(pos):
    inv_freq = 1.0 / (ROPE_THETA ** (jnp.arange(ROPE_HALF, dtype=F32) / ROPE_HALF))
    ang = pos.astype(F32)[:, None] * inv_freq[None, :]
    c, s = jnp.cos(ang), jnp.sin(ang)
    z = jnp.zeros((pos.shape[0], LANES - QK_ROPE_DIM), F32)
    return {
        "cos_nat": jnp.concatenate([c, c, z], axis=1), "sin_nat": jnp.concatenate([s, s, z], axis=1),
        "cosT": c.T, "sinT": s.T,
    }


def _unit(k, n):
    ang = k.astype(F32) * (-2.0 * math.pi / n)
    return jnp.cos(ang), jnp.sin(ang)


def _fnet_tables(S, tkf):
    L = S + N_META
    j = jnp.arange(S, dtype=jnp.int32)[:, None] + N_META
    er, ei = _unit((j * jnp.arange(tkf, dtype=jnp.int32)[None, :]) % L, L)
    col0 = jnp.arange(S // tkf, dtype=jnp.int32)[None, :] * tkf + N_META
    pr, pi = _unit((j * col0) % L, L)
    mr, mi = _unit((j * jnp.arange(N_META, dtype=jnp.int32)[None, :]) % L, L)
    return {"er": er, "ei": ei, "pr": pr, "pi": pi, "wm": jnp.concatenate([mr, mi], axis=1).astype(BF16)}


def _channel_dft(L):
    c = jnp.arange(FNET_GROUP_DIM, dtype=jnp.int32)
    cr, ci = _unit((c[:, None] * c[None, :]) % FNET_GROUP_DIM, FNET_GROUP_DIM)
    norm = 1.0 / math.sqrt(L * FNET_GROUP_DIM)
    eye = jnp.eye(N_FNET_GROUPS, dtype=F32)
    return (jnp.kron(eye, cr * norm).astype(BF16), jnp.kron(eye, -ci * norm).astype(BF16))


def _prep_weights(norm_attn, w_in, g_q, w_uq, g_kv, w_uk, w_uv, w_o_mla, w_fo, w_out, norm_moe,
                  w_rg, b_rg, w_re, b_re, w_eg, w_eu, w_ed, norm_final):
    s1 = Q_LORA_RANK
    s2 = s1 + KV_LORA_RANK
    s3 = s2 + QK_ROPE_DIM
    s4 = s3 + FNET_DIM
    w_in_p = jnp.concatenate(
        [w_in[:, :s1], w_in[:, s1:s2], w_in[:, s3:s4], w_in[:, s4:], w_in[:, s2:s3],
         jnp.zeros((D_MODEL, LANES - QK_ROPE_DIM), w_in.dtype)], axis=1)
    r = jnp.concatenate([w_re, w_rg, jnp.zeros((D_MODEL, LANES - N_EXPERTS - N_EXPERT_GROUPS), F32)], axis=1)
    r_hi = r.astype(BF16)
    r_b = jnp.concatenate([b_re, b_rg, jnp.zeros((LANES - N_EXPERTS - N_EXPERT_GROUPS,), F32)])[None, :]
    return {
        "norm_attn": norm_attn[None, :], "w_in": w_in_p.astype(BF16),
        "g_q": g_q[None, :], "w_uqT": w_uq.reshape(Q_LORA_RANK, N_HEADS * QK_DIM).T.astype(BF16),
        "g_kv": g_kv[None, :], "w_uk": w_uk.reshape(KV_LORA_RANK, N_HEADS * QK_NOPE_DIM).astype(BF16),
        "w_uvT": w_uv.reshape(KV_LORA_RANK, N_HEADS * V_HEAD_DIM).T.astype(BF16),
        "w_o": w_o_mla.astype(BF16), "w_fo": w_fo.astype(BF16), "w_out": w_out.astype(BF16),
        "norm_moe": norm_moe[None, :], "r_hi": r_hi, "r_lo": (r - r_hi.astype(F32)).astype(BF16), "r_b": r_b,
        "w_eg": w_eg.reshape(N_EXPERTS, D_MODEL, D_FF_EXPERT).astype(BF16),
        "w_eu": w_eu.reshape(N_EXPERTS, D_MODEL, D_FF_EXPERT).astype(BF16),
        "w_ed": w_ed.reshape(N_EXPERTS, D_FF_EXPERT, D_MODEL).astype(BF16),
        "norm_final": norm_final[None, :],
    }


def _tiles(B, S):
    ts = min(512, S // 2)
    return {
        "ts": ts, "tq": min(512, S), "tm": min(1024, S), "tkf": min(512, S),
        "tn": min(2048, B * FNET_DIM), "tb": 128,
    }


def _trunk(x, meta, w):
    B, S, _ = x.shape
    L = S + N_META
    t = _tiles(B, S)
    wl = dict(w)
    wl["cc"], wl["sc"] = _channel_dft(L)

    meta_pad = jnp.zeros((1, META_PAD, D_MODEL), F32).at[0, :N_META].set(meta)
    _, k_m, vT_m, a_m, b_m, _ = _in_proj(meta_pad, wl, _rope_tables(jnp.arange(META_PAD)), META_PAD)
    qT, k, vT, a, b, gates = _in_proj(x, wl, _rope_tables(jnp.arange(S) + N_META), t["ts"])

    att = _attention(qT, k, vT, k_m[0], vT_m[0, :, 0], t["tq"])
    abm = jnp.concatenate([a_m[:N_META], b_m[:N_META]], axis=0)
    f = _fnet(a, b, _fnet_tables(S, t["tkf"]), abm, t["tm"], t["tkf"], t["tn"])
    h1, route = _post_mix(x, att, f, gates, wl, t["ts"])
    y = _moe(h1.reshape(B * S, D_MODEL), route.reshape(B * S, LANES), wl, t["tb"])
    return y.reshape(B, S, D_MODEL)


def kernel(x_prompt, x_sample, meta_tokens, norm_attn, w_in, g_q, w_uq, g_kv, w_uk, w_uv, w_o_mla, w_fo, w_out,
           norm_moe, w_router_group, b_router_group, w_router_expert, b_router_expert, w_exp_gate, w_exp_up,
           w_exp_down, norm_final):
    w = _prep_weights(norm_attn[0], w_in[0], g_q[0], w_uq[0], g_kv[0], w_uk[0], w_uv[0], w_o_mla[0], w_fo[0],
                      w_out[0], norm_moe[0], w_router_group[0], b_router_group[0], w_router_expert[0],
                      b_router_expert[0], w_exp_gate[0], w_exp_up[0], w_exp_down[0], norm_final)
    return _trunk(x_prompt, meta_tokens, w), _trunk(x_sample, meta_tokens, w)
```

```python
import functools
import math

import jax
import jax.numpy as jnp
import numpy as np
from jax import lax
from jax.experimental import pallas as pl
from jax.experimental.pallas import tpu as pltpu

D_MODEL = 1024
N_META = 16
N_HEADS = 8
QK_NOPE_DIM = 128
QK_ROPE_DIM = 64
ROPE_HALF = QK_ROPE_DIM // 2
QK_DIM = QK_NOPE_DIM + QK_ROPE_DIM
V_HEAD_DIM = 128
Q_LORA_RANK = 384
KV_LORA_RANK = 256
ROPE_THETA = 10000.0
N_FNET_GROUPS = 4
FNET_GROUP_DIM = 128
FNET_DIM = N_FNET_GROUPS * FNET_GROUP_DIM
N_EXPERT_GROUPS = 4
EXPERTS_PER_GROUP = 8
N_EXPERTS = N_EXPERT_GROUPS * EXPERTS_PER_GROUP
D_FF_EXPERT = 256
EPS = 1e-6
PAIRS_PER_GROUP = EXPERTS_PER_GROUP * (EXPERTS_PER_GROUP - 1) // 2
N_BUCKETS = N_EXPERT_GROUPS * PAIRS_PER_GROUP
_PAIRS = [(a, b) for a in range(EXPERTS_PER_GROUP) for b in range(a + 1, EXPERTS_PER_GROUP)]
_BUCKET_LO = np.array([g * EXPERTS_PER_GROUP + a for g in range(N_EXPERT_GROUPS) for a, _ in _PAIRS], np.int32)
_BUCKET_HI = np.array([g * EXPERTS_PER_GROUP + b for g in range(N_EXPERT_GROUPS) for _, b in _PAIRS], np.int32)

LANES = 128
META_PAD = 128
VMEM_LIMIT_BYTES = 56 * 1024 * 1024
NEG_BIG = -1e30

_CQ0, _CQ1 = 0, Q_LORA_RANK
_CKV0, _CKV1 = _CQ1, _CQ1 + KV_LORA_RANK
_UF0, _UF1 = _CKV1, _CKV1 + FNET_DIM
_G0, _G1 = _UF1, _UF1 + 2 * D_MODEL
_KR0, _KR1 = _G1, _G1 + LANES
IN_DIM_PADDED = _KR1

F32 = jnp.float32
BF16 = jnp.bfloat16


def _dot(a, b):
    return jnp.dot(a, b, preferred_element_type=F32)


def _rms(x, g):
    ms = jnp.mean(x * x, axis=-1, keepdims=True)
    return x * lax.rsqrt(ms + EPS) * g


def _params(semantics):
    return pltpu.CompilerParams(dimension_semantics=semantics, vmem_limit_bytes=VMEM_LIMIT_BYTES)


def _in_proj_body(x_ref, na_ref, win_ref, gq_ref, wuq_ref, gkv_ref, wukT_ref, wuv_ref,
                  cosn_ref, sinn_ref, cosT_ref, sinT_ref, cc_ref, sc_ref,
                  q_ref, kT_ref, v_ref, a_ref, b_ref, gate_ref):
    x = x_ref[0]
    hb = _rms(x, na_ref[...]).astype(BF16)

    cqn = _rms(_dot(hb, win_ref[:, _CQ0:_CQ1]), gq_ref[...])
    qa = _dot(cqn.astype(BF16), wuq_ref[...])
    cn = cosn_ref[...]
    sn = sinn_ref[...]
    first_half = lax.broadcasted_iota(jnp.int32, cn.shape, 1) < ROPE_HALF
    scale = math.log2(math.e) / math.sqrt(QK_DIM)
    for h in range(N_HEADS):
        c0 = h * 2 * LANES
        qr = qa[:, c0 + LANES:c0 + 2 * LANES]
        rot = jnp.where(first_half, -pltpu.roll(qr, LANES - ROPE_HALF, 1), pltpu.roll(qr, ROPE_HALF, 1))
        q_ref[0, h, :, 0:QK_NOPE_DIM] = (qa[:, c0:c0 + LANES] * scale).astype(BF16)
        q_ref[0, h, :, QK_NOPE_DIM:QK_DIM] = ((qr * cn + rot * sn) * scale)[:, 0:QK_ROPE_DIM].astype(BF16)

    ckvn = _rms(_dot(hb, win_ref[:, _CKV0:_CKV1]), gkv_ref[...])
    knT = _dot(wukT_ref[...], ckvn.T.astype(BF16))
    krT = _dot(hb, win_ref[:, _KR0:_KR1]).T
    x1 = krT[0:ROPE_HALF]
    x2 = krT[ROPE_HALF:QK_ROPE_DIM]
    cT = cosT_ref[...]
    sT = sinT_ref[...]
    r1 = (x1 * cT - x2 * sT).astype(BF16)
    r2 = (x2 * cT + x1 * sT).astype(BF16)
    for h in range(N_HEADS):
        kT_ref[0, h, 0, 0:QK_NOPE_DIM, :] = knT[h * QK_NOPE_DIM:(h + 1) * QK_NOPE_DIM].astype(BF16)
        kT_ref[0, h, 0, QK_NOPE_DIM:QK_NOPE_DIM + ROPE_HALF, :] = r1
        kT_ref[0, h, 0, QK_NOPE_DIM + ROPE_HALF:QK_DIM, :] = r2
    v_ref[0] = _dot(ckvn.astype(BF16), wuv_ref[...]).astype(BF16)

    ub = _dot(hb, win_ref[:, _UF0:_UF1]).astype(BF16)
    a_ref[...] = _dot(ub, cc_ref[...]).astype(BF16)
    b_ref[...] = _dot(ub, sc_ref[...]).astype(BF16)

    gl = _dot(hb, win_ref[:, _G0:_G1])
    gate_ref[0] = (1.0 / (1.0 + jnp.exp(-gl))).astype(BF16)


def _in_proj(x, w, rope, ts):
    B, S, _ = x.shape
    nt = S // ts
    const = lambda shape: pl.BlockSpec(shape, lambda b, t: (0,) * len(shape))
    out_shape = (
        jax.ShapeDtypeStruct((B, N_HEADS, S, QK_DIM), BF16),
        jax.ShapeDtypeStruct((B, N_HEADS, nt, QK_DIM, ts), BF16),
        jax.ShapeDtypeStruct((B, S, N_HEADS * V_HEAD_DIM), BF16),
        jax.ShapeDtypeStruct((S, B * FNET_DIM), BF16),
        jax.ShapeDtypeStruct((S, B * FNET_DIM), BF16),
        jax.ShapeDtypeStruct((B, S, 2 * D_MODEL), BF16),
    )
    in_specs = [
        pl.BlockSpec((1, ts, D_MODEL), lambda b, t: (b, t, 0)),
        const((1, D_MODEL)),
        const((D_MODEL, IN_DIM_PADDED)),
        const((1, Q_LORA_RANK)),
        const((Q_LORA_RANK, N_HEADS * 2 * LANES)),
        const((1, KV_LORA_RANK)),
        const((N_HEADS * QK_NOPE_DIM, KV_LORA_RANK)),
        const((KV_LORA_RANK, N_HEADS * V_HEAD_DIM)),
        pl.BlockSpec((ts, LANES), lambda b, t: (t, 0)),
        pl.BlockSpec((ts, LANES), lambda b, t: (t, 0)),
        pl.BlockSpec((ROPE_HALF, ts), lambda b, t: (0, t)),
        pl.BlockSpec((ROPE_HALF, ts), lambda b, t: (0, t)),
        const((FNET_DIM, FNET_DIM)),
        const((FNET_DIM, FNET_DIM)),
    ]
    out_specs = (
        pl.BlockSpec((1, N_HEADS, ts, QK_DIM), lambda b, t: (b, 0, t, 0)),
        pl.BlockSpec((1, N_HEADS, 1, QK_DIM, ts), lambda b, t: (b, 0, t, 0, 0)),
        pl.BlockSpec((1, ts, N_HEADS * V_HEAD_DIM), lambda b, t: (b, t, 0)),
        pl.BlockSpec((ts, FNET_DIM), lambda b, t: (t, b)),
        pl.BlockSpec((ts, FNET_DIM), lambda b, t: (t, b)),
        pl.BlockSpec((1, ts, 2 * D_MODEL), lambda b, t: (b, t, 0)),
    )
    return pl.pallas_call(
        _in_proj_body, grid=(B, nt), in_specs=in_specs, out_specs=out_specs, out_shape=out_shape,
        compiler_params=_params(("parallel", "parallel")), name="in_proj",
    )(x, w["norm_attn"], w["w_in"], w["g_q"], w["w_uq"], w["g_kv"], w["w_ukT"], w["w_uv"],
      rope["cos_nat"], rope["sin_nat"], rope["cosT"], rope["sinT"], w["cc"], w["sc"])


def _attn_body(q_ref, kT_ref, v_ref, kmT_ref, vm_ref, o_ref, s_ref, p_ref, acc_ref, *, nk, tk):
    q = q_ref[0, 0]

    def scores(j, slot):
        s_ref[slot] = _dot(q, kT_ref[0, 0, j])

    def values(j, slot, alpha):
        v = v_ref[0, pl.ds(pl.multiple_of(j * tk, tk), tk), :]
        acc_ref[...] = alpha * acc_ref[...] + _dot(p_ref[slot], v)

    def softmax(slot, m, l):
        s = s_ref[slot]
        m_new = jnp.maximum(m, jnp.max(s, axis=1, keepdims=True))
        alpha = jnp.exp2(m - m_new)
        p = jnp.exp2(s - m_new)
        p_ref[slot] = p.astype(BF16)
        return m_new, alpha * l + jnp.sum(p, axis=1, keepdims=True), alpha

    s = _dot(q, kmT_ref[0])
    col = lax.broadcasted_iota(jnp.int32, s.shape, 1)
    s = jnp.where(col < N_META, s, NEG_BIG)
    m = jnp.max(s, axis=1, keepdims=True)
    p = jnp.exp2(s - m)
    l = jnp.sum(p, axis=1, keepdims=True)
    acc_ref[...] = _dot(p.astype(BF16), vm_ref[0])

    p_ref[1] = jnp.zeros(p_ref.shape[1:], BF16)
    scores(0, 0)

    def pair(jj, carry, last):
        m, l, alpha = carry
        a = 2 * jj
        values(jnp.maximum(a - 1, 0), 1, alpha)
        scores(a + 1, 1)
        m, l, alpha = softmax(0, m, l)
        values(a, 0, alpha)
        if not last:
            scores(a + 2, 0)
        return softmax(1, m, l)

    carry = (m, l, jnp.ones_like(m))
    carry = lax.fori_loop(0, nk // 2 - 1, functools.partial(pair, last=False), carry)
    m, l, alpha = pair(nk // 2 - 1, carry, last=True)
    values(nk - 1, 1, alpha)
    o_ref[0] = (acc_ref[...] / l).astype(BF16)


def _attention(q, kT, v, kT_meta, v_meta, tq):
    B, H, S, _ = q.shape
    nk, tk = kT.shape[2], kT.shape[4]
    assert nk % 2 == 0, "the key loop is unrolled in pairs"
    return pl.pallas_call(
        functools.partial(_attn_body, nk=nk, tk=tk),
        grid=(B, H, S // tq),
        in_specs=[
            pl.BlockSpec((1, 1, tq, QK_DIM), lambda b, h, i: (b, h, i, 0)),
            pl.BlockSpec((1, 1, nk, QK_DIM, tk), lambda b, h, i: (b, h, 0, 0, 0)),
            pl.BlockSpec((1, S, V_HEAD_DIM), lambda b, h, i: (b, 0, h)),
            pl.BlockSpec((1, QK_DIM, META_PAD), lambda b, h, i: (h, 0, 0)),
            pl.BlockSpec((1, META_PAD, V_HEAD_DIM), lambda b, h, i: (h, 0, 0)),
        ],
        out_specs=pl.BlockSpec((1, tq, V_HEAD_DIM), lambda b, h, i: (b, i, h)),
        out_shape=jax.ShapeDtypeStruct((B, S, H * V_HEAD_DIM), BF16),
        scratch_shapes=[
            pltpu.VMEM((2, tq, tk), F32), pltpu.VMEM((2, tq, tk), BF16), pltpu.VMEM((tq, V_HEAD_DIM), F32),
        ],
        compiler_params=_params(("parallel", "parallel", "parallel")), name="attention",
    )(q, kT, v, kT_meta, v_meta)


def _fnet_body(a_ref, b_ref, er_ref, ei_ref, pr_ref, pi_ref, wm_ref, abm_ref, o_ref, acc_ref, *, reps):
    ci = pl.program_id(2)

    @pl.when(ci == 0)
    def _():
        mt = _dot(wm_ref[...], abm_ref[...])
        acc_ref[...] = jnp.concatenate([mt] * reps, axis=1)

    lane = lax.broadcasted_iota(jnp.int32, pr_ref.shape, 1)
    sel = lane == ci
    pr = jnp.sum(jnp.where(sel, pr_ref[...], 0.0), axis=1, keepdims=True)
    pi = jnp.sum(jnp.where(sel, pi_ref[...], 0.0), axis=1, keepdims=True)
    er = er_ref[...]
    ei = ei_ref[...]
    wr = (pr * er - pi * ei).astype(BF16)
    wi = (pr * ei + pi * er).astype(BF16)
    acc_ref[...] += _dot(wr, a_ref[...]) + _dot(wi, b_ref[...])

    @pl.when(ci == pl.num_programs(2) - 1)
    def _():
        o_ref[...] = acc_ref[...].astype(BF16)


def _fnet(a, b, tabs, abm, tm, tkf, tn):
    S, NB = a.shape
    nkt = S // tkf
    return pl.pallas_call(
        functools.partial(_fnet_body, reps=tn // FNET_DIM),
        grid=(NB // tn, S // tm, nkt),
        in_specs=[
            pl.BlockSpec((tkf, tn), lambda n, r, c: (c, n)),
            pl.BlockSpec((tkf, tn), lambda n, r, c: (c, n)),
            pl.BlockSpec((tm, tkf), lambda n, r, c: (r, 0)),
            pl.BlockSpec((tm, tkf), lambda n, r, c: (r, 0)),
            pl.BlockSpec((tm, nkt), lambda n, r, c: (r, 0)),
            pl.BlockSpec((tm, nkt), lambda n, r, c: (r, 0)),
            pl.BlockSpec((tm, 2 * N_META), lambda n, r, c: (r, 0)),
            pl.BlockSpec((2 * N_META, FNET_DIM), lambda n, r, c: (0, 0)),
        ],
        out_specs=pl.BlockSpec((tm, tn), lambda n, r, c: (r, n)),
        out_shape=jax.ShapeDtypeStruct((S, NB), BF16),
        scratch_shapes=[pltpu.VMEM((tm, tn), F32)],
        compiler_params=_params(("parallel", "parallel", "arbitrary")), name="fnet",
    )(a, b, tabs["er"], tabs["ei"], tabs["pr"], tabs["pi"], tabs["wm"], abm)


def _post_mix_body(x_ref, att_ref, f_ref, gate_ref, wo_ref, wfo_ref, wout_ref, nm_ref,
                   rhi_ref, rlo_ref, rb_ref, h1_ref, route_ref):
    a = _dot(att_ref[0], wo_ref[...])
    bb = _dot(f_ref[...], wfo_ref[...])
    g = gate_ref[0].astype(F32)
    merged = g[:, :D_MODEL] * a + g[:, D_MODEL:] * bb
    h1 = x_ref[0] + _dot(merged.astype(BF16), wout_ref[...])
    h1_ref[0] = h1
    xn = _rms(h1, nm_ref[...])
    xh = xn.astype(BF16)

    xl = (xn - xh.astype(F32)).astype(BF16)
    logits = _dot(xh, rhi_ref[...]) + (_dot(xh, rlo_ref[...]) + _dot(xl, rhi_ref[...])) + rb_ref[...]
    lane = lax.broadcasted_iota(jnp.int32, logits.shape, 1)
    lanef = lane.astype(F32)
    far = float(2 * LANES)

    gl = jnp.where((lane >= N_EXPERTS) & (lane < N_EXPERTS + N_EXPERT_GROUPS), logits, NEG_BIG)
    gmax = jnp.max(gl, axis=1, keepdims=True)
    g_w = 1.0 / jnp.sum(jnp.exp(gl - gmax), axis=1, keepdims=True)
    gidx = jnp.min(jnp.where(gl == gmax, lanef, far), axis=1, keepdims=True) - float(N_EXPERTS)

    grp = (lane >> 3).astype(F32)
    el = jnp.where((lane < N_EXPERTS) & (grp == gidx), logits, NEG_BIG)
    m1 = jnp.max(el, axis=1, keepdims=True)
    i1 = jnp.min(jnp.where(el == m1, lanef, far), axis=1, keepdims=True)
    el2 = jnp.where(lanef == i1, NEG_BIG, el)
    m2 = jnp.max(el2, axis=1, keepdims=True)
    i2 = jnp.min(jnp.where(el2 == m2, lanef, far), axis=1, keepdims=True)
    r = jnp.exp(m2 - m1)
    w1 = g_w / (1.0 + r)
    w2 = g_w * r / (1.0 + r)

    first_is_lo = i1 < i2
    a_loc = jnp.minimum(i1, i2) - gidx * EXPERTS_PER_GROUP
    b_loc = jnp.maximum(i1, i2) - gidx * EXPERTS_PER_GROUP
    pair = a_loc * (2 * EXPERTS_PER_GROUP - 1 - a_loc) * 0.5 + (b_loc - a_loc - 1.0)
    bucket = gidx * PAIRS_PER_GROUP + pair
    w_lo = jnp.where(first_is_lo, w1, w2)
    w_hi = jnp.where(first_is_lo, w2, w1)
    route_ref[0] = jnp.where(lane == 0, bucket, jnp.where(lane == 1, w_lo, jnp.where(lane == 2, w_hi, 0.0)))


def _post_mix(x, att, f, gates, w, ts):
    B, S, _ = x.shape
    const = lambda shape: pl.BlockSpec(shape, lambda b, t: (0,) * len(shape))
    tok = lambda width: pl.BlockSpec((1, ts, width), lambda b, t: (b, t, 0))
    return pl.pallas_call(
        _post_mix_body, grid=(B, S // ts),
        in_specs=[
            tok(D_MODEL), tok(D_MODEL),
            pl.BlockSpec((ts, FNET_DIM), lambda b, t: (t, b)),
            tok(2 * D_MODEL),
            const((D_MODEL, D_MODEL)), const((FNET_DIM, D_MODEL)), const((D_MODEL, D_MODEL)),
            const((1, D_MODEL)), const((D_MODEL, LANES)), const((D_MODEL, LANES)), const((1, LANES)),
        ],
        out_specs=(tok(D_MODEL), tok(LANES)),
        out_shape=(
            jax.ShapeDtypeStruct((B, S, D_MODEL), F32),
            jax.ShapeDtypeStruct((B, S, LANES), F32),
        ),
        compiler_params=_params(("parallel", "parallel")), name="post_mix",
    )(x, att, f, gates, w["w_o"], w["w_fo"], w["w_out"], w["norm_moe"], w["r_hi"], w["r_lo"], w["r_b"])


def _moe_body(lo_ref, hi_ref, nv_ref, src_ref, nxt_ref, rt_ref, h1_hbm,
              wgl_ref, wul_ref, wdl_ref, wgh_ref, wuh_ref, wdh_ref, nm_ref, nf_ref,
              out_hbm, xbuf, ybuf, gsem, ssem, *, tb):
    i = pl.program_id(0)
    slot = lax.rem(i, 2)
    other = 1 - slot
    cnt = nv_ref[i]
    prev = jnp.where(i > 0, nv_ref[jnp.maximum(i - 1, 0)], 0)

    def gather(idx_ref, s, r):
        return pltpu.make_async_copy(h1_hbm.at[idx_ref[0, 0, r]], xbuf.at[s, r], gsem.at[s])

    def scatter_wait(s, rows):
        full = pl.multiple_of(lax.shift_left(lax.shift_right_logical(rows, 3), 3), 8)

        @pl.when(full > 0)
        def _():
            pltpu.make_async_copy(ybuf.at[s, pl.ds(0, full)], out_hbm.at[pl.ds(0, full)], ssem.at[s]).wait()

        def one(r, c):
            pltpu.make_async_copy(ybuf.at[s, r], out_hbm.at[0], ssem.at[s]).wait()
            return c

        lax.fori_loop(full, rows, one, 0)

    @pl.when(i == 0)
    def _():
        for r in range(tb):
            gather(src_ref, 0, r).start()

    @pl.when(cnt > 0)
    def _():
        for r in range(tb):
            gather(src_ref, slot, r).wait()
        for r in range(tb):
            gather(nxt_ref, other, r).start()
        x = xbuf[slot]
        xb = _rms(x, nm_ref[...]).astype(BF16)
        rt = rt_ref[0]

        def expert(wg_ref, wu_ref, wd_ref, wcol):
            gt = _dot(xb, wg_ref[0])
            up = _dot(xb, wu_ref[0])
            act = (gt / (1.0 + jnp.exp(-gt))) * up * wcol
            return _dot(act.astype(BF16), wd_ref[0])

        y = expert(wgl_ref, wul_ref, wdl_ref, rt[:, 0:1]) + expert(wgh_ref, wuh_ref, wdh_ref, rt[:, 1:2])
        ybuf[slot] = _rms(x + y, nf_ref[...])

        scatter_wait(other, prev)

        def issue(r):
            pltpu.make_async_copy(ybuf.at[slot, r], out_hbm.at[src_ref[0, 0, r]], ssem.at[slot]).start()

        def issue8(g, c):
            for u in range(8):
                issue(g * 8 + u)
            return c

        def issue1(r, c):
            issue(r)
            return c

        groups = lax.shift_right_logical(cnt, 3)
        lax.fori_loop(0, groups, issue8, 0)
        lax.fori_loop(groups * 8, cnt, issue1, 0)

    @pl.when((cnt == 0) & (prev > 0))
    def _():
        for r in range(tb):
            gather(src_ref, slot, r).wait()
        scatter_wait(other, prev)


def _moe(h1, route, w, tb):
    N = h1.shape[0]
    t = _route_tables(route, tb)
    nblk = t["src"].shape[0]
    idx_spec = lambda f: pl.BlockSpec((1, 1, tb), f, memory_space=pltpu.SMEM)
    wspec = lambda shape, which: pl.BlockSpec(
        (1,) + shape, (lambda i, lo, hi, nv: (lo[i], 0, 0)) if which == 0 else (lambda i, lo, hi, nv: (hi[i], 0, 0)))
    up_shape, down_shape = (D_MODEL, D_FF_EXPERT), (D_FF_EXPERT, D_MODEL)
    grid_spec = pltpu.PrefetchScalarGridSpec(
        num_scalar_prefetch=3, grid=(nblk,),
        in_specs=[
            idx_spec(lambda i, lo, hi, nv: (i, 0, 0)),
            idx_spec(lambda i, lo, hi, nv: (jnp.minimum(i + 1, nblk - 1), 0, 0)),
            pl.BlockSpec((1, tb, 2), lambda i, lo, hi, nv: (i, 0, 0)),
            pl.BlockSpec(memory_space=pl.ANY),
            wspec(up_shape, 0), wspec(up_shape, 0), wspec(down_shape, 0),
            wspec(up_shape, 1), wspec(up_shape, 1), wspec(down_shape, 1),
            pl.BlockSpec((1, D_MODEL), lambda i, lo, hi, nv: (0, 0)),
            pl.BlockSpec((1, D_MODEL), lambda i, lo, hi, nv: (0, 0)),
        ],
        out_specs=pl.BlockSpec(memory_space=pl.ANY),
        scratch_shapes=[
            pltpu.VMEM((2, tb, D_MODEL), F32), pltpu.VMEM((2, tb, D_MODEL), F32),
            pltpu.SemaphoreType.DMA((2,)), pltpu.SemaphoreType.DMA((2,)),
        ],
    )
    return pl.pallas_call(
        functools.partial(_moe_body, tb=tb), grid_spec=grid_spec,
        out_shape=jax.ShapeDtypeStruct((N, D_MODEL), F32),
        compiler_params=_params(("arbitrary",)), name="moe",
    )(t["lo"], t["hi"], t["nvalid"], t["src"], t["src"], t["rt"], h1,
      w["w_eg"], w["w_eu"], w["w_ed"], w["w_eg"], w["w_eu"], w["w_ed"], w["norm_moe"], w["norm_final"])


def _route_tables(route, tb):
    N = route.shape[0]
    bucket = route[:, 0].astype(jnp.int32)
    order = jnp.argsort(bucket).astype(jnp.int32)
    ids = jnp.arange(N_BUCKETS, dtype=jnp.int32)
    counts = jnp.sum((bucket[:, None] == ids[None, :]).astype(jnp.int32), axis=0)
    starts = jnp.cumsum(counts) - counts
    padded = ((counts + tb - 1) // tb) * tb
    pend = jnp.cumsum(padded)
    poff = pend - padded
    nblk = N // tb + N_BUCKETS + 1
    blk_start = jnp.arange(nblk, dtype=jnp.int32) * tb
    active = blk_start < pend[-1]
    bb = jnp.minimum(jnp.sum((pend[None, :] <= blk_start[:, None]).astype(jnp.int32), axis=1), N_BUCKETS - 1)
    r0 = blk_start - poff[bb]
    nvalid = jnp.where(active, jnp.clip(counts[bb] - r0, 0, tb), 0).astype(jnp.int32)
    bb_w = jnp.where(active, bb, bb[jnp.maximum(jnp.sum(active.astype(jnp.int32)) - 1, 0)])
    rows = starts[bb][:, None] + r0[:, None] + jnp.arange(tb, dtype=jnp.int32)[None, :]
    src = order[jnp.clip(rows, 0, N - 1)]
    return {
        "lo": jnp.asarray(_BUCKET_LO)[bb_w], "hi": jnp.asarray(_BUCKET_HI)[bb_w], "nvalid": nvalid,
        "src": src[:, None, :], "rt": route[:, 1:3][src],
    }


def _rope_tables(pos):
    inv_freq = 1.0 / (ROPE_THETA ** (jnp.arange(ROPE_HALF, dtype=F32) / ROPE_HALF))
    ang = pos.astype(F32)[:, None] * inv_freq[None, :]
    c, s = jnp.cos(ang), jnp.sin(ang)
    z = jnp.zeros((pos.shape[0], LANES - QK_ROPE_DIM), F32)
    return {
        "cos_nat": jnp.concatenate([c, c, z], axis=1), "sin_nat": jnp.concatenate([s, s, z], axis=1),
        "cosT": c.T, "sinT": s.T,
    }


def _unit(k, n):
    ang = k.astype(F32) * (-2.0 * math.pi / n)
    return jnp.cos(ang), jnp.sin(ang)


def _fnet_tables(S, tkf):
    L = S + N_META
    j = jnp.arange(S, dtype=jnp.int32)[:, None] + N_META
    er, ei = _unit((j * jnp.arange(tkf, dtype=jnp.int32)[None, :]) % L, L)
    col0 = jnp.arange(S // tkf, dtype=jnp.int32)[None, :] * tkf + N_META
    pr, pi = _unit((j * col0) % L, L)
    mr, mi = _unit((j * jnp.arange(N_META, dtype=jnp.int32)[None, :]) % L, L)
    return {"er": er, "ei": ei, "pr": pr, "pi": pi, "wm": jnp.concatenate([mr, mi], axis=1).astype(BF16)}


def _channel_dft(L):
    c = jnp.arange(FNET_GROUP_DIM, dtype=jnp.int32)
    cr, ci = _unit((c[:, None] * c[None, :]) % FNET_GROUP_DIM, FNET_GROUP_DIM)
    norm = 1.0 / math.sqrt(L * FNET_GROUP_DIM)
    eye = jnp.eye(N_FNET_GROUPS, dtype=F32)
    return (jnp.kron(eye, cr * norm).astype(BF16), jnp.kron(eye, -ci * norm).astype(BF16))


def _prep_weights(norm_attn, w_in, g_q, w_uq, g_kv, w_uk, w_uv, w_o_mla, w_fo, w_out, norm_moe,
                  w_rg, b_rg, w_re, b_re, w_eg, w_eu, w_ed, norm_final):
    s1 = Q_LORA_RANK
    s2 = s1 + KV_LORA_RANK
    s3 = s2 + QK_ROPE_DIM
    s4 = s3 + FNET_DIM
    w_in_p = jnp.concatenate(
        [w_in[:, :s1], w_in[:, s1:s2], w_in[:, s3:s4], w_in[:, s4:], w_in[:, s2:s3],
         jnp.zeros((D_MODEL, LANES - QK_ROPE_DIM), w_in.dtype)], axis=1)
    w_uq_p = jnp.pad(w_uq, ((0, 0), (0, 0), (0, 2 * LANES - QK_DIM)))
    r = jnp.concatenate([w_re, w_rg, jnp.zeros((D_MODEL, LANES - N_EXPERTS - N_EXPERT_GROUPS), F32)], axis=1)
    r_hi = r.astype(BF16)
    r_b = jnp.concatenate([b_re, b_rg, jnp.zeros((LANES - N_EXPERTS - N_EXPERT_GROUPS,), F32)])[None, :]
    return {
        "norm_attn": norm_attn[None, :], "w_in": w_in_p.astype(BF16),
        "g_q": g_q[None, :], "w_uq": w_uq_p.reshape(Q_LORA_RANK, N_HEADS * 2 * LANES).astype(BF16),
        "g_kv": g_kv[None, :], "w_ukT": w_uk.reshape(KV_LORA_RANK, N_HEADS * QK_NOPE_DIM).T.astype(BF16),
        "w_uv": w_uv.reshape(KV_LORA_RANK, N_HEADS * V_HEAD_DIM).astype(BF16),
        "w_o": w_o_mla.astype(BF16), "w_fo": w_fo.astype(BF16), "w_out": w_out.astype(BF16),
        "norm_moe": norm_moe[None, :], "r_hi": r_hi, "r_lo": (r - r_hi.astype(F32)).astype(BF16), "r_b": r_b,
        "w_eg": w_eg.reshape(N_EXPERTS, D_MODEL, D_FF_EXPERT).astype(BF16),
        "w_eu": w_eu.reshape(N_EXPERTS, D_MODEL, D_FF_EXPERT).astype(BF16),
        "w_ed": w_ed.reshape(N_EXPERTS, D_FF_EXPERT, D_MODEL).astype(BF16),
        "norm_final": norm_final[None, :],
    }


def _tiles(B, S):
    ts = min(512, S // 2)
    return {
        "ts": ts, "tq": min(2048, S), "tm": min(1024, S), "tkf": min(512, S),
        "tn": min(2048, B * FNET_DIM), "tb": 128,
    }


def _trunk(x, meta, w):
    B, S, _ = x.shape
    L = S + N_META
    t = _tiles(B, S)
    wl = dict(w)
    wl["cc"], wl["sc"] = _channel_dft(L)

    meta_pad = jnp.zeros((1, META_PAD, D_MODEL), F32).at[0, :N_META].set(meta)
    _, kT_m, v_m, a_m, b_m, _ = _in_proj(meta_pad, wl, _rope_tables(jnp.arange(META_PAD)), META_PAD)
    q, kT, v, a, b, gates = _in_proj(x, wl, _rope_tables(jnp.arange(S) + N_META), t["ts"])

    v_meta = v_m[0].reshape(META_PAD, N_HEADS, V_HEAD_DIM).transpose(1, 0, 2)
    att = _attention(q, kT, v, kT_m[0, :, 0], v_meta, t["tq"])
    abm = jnp.concatenate([a_m[:N_META], b_m[:N_META]], axis=0)
    f = _fnet(a, b, _fnet_tables(S, t["tkf"]), abm, t["tm"], t["tkf"], t["tn"])
    h1, route = _post_mix(x, att, f, gates, wl, t["ts"])
    y = _moe(h1.reshape(B * S, D_MODEL), route.reshape(B * S, LANES), wl, t["tb"])
    return y.reshape(B, S, D_MODEL)


def kernel(x_prompt, x_sample, meta_tokens, norm_attn, w_in, g_q, w_uq, g_kv, w_uk, w_uv, w_o_mla, w_fo, w_out,
           norm_moe, w_router_group, b_router_group, w_router_expert, b_router_expert, w_exp_gate, w_exp_up,
           w_exp_down, norm_final):
    w = _prep_weights(norm_attn[0], w_in[0], g_q[0], w_uq[0], g_kv[0], w_uk[0], w_uv[0], w_o_mla[0], w_fo[0],
                      w_out[0], norm_moe[0], w_router_group[0], b_router_group[0], w_router_expert[0],
                      b_router_expert[0], w_exp_gate[0], w_exp_up[0], w_exp_down[0], norm_final)
    return _trunk(x_prompt, meta_tokens, w), _trunk(x_sample, meta_tokens, w)
```

```python
import functools
import math

import jax
import jax.numpy as jnp
import numpy as np
from jax import lax
from jax.experimental import pallas as pl
from jax.experimental.pallas import tpu as pltpu

D_MODEL = 1024
N_META = 16
N_HEADS = 8
QK_NOPE_DIM = 128
QK_ROPE_DIM = 64
ROPE_HALF = QK_ROPE_DIM // 2
QK_DIM = QK_NOPE_DIM + QK_ROPE_DIM
V_HEAD_DIM = 128
Q_LORA_RANK = 384
KV_LORA_RANK = 256
ROPE_THETA = 10000.0
N_FNET_GROUPS = 4
FNET_GROUP_DIM = 128
FNET_DIM = N_FNET_GROUPS * FNET_GROUP_DIM
N_EXPERT_GROUPS = 4
EXPERTS_PER_GROUP = 8
N_EXPERTS = N_EXPERT_GROUPS * EXPERTS_PER_GROUP
D_FF_EXPERT = 256
EPS = 1e-6
PAIRS_PER_GROUP = EXPERTS_PER_GROUP * (EXPERTS_PER_GROUP - 1) // 2
N_BUCKETS = N_EXPERT_GROUPS * PAIRS_PER_GROUP
_CHANNEL_FLIP = np.array([g * FNET_GROUP_DIM + (FNET_GROUP_DIM - c) % FNET_GROUP_DIM
                          for g in range(N_FNET_GROUPS) for c in range(FNET_GROUP_DIM)], np.int32)
_PAIRS = [(a, b) for a in range(EXPERTS_PER_GROUP) for b in range(a + 1, EXPERTS_PER_GROUP)]
_BUCKET_LO = np.array([g * EXPERTS_PER_GROUP + a for g in range(N_EXPERT_GROUPS) for a, _ in _PAIRS], np.int32)
_BUCKET_HI = np.array([g * EXPERTS_PER_GROUP + b for g in range(N_EXPERT_GROUPS) for _, b in _PAIRS], np.int32)

LANES = 128
META_PAD = 128
VMEM_LIMIT_BYTES = 56 * 1024 * 1024
NEG_BIG = -1e30

_CQ0, _CQ1 = 0, Q_LORA_RANK
_CKV0, _CKV1 = _CQ1, _CQ1 + KV_LORA_RANK
_UF0, _UF1 = _CKV1, _CKV1 + FNET_DIM
_G0, _G1 = _UF1, _UF1 + 2 * D_MODEL
_KR0, _KR1 = _G1, _G1 + LANES
IN_DIM_PADDED = _KR1

F32 = jnp.float32
BF16 = jnp.bfloat16


def _dot(a, b):
    return jnp.dot(a, b, preferred_element_type=F32)


def _rms(x, g):
    ms = jnp.mean(x * x, axis=-1, keepdims=True)
    return x * lax.rsqrt(ms + EPS) * g


def _params(semantics):
    return pltpu.CompilerParams(dimension_semantics=semantics, vmem_limit_bytes=VMEM_LIMIT_BYTES)


def _in_proj_body(x_ref, na_ref, win_ref, gq_ref, wuq_ref, gkv_ref, wukT_ref, wuv_ref,
                  cosn_ref, sinn_ref, cosT_ref, sinT_ref, cc_ref, sc_ref,
                  q_ref, kT_ref, v_ref, a_ref, b_ref, gate_ref):
    x = x_ref[0]
    hb = _rms(x, na_ref[...]).astype(BF16)

    cqn = _rms(_dot(hb, win_ref[:, _CQ0:_CQ1]), gq_ref[...])
    qa = _dot(cqn.astype(BF16), wuq_ref[...])
    cn = cosn_ref[...]
    sn = sinn_ref[...]
    first_half = lax.broadcasted_iota(jnp.int32, cn.shape, 1) < ROPE_HALF
    scale = math.log2(math.e) / math.sqrt(QK_DIM)
    for h in range(N_HEADS):
        c0 = h * 2 * LANES
        qr = qa[:, c0 + LANES:c0 + 2 * LANES]
        rot = jnp.where(first_half, -pltpu.roll(qr, LANES - ROPE_HALF, 1), pltpu.roll(qr, ROPE_HALF, 1))
        q_ref[0, h, :, 0:QK_NOPE_DIM] = (qa[:, c0:c0 + LANES] * scale).astype(BF16)
        q_ref[0, h, :, QK_NOPE_DIM:QK_DIM] = ((qr * cn + rot * sn) * scale)[:, 0:QK_ROPE_DIM].astype(BF16)

    ckvn = _rms(_dot(hb, win_ref[:, _CKV0:_CKV1]), gkv_ref[...])
    knT = _dot(wukT_ref[...], ckvn.T.astype(BF16))
    krT = _dot(hb, win_ref[:, _KR0:_KR1]).T
    x1 = krT[0:ROPE_HALF]
    x2 = krT[ROPE_HALF:QK_ROPE_DIM]
    cT = cosT_ref[...]
    sT = sinT_ref[...]
    r1 = (x1 * cT - x2 * sT).astype(BF16)
    r2 = (x2 * cT + x1 * sT).astype(BF16)
    for h in range(N_HEADS):
        kT_ref[0, h, 0, 0:QK_NOPE_DIM, :] = knT[h * QK_NOPE_DIM:(h + 1) * QK_NOPE_DIM].astype(BF16)
        kT_ref[0, h, 0, QK_NOPE_DIM:QK_NOPE_DIM + ROPE_HALF, :] = r1
        kT_ref[0, h, 0, QK_NOPE_DIM + ROPE_HALF:QK_DIM, :] = r2
    v_ref[0] = _dot(ckvn.astype(BF16), wuv_ref[...]).astype(BF16)

    ub = _dot(hb, win_ref[:, _UF0:_UF1]).astype(BF16)
    a_ref[...] = _dot(ub, cc_ref[...]).astype(BF16)
    b_ref[...] = _dot(ub, sc_ref[...]).astype(BF16)

    gl = _dot(hb, win_ref[:, _G0:_G1])
    gate_ref[0] = (1.0 / (1.0 + jnp.exp(-gl))).astype(BF16)


def _in_proj(x, w, rope, ts):
    B, S, _ = x.shape
    nt = S // ts
    const = lambda shape: pl.BlockSpec(shape, lambda b, t: (0,) * len(shape))
    out_shape = (
        jax.ShapeDtypeStruct((B, N_HEADS, S, QK_DIM), BF16),
        jax.ShapeDtypeStruct((B, N_HEADS, nt, QK_DIM, ts), BF16),
        jax.ShapeDtypeStruct((B, S, N_HEADS * V_HEAD_DIM), BF16),
        jax.ShapeDtypeStruct((S, B * FNET_DIM), BF16),
        jax.ShapeDtypeStruct((S, B * FNET_DIM), BF16),
        jax.ShapeDtypeStruct((B, S, 2 * D_MODEL), BF16),
    )
    in_specs = [
        pl.BlockSpec((1, ts, D_MODEL), lambda b, t: (b, t, 0)),
        const((1, D_MODEL)),
        const((D_MODEL, IN_DIM_PADDED)),
        const((1, Q_LORA_RANK)),
        const((Q_LORA_RANK, N_HEADS * 2 * LANES)),
        const((1, KV_LORA_RANK)),
        const((N_HEADS * QK_NOPE_DIM, KV_LORA_RANK)),
        const((KV_LORA_RANK, N_HEADS * V_HEAD_DIM)),
        pl.BlockSpec((ts, LANES), lambda b, t: (t, 0)),
        pl.BlockSpec((ts, LANES), lambda b, t: (t, 0)),
        pl.BlockSpec((ROPE_HALF, ts), lambda b, t: (0, t)),
        pl.BlockSpec((ROPE_HALF, ts), lambda b, t: (0, t)),
        const((FNET_DIM, FNET_DIM)),
        const((FNET_DIM, FNET_DIM)),
    ]
    out_specs = (
        pl.BlockSpec((1, N_HEADS, ts, QK_DIM), lambda b, t: (b, 0, t, 0)),
        pl.BlockSpec((1, N_HEADS, 1, QK_DIM, ts), lambda b, t: (b, 0, t, 0, 0)),
        pl.BlockSpec((1, ts, N_HEADS * V_HEAD_DIM), lambda b, t: (b, t, 0)),
        pl.BlockSpec((ts, FNET_DIM), lambda b, t: (t, b)),
        pl.BlockSpec((ts, FNET_DIM), lambda b, t: (t, b)),
        pl.BlockSpec((1, ts, 2 * D_MODEL), lambda b, t: (b, t, 0)),
    )
    return pl.pallas_call(
        _in_proj_body, grid=(B, nt), in_specs=in_specs, out_specs=out_specs, out_shape=out_shape,
        compiler_params=_params(("parallel", "parallel")), name="in_proj",
    )(x, w["norm_attn"], w["w_in"], w["g_q"], w["w_uq"], w["g_kv"], w["w_ukT"], w["w_uv"],
      rope["cos_nat"], rope["sin_nat"], rope["cosT"], rope["sinT"], w["cc"], w["sc"])


def _attn_body(q_ref, kT_ref, v_ref, kmT_ref, vm_ref, o_ref, s_ref, p_ref, acc_ref, *, nk, tk):
    q = q_ref[0, 0]

    def scores(j, slot):
        s_ref[slot] = _dot(q, kT_ref[0, 0, j])

    def values(j, slot, alpha):
        v = v_ref[0, pl.ds(pl.multiple_of(j * tk, tk), tk), :]
        acc_ref[...] = alpha * acc_ref[...] + _dot(p_ref[slot], v)

    def softmax(slot, m, l):
        s = s_ref[slot]
        m_new = jnp.maximum(m, jnp.max(s, axis=1, keepdims=True))
        alpha = jnp.exp2(m - m_new)
        p = jnp.exp2(s - m_new)
        p_ref[slot] = p.astype(BF16)
        return m_new, alpha * l + jnp.sum(p, axis=1, keepdims=True), alpha

    s = _dot(q, kmT_ref[0])
    col = lax.broadcasted_iota(jnp.int32, s.shape, 1)
    s = jnp.where(col < N_META, s, NEG_BIG)
    m = jnp.max(s, axis=1, keepdims=True)
    p = jnp.exp2(s - m)
    l = jnp.sum(p, axis=1, keepdims=True)
    acc_ref[...] = _dot(p.astype(BF16), vm_ref[0])

    p_ref[1] = jnp.zeros(p_ref.shape[1:], BF16)
    scores(0, 0)

    def pair(jj, carry, last):
        m, l, alpha = carry
        a = 2 * jj
        values(jnp.maximum(a - 1, 0), 1, alpha)
        scores(a + 1, 1)
        m, l, alpha = softmax(0, m, l)
        values(a, 0, alpha)
        if not last:
            scores(a + 2, 0)
        return softmax(1, m, l)

    carry = (m, l, jnp.ones_like(m))
    carry = lax.fori_loop(0, nk // 2 - 1, functools.partial(pair, last=False), carry)
    m, l, alpha = pair(nk // 2 - 1, carry, last=True)
    values(nk - 1, 1, alpha)
    o_ref[0] = (acc_ref[...] / l).astype(BF16)


def _attention(q, kT, v, kT_meta, v_meta, tq):
    B, H, S, _ = q.shape
    nk, tk = kT.shape[2], kT.shape[4]
    assert nk % 2 == 0, "the key loop is unrolled in pairs"
    return pl.pallas_call(
        functools.partial(_attn_body, nk=nk, tk=tk),
        grid=(B, H, S // tq),
        in_specs=[
            pl.BlockSpec((1, 1, tq, QK_DIM), lambda b, h, i: (b, h, i, 0)),
            pl.BlockSpec((1, 1, nk, QK_DIM, tk), lambda b, h, i: (b, h, 0, 0, 0)),
            pl.BlockSpec((1, S, V_HEAD_DIM), lambda b, h, i: (b, 0, h)),
            pl.BlockSpec((1, QK_DIM, META_PAD), lambda b, h, i: (h, 0, 0)),
            pl.BlockSpec((1, META_PAD, V_HEAD_DIM), lambda b, h, i: (h, 0, 0)),
        ],
        out_specs=pl.BlockSpec((1, tq, V_HEAD_DIM), lambda b, h, i: (b, i, h)),
        out_shape=jax.ShapeDtypeStruct((B, S, H * V_HEAD_DIM), BF16),
        scratch_shapes=[
            pltpu.VMEM((2, tq, tk), F32), pltpu.VMEM((2, tq, tk), BF16), pltpu.VMEM((tq, V_HEAD_DIM), F32),
        ],
        compiler_params=_params(("parallel", "parallel", "parallel")), name="attention",
    )(q, kT, v, kT_meta, v_meta)


def _fnet_body(a_ref, b_ref, er_ref, ei_ref, pr_ref, pi_ref, wm_ref, abm_ref, o_ref, acc_ref, *, reps):
    ci = pl.program_id(2)

    @pl.when(ci == 0)
    def _():
        mt = _dot(wm_ref[...], abm_ref[...])
        acc_ref[...] = jnp.concatenate([mt] * reps, axis=1)

    lane = lax.broadcasted_iota(jnp.int32, pr_ref.shape, 1)
    sel = lane == ci
    pr = jnp.sum(jnp.where(sel, pr_ref[...], 0.0), axis=1, keepdims=True)
    pi = jnp.sum(jnp.where(sel, pi_ref[...], 0.0), axis=1, keepdims=True)
    er = er_ref[...]
    ei = ei_ref[...]
    wr = (pr * er - pi * ei).astype(BF16)
    wi = (pr * ei + pi * er).astype(BF16)
    acc_ref[...] += _dot(wr, a_ref[...]) + _dot(wi, b_ref[...])

    @pl.when(ci == pl.num_programs(2) - 1)
    def _():
        o_ref[...] = acc_ref[...]


def _fnet(a, b, tabs, abm, tm, tkf, tn):
    S, NB = a.shape
    rows = tabs["er"].shape[0]
    nkt = S // tkf
    return pl.pallas_call(
        functools.partial(_fnet_body, reps=tn // FNET_DIM),
        grid=(NB // tn, rows // tm, nkt),
        in_specs=[
            pl.BlockSpec((tkf, tn), lambda n, r, c: (c, n)),
            pl.BlockSpec((tkf, tn), lambda n, r, c: (c, n)),
            pl.BlockSpec((tm, tkf), lambda n, r, c: (r, 0)),
            pl.BlockSpec((tm, tkf), lambda n, r, c: (r, 0)),
            pl.BlockSpec((tm, nkt), lambda n, r, c: (r, 0)),
            pl.BlockSpec((tm, nkt), lambda n, r, c: (r, 0)),
            pl.BlockSpec((tm, 2 * N_META), lambda n, r, c: (r, 0)),
            pl.BlockSpec((2 * N_META, FNET_DIM), lambda n, r, c: (0, 0)),
        ],
        out_specs=pl.BlockSpec((tm, tn), lambda n, r, c: (r, n)),
        out_shape=jax.ShapeDtypeStruct((rows, NB), F32),
        scratch_shapes=[pltpu.VMEM((tm, tn), F32)],
        compiler_params=_params(("parallel", "parallel", "arbitrary")), name="fnet",
    )(a, b, tabs["er"], tabs["ei"], tabs["pr"], tabs["pi"], tabs["wm"], abm)


def _fnet_mirror_body(d_hbm, x0_hbm, m_hbm, sem, *, half):
    n_main = half - N_META + 1

    def main_copy(i):
        return pltpu.make_async_copy(d_hbm.at[half - N_META - i], m_hbm.at[i], sem)

    def low_copy(u):
        return pltpu.make_async_copy(x0_hbm.at[u], m_hbm.at[half - u], sem)

    def start(i, c):
        main_copy(i).start()
        return c

    def wait(i, c):
        main_copy(i).wait()
        return c

    lax.fori_loop(0, n_main, start, 0)
    for u in range(1, N_META):
        low_copy(u).start()
    lax.fori_loop(0, n_main, wait, 0)
    for u in range(1, N_META):
        low_copy(u).wait()


def _fnet_mirror(d, x0):
    half, NB = d.shape
    return pl.pallas_call(
        functools.partial(_fnet_mirror_body, half=half),
        in_specs=[pl.BlockSpec(memory_space=pl.ANY), pl.BlockSpec(memory_space=pl.ANY)],
        out_specs=pl.BlockSpec(memory_space=pl.ANY),
        out_shape=jax.ShapeDtypeStruct((half, NB), F32),
        scratch_shapes=[pltpu.SemaphoreType.DMA(())],
        name="fnet_mirror",
    )(d, x0)


def _post_mix_body(x_ref, att_ref, flo_ref, fhi_ref, gate_ref, wo_ref, wfo_ref, wout_ref, nm_ref,
                   rhi_ref, rlo_ref, rb_ref, h1_ref, route_ref):
    a = _dot(att_ref[0], wo_ref[...])
    first_half = pl.program_id(1) < pl.num_programs(1) // 2
    f = jnp.where(first_half, flo_ref[...], fhi_ref[...])
    bb = _dot(f.astype(BF16), wfo_ref[0])
    g = gate_ref[0].astype(F32)
    merged = g[:, :D_MODEL] * a + g[:, D_MODEL:] * bb
    h1 = x_ref[0] + _dot(merged.astype(BF16), wout_ref[...])
    h1_ref[0] = h1
    xn = _rms(h1, nm_ref[...])
    xh = xn.astype(BF16)

    xl = (xn - xh.astype(F32)).astype(BF16)
    logits = _dot(xh, rhi_ref[...]) + (_dot(xh, rlo_ref[...]) + _dot(xl, rhi_ref[...])) + rb_ref[...]
    lane = lax.broadcasted_iota(jnp.int32, logits.shape, 1)
    lanef = lane.astype(F32)
    far = float(2 * LANES)

    gl = jnp.where((lane >= N_EXPERTS) & (lane < N_EXPERTS + N_EXPERT_GROUPS), logits, NEG_BIG)
    gmax = jnp.max(gl, axis=1, keepdims=True)
    g_w = 1.0 / jnp.sum(jnp.exp(gl - gmax), axis=1, keepdims=True)
    gidx = jnp.min(jnp.where(gl == gmax, lanef, far), axis=1, keepdims=True) - float(N_EXPERTS)

    grp = (lane >> 3).astype(F32)
    el = jnp.where((lane < N_EXPERTS) & (grp == gidx), logits, NEG_BIG)
    m1 = jnp.max(el, axis=1, keepdims=True)
    i1 = jnp.min(jnp.where(el == m1, lanef, far), axis=1, keepdims=True)
    el2 = jnp.where(lanef == i1, NEG_BIG, el)
    m2 = jnp.max(el2, axis=1, keepdims=True)
    i2 = jnp.min(jnp.where(el2 == m2, lanef, far), axis=1, keepdims=True)
    r = jnp.exp(m2 - m1)
    w1 = g_w / (1.0 + r)
    w2 = g_w * r / (1.0 + r)

    first_is_lo = i1 < i2
    a_loc = jnp.minimum(i1, i2) - gidx * EXPERTS_PER_GROUP
    b_loc = jnp.maximum(i1, i2) - gidx * EXPERTS_PER_GROUP
    pair = a_loc * (2 * EXPERTS_PER_GROUP - 1 - a_loc) * 0.5 + (b_loc - a_loc - 1.0)
    bucket = gidx * PAIRS_PER_GROUP + pair
    w_lo = jnp.where(first_is_lo, w1, w2)
    w_hi = jnp.where(first_is_lo, w2, w1)
    route_ref[0] = jnp.where(lane == 0, bucket, jnp.where(lane == 1, w_lo, jnp.where(lane == 2, w_hi, 0.0)))


def _post_mix(x, att, f_lo, f_hi, gates, w, ts):
    B, S, _ = x.shape
    nh = S // ts // 2
    const = lambda shape: pl.BlockSpec(shape, lambda b, t: (0,) * len(shape))
    tok = lambda width: pl.BlockSpec((1, ts, width), lambda b, t: (b, t, 0))
    return pl.pallas_call(
        _post_mix_body, grid=(B, S // ts),
        in_specs=[
            tok(D_MODEL), tok(D_MODEL),
            pl.BlockSpec((ts, FNET_DIM), lambda b, t: (jnp.minimum(t, nh - 1), b)),
            pl.BlockSpec((ts, FNET_DIM), lambda b, t: (jnp.maximum(t - nh, 0), b)),
            tok(2 * D_MODEL),
            const((D_MODEL, D_MODEL)),
            pl.BlockSpec((1, FNET_DIM, D_MODEL), lambda b, t: (jnp.where(t < nh, 0, 1), 0, 0)),
            const((D_MODEL, D_MODEL)),
            const((1, D_MODEL)), const((D_MODEL, LANES)), const((D_MODEL, LANES)), const((1, LANES)),
        ],
        out_specs=(tok(D_MODEL), tok(LANES)),
        out_shape=(
            jax.ShapeDtypeStruct((B, S, D_MODEL), F32),
            jax.ShapeDtypeStruct((B, S, LANES), F32),
        ),
        compiler_params=_params(("parallel", "parallel")), name="post_mix",
    )(x, att, f_lo, f_hi, gates, w["w_o"], w["w_fo"], w["w_out"], w["norm_moe"], w["r_hi"], w["r_lo"], w["r_b"])


def _moe_body(lo_ref, hi_ref, nv_ref, src_ref, nxt_ref, rt_ref, h1_hbm,
              wgl_ref, wul_ref, wdl_ref, wgh_ref, wuh_ref, wdh_ref, nm_ref, nf_ref,
              out_hbm, xbuf, ybuf, gsem, ssem, *, tb):
    i = pl.program_id(0)
    slot = lax.rem(i, 2)
    other = 1 - slot
    cnt = nv_ref[i]
    prev = jnp.where(i > 0, nv_ref[jnp.maximum(i - 1, 0)], 0)

    def gather(idx_ref, s, r):
        return pltpu.make_async_copy(h1_hbm.at[idx_ref[0, 0, r]], xbuf.at[s, r], gsem.at[s])

    def scatter_wait(s, rows):
        full = pl.multiple_of(lax.shift_left(lax.shift_right_logical(rows, 3), 3), 8)

        @pl.when(full > 0)
        def _():
            pltpu.make_async_copy(ybuf.at[s, pl.ds(0, full)], out_hbm.at[pl.ds(0, full)], ssem.at[s]).wait()

        def one(r, c):
            pltpu.make_async_copy(ybuf.at[s, r], out_hbm.at[0], ssem.at[s]).wait()
            return c

        lax.fori_loop(full, rows, one, 0)

    @pl.when(i == 0)
    def _():
        for r in range(tb):
            gather(src_ref, 0, r).start()

    @pl.when(cnt > 0)
    def _():
        for r in range(tb):
            gather(src_ref, slot, r).wait()
        for r in range(tb):
            gather(nxt_ref, other, r).start()
        x = xbuf[slot]
        xb = _rms(x, nm_ref[...]).astype(BF16)
        rt = rt_ref[0]

        def expert(wg_ref, wu_ref, wd_ref, wcol):
            gt = _dot(xb, wg_ref[0])
            up = _dot(xb, wu_ref[0])
            act = (gt / (1.0 + jnp.exp(-gt))) * up * wcol
            return _dot(act.astype(BF16), wd_ref[0])

        y = expert(wgl_ref, wul_ref, wdl_ref, rt[:, 0:1]) + expert(wgh_ref, wuh_ref, wdh_ref, rt[:, 1:2])
        ybuf[slot] = _rms(x + y, nf_ref[...])

        scatter_wait(other, prev)

        def issue(r):
            pltpu.make_async_copy(ybuf.at[slot, r], out_hbm.at[src_ref[0, 0, r]], ssem.at[slot]).start()

        def issue8(g, c):
            for u in range(8):
                issue(g * 8 + u)
            return c

        def issue1(r, c):
            issue(r)
            return c

        groups = lax.shift_right_logical(cnt, 3)
        lax.fori_loop(0, groups, issue8, 0)
        lax.fori_loop(groups * 8, cnt, issue1, 0)

    @pl.when((cnt == 0) & (prev > 0))
    def _():
        for r in range(tb):
            gather(src_ref, slot, r).wait()
        scatter_wait(other, prev)


def _moe(h1, route, w, tb):
    N = h1.shape[0]
    t = _route_tables(route, tb)
    nblk = t["src"].shape[0]
    idx_spec = lambda f: pl.BlockSpec((1, 1, tb), f, memory_space=pltpu.SMEM)
    wspec = lambda shape, which: pl.BlockSpec(
        (1,) + shape, (lambda i, lo, hi, nv: (lo[i], 0, 0)) if which == 0 else (lambda i, lo, hi, nv: (hi[i], 0, 0)))
    up_shape, down_shape = (D_MODEL, D_FF_EXPERT), (D_FF_EXPERT, D_MODEL)
    grid_spec = pltpu.PrefetchScalarGridSpec(
        num_scalar_prefetch=3, grid=(nblk,),
        in_specs=[
            idx_spec(lambda i, lo, hi, nv: (i, 0, 0)),
            idx_spec(lambda i, lo, hi, nv: (jnp.minimum(i + 1, nblk - 1), 0, 0)),
            pl.BlockSpec((1, tb, 2), lambda i, lo, hi, nv: (i, 0, 0)),
            pl.BlockSpec(memory_space=pl.ANY),
            wspec(up_shape, 0), wspec(up_shape, 0), wspec(down_shape, 0),
            wspec(up_shape, 1), wspec(up_shape, 1), wspec(down_shape, 1),
            pl.BlockSpec((1, D_MODEL), lambda i, lo, hi, nv: (0, 0)),
            pl.BlockSpec((1, D_MODEL), lambda i, lo, hi, nv: (0, 0)),
        ],
        out_specs=pl.BlockSpec(memory_space=pl.ANY),
        scratch_shapes=[
            pltpu.VMEM((2, tb, D_MODEL), F32), pltpu.VMEM((2, tb, D_MODEL), F32),
            pltpu.SemaphoreType.DMA((2,)), pltpu.SemaphoreType.DMA((2,)),
        ],
    )
    return pl.pallas_call(
        functools.partial(_moe_body, tb=tb), grid_spec=grid_spec,
        out_shape=jax.ShapeDtypeStruct((N, D_MODEL), F32),
        compiler_params=_params(("arbitrary",)), name="moe",
    )(t["lo"], t["hi"], t["nvalid"], t["src"], t["src"], t["rt"], h1,
      w["w_eg"], w["w_eu"], w["w_ed"], w["w_eg"], w["w_eu"], w["w_ed"], w["norm_moe"], w["norm_final"])


def _route_tables(route, tb):
    N = route.shape[0]
    bucket = route[:, 0].astype(jnp.int32)
    order = jnp.argsort(bucket).astype(jnp.int32)
    ids = jnp.arange(N_BUCKETS, dtype=jnp.int32)
    counts = jnp.sum((bucket[:, None] == ids[None, :]).astype(jnp.int32), axis=0)
    starts = jnp.cumsum(counts) - counts
    padded = ((counts + tb - 1) // tb) * tb
    pend = jnp.cumsum(padded)
    poff = pend - padded
    nblk = N // tb + N_BUCKETS + 1
    blk_start = jnp.arange(nblk, dtype=jnp.int32) * tb
    active = blk_start < pend[-1]
    bb = jnp.minimum(jnp.sum((pend[None, :] <= blk_start[:, None]).astype(jnp.int32), axis=1), N_BUCKETS - 1)
    r0 = blk_start - poff[bb]
    nvalid = jnp.where(active, jnp.clip(counts[bb] - r0, 0, tb), 0).astype(jnp.int32)
    bb_w = jnp.where(active, bb, bb[jnp.maximum(jnp.sum(active.astype(jnp.int32)) - 1, 0)])
    rows = starts[bb][:, None] + r0[:, None] + jnp.arange(tb, dtype=jnp.int32)[None, :]
    src = order[jnp.clip(rows, 0, N - 1)]
    return {
        "lo": jnp.asarray(_BUCKET_LO)[bb_w], "hi": jnp.asarray(_BUCKET_HI)[bb_w], "nvalid": nvalid,
        "src": src[:, None, :], "rt": route[:, 1:3][src],
    }


def _rope_tables(pos):
    inv_freq = 1.0 / (ROPE_THETA ** (jnp.arange(ROPE_HALF, dtype=F32) / ROPE_HALF))
    ang = pos.astype(F32)[:, None] * inv_freq[None, :]
    c, s = jnp.cos(ang), jnp.sin(ang)
    z = jnp.zeros((pos.shape[0], LANES - QK_ROPE_DIM), F32)
    return {
        "cos_nat": jnp.concatenate([c, c, z], axis=1), "sin_nat": jnp.concatenate([s, s, z], axis=1),
        "cosT": c.T, "sinT": s.T,
    }


def _unit(k, n):
    ang = k.astype(F32) * (-2.0 * math.pi / n)
    return jnp.cos(ang), jnp.sin(ang)


def _fnet_tables(S, tkf, freqs):
    L = S + N_META
    j = freqs.astype(jnp.int32)[:, None]
    er, ei = _unit((j * jnp.arange(tkf, dtype=jnp.int32)[None, :]) % L, L)
    col0 = jnp.arange(S // tkf, dtype=jnp.int32)[None, :] * tkf + N_META
    pr, pi = _unit((j * col0) % L, L)
    mr, mi = _unit((j * jnp.arange(N_META, dtype=jnp.int32)[None, :]) % L, L)
    return {"er": er, "ei": ei, "pr": pr, "pi": pi, "wm": jnp.concatenate([mr, mi], axis=1).astype(BF16)}


def _channel_dft(L):
    c = jnp.arange(FNET_GROUP_DIM, dtype=jnp.int32)
    cr, ci = _unit((c[:, None] * c[None, :]) % FNET_GROUP_DIM, FNET_GROUP_DIM)
    norm = 1.0 / math.sqrt(L * FNET_GROUP_DIM)
    eye = jnp.eye(N_FNET_GROUPS, dtype=F32)
    return (jnp.kron(eye, cr * norm).astype(BF16), jnp.kron(eye, -ci * norm).astype(BF16))


def _prep_weights(norm_attn, w_in, g_q, w_uq, g_kv, w_uk, w_uv, w_o_mla, w_fo, w_out, norm_moe,
                  w_rg, b_rg, w_re, b_re, w_eg, w_eu, w_ed, norm_final):
    s1 = Q_LORA_RANK
    s2 = s1 + KV_LORA_RANK
    s3 = s2 + QK_ROPE_DIM
    s4 = s3 + FNET_DIM
    w_in_p = jnp.concatenate(
        [w_in[:, :s1], w_in[:, s1:s2], w_in[:, s3:s4], w_in[:, s4:], w_in[:, s2:s3],
         jnp.zeros((D_MODEL, LANES - QK_ROPE_DIM), w_in.dtype)], axis=1)
    w_uq_p = jnp.pad(w_uq, ((0, 0), (0, 0), (0, 2 * LANES - QK_DIM)))
    r = jnp.concatenate([w_re, w_rg, jnp.zeros((D_MODEL, LANES - N_EXPERTS - N_EXPERT_GROUPS), F32)], axis=1)
    r_hi = r.astype(BF16)
    r_b = jnp.concatenate([b_re, b_rg, jnp.zeros((LANES - N_EXPERTS - N_EXPERT_GROUPS,), F32)])[None, :]
    return {
        "norm_attn": norm_attn[None, :], "w_in": w_in_p.astype(BF16),
        "g_q": g_q[None, :], "w_uq": w_uq_p.reshape(Q_LORA_RANK, N_HEADS * 2 * LANES).astype(BF16),
        "g_kv": g_kv[None, :], "w_ukT": w_uk.reshape(KV_LORA_RANK, N_HEADS * QK_NOPE_DIM).T.astype(BF16),
        "w_uv": w_uv.reshape(KV_LORA_RANK, N_HEADS * V_HEAD_DIM).astype(BF16),
        "w_o": w_o_mla.astype(BF16), "w_out": w_out.astype(BF16),
        "w_fo": jnp.stack([w_fo, w_fo[_CHANNEL_FLIP]]).astype(BF16),
        "norm_moe": norm_moe[None, :], "r_hi": r_hi, "r_lo": (r - r_hi.astype(F32)).astype(BF16), "r_b": r_b,
        "w_eg": w_eg.reshape(N_EXPERTS, D_MODEL, D_FF_EXPERT).astype(BF16),
        "w_eu": w_eu.reshape(N_EXPERTS, D_MODEL, D_FF_EXPERT).astype(BF16),
        "w_ed": w_ed.reshape(N_EXPERTS, D_FF_EXPERT, D_MODEL).astype(BF16),
        "norm_final": norm_final[None, :],
    }


def _tiles(B, S):
    ts = min(512, S // 2)
    return {
        "ts": ts, "tq": min(2048, S), "tm": min(1024, S // 2), "tkf": min(512, S),
        "tn": min(2048, B * FNET_DIM), "tb": 128,
    }


def _trunk(x, meta, w):
    B, S, _ = x.shape
    L = S + N_META
    t = _tiles(B, S)
    wl = dict(w)
    wl["cc"], wl["sc"] = _channel_dft(L)

    meta_pad = jnp.zeros((1, META_PAD, D_MODEL), F32).at[0, :N_META].set(meta)
    _, kT_m, v_m, a_m, b_m, _ = _in_proj(meta_pad, wl, _rope_tables(jnp.arange(META_PAD)), META_PAD)
    q, kT, v, a, b, gates = _in_proj(x, wl, _rope_tables(jnp.arange(S) + N_META), t["ts"])

    v_meta = v_m[0].reshape(META_PAD, N_HEADS, V_HEAD_DIM).transpose(1, 0, 2)
    att = _attention(q, kT, v, kT_m[0, :, 0], v_meta, t["tq"])
    abm = jnp.concatenate([a_m[:N_META], b_m[:N_META]], axis=0)
    half = S // 2
    f_lo = _fnet(a, b, _fnet_tables(S, t["tkf"], jnp.arange(half) + N_META), abm, t["tm"], t["tkf"], t["tn"])
    f_meta = _fnet(a, b, _fnet_tables(S, t["tkf"], jnp.arange(N_META)), abm, N_META, t["tkf"], t["tn"])
    f_hi = _fnet_mirror(f_lo, f_meta)
    h1, route = _post_mix(x, att, f_lo, f_hi, gates, wl, t["ts"])
    y = _moe(h1.reshape(B * S, D_MODEL), route.reshape(B * S, LANES), wl, t["tb"])
    return y.reshape(B, S, D_MODEL)


def kernel(x_prompt, x_sample, meta_tokens, norm_attn, w_in, g_q, w_uq, g_kv, w_uk, w_uv, w_o_mla, w_fo, w_out,
           norm_moe, w_router_group, b_router_group, w_router_expert, b_router_expert, w_exp_gate, w_exp_up,
           w_exp_down, norm_final):
    w = _prep_weights(norm_attn[0], w_in[0], g_q[0], w_uq[0], g_kv[0], w_uk[0], w_uv[0], w_o_mla[0], w_fo[0],
                      w_out[0], norm_moe[0], w_router_group[0], b_router_group[0], w_router_expert[0],
                      b_router_expert[0], w_exp_gate[0], w_exp_up[0], w_exp_down[0], norm_final)
    return _trunk(x_prompt, meta_tokens, w), _trunk(x_sample, meta_tokens, w)
```

```python
import functools
import math

import jax
import jax.numpy as jnp
import numpy as np
from jax import lax
from jax.experimental import pallas as pl
from jax.experimental.pallas import tpu as pltpu

D_MODEL = 1024
N_META = 16
N_HEADS = 8
QK_NOPE_DIM = 128
QK_ROPE_DIM = 64
ROPE_HALF = QK_ROPE_DIM // 2
QK_DIM = QK_NOPE_DIM + QK_ROPE_DIM
V_HEAD_DIM = 128
Q_LORA_RANK = 384
KV_LORA_RANK = 256
ROPE_THETA = 10000.0
N_FNET_GROUPS = 4
FNET_GROUP_DIM = 128
FNET_DIM = N_FNET_GROUPS * FNET_GROUP_DIM
N_EXPERT_GROUPS = 4
EXPERTS_PER_GROUP = 8
N_EXPERTS = N_EXPERT_GROUPS * EXPERTS_PER_GROUP
D_FF_EXPERT = 256
EPS = 1e-6
PAIRS_PER_GROUP = EXPERTS_PER_GROUP * (EXPERTS_PER_GROUP - 1) // 2
N_BUCKETS = N_EXPERT_GROUPS * PAIRS_PER_GROUP
_CHANNEL_FLIP = np.array([g * FNET_GROUP_DIM + (FNET_GROUP_DIM - c) % FNET_GROUP_DIM
                          for g in range(N_FNET_GROUPS) for c in range(FNET_GROUP_DIM)], np.int32)
_PAIRS = [(a, b) for a in range(EXPERTS_PER_GROUP) for b in range(a + 1, EXPERTS_PER_GROUP)]
_BUCKET_LO = np.array([g * EXPERTS_PER_GROUP + a for g in range(N_EXPERT_GROUPS) for a, _ in _PAIRS], np.int32)
_BUCKET_HI = np.array([g * EXPERTS_PER_GROUP + b for g in range(N_EXPERT_GROUPS) for _, b in _PAIRS], np.int32)

LANES = 128
META_PAD = 128
VMEM_LIMIT_BYTES = 56 * 1024 * 1024
NEG_BIG = -1e30

_CQ0, _CQ1 = 0, Q_LORA_RANK
_CKV0, _CKV1 = _CQ1, _CQ1 + KV_LORA_RANK
_UF0, _UF1 = _CKV1, _CKV1 + FNET_DIM
_G0, _G1 = _UF1, _UF1 + 2 * D_MODEL
_KR0, _KR1 = _G1, _G1 + LANES
IN_DIM_PADDED = _KR1

F32 = jnp.float32
BF16 = jnp.bfloat16


def _dot(a, b):
    return jnp.dot(a, b, preferred_element_type=F32)


def _rms(x, g):
    ms = jnp.mean(x * x, axis=-1, keepdims=True)
    return x * lax.rsqrt(ms + EPS) * g


def _params(semantics):
    return pltpu.CompilerParams(dimension_semantics=semantics, vmem_limit_bytes=VMEM_LIMIT_BYTES)


def _in_proj_body(x_ref, na_ref, win_ref, gq_ref, wuq_ref, gkv_ref, wukT_ref, wuv_ref,
                  cosn_ref, sinn_ref, cosT_ref, sinT_ref, cc_ref, sc_ref,
                  q_ref, kT_ref, v_ref, a_ref, b_ref, gate_ref):
    x = x_ref[0]
    hb = _rms(x, na_ref[...]).astype(BF16)

    cqn = _rms(_dot(hb, win_ref[:, _CQ0:_CQ1]), gq_ref[...])
    qa = _dot(cqn.astype(BF16), wuq_ref[...])
    cn = cosn_ref[...]
    sn = sinn_ref[...]
    first_half = lax.broadcasted_iota(jnp.int32, cn.shape, 1) < ROPE_HALF
    scale = math.log2(math.e) / math.sqrt(QK_DIM)
    for h in range(N_HEADS):
        c0 = h * 2 * LANES
        qr = qa[:, c0 + LANES:c0 + 2 * LANES]
        rot = jnp.where(first_half, -pltpu.roll(qr, LANES - ROPE_HALF, 1), pltpu.roll(qr, ROPE_HALF, 1))
        q_ref[0, h, :, 0:QK_NOPE_DIM] = (qa[:, c0:c0 + LANES] * scale).astype(BF16)
        q_ref[0, h, :, QK_NOPE_DIM:QK_DIM] = ((qr * cn + rot * sn) * scale)[:, 0:QK_ROPE_DIM].astype(BF16)

    ckvn = _rms(_dot(hb, win_ref[:, _CKV0:_CKV1]), gkv_ref[...])
    knT = _dot(wukT_ref[...], ckvn.T.astype(BF16))
    krT = _dot(hb, win_ref[:, _KR0:_KR1]).T
    x1 = krT[0:ROPE_HALF]
    x2 = krT[ROPE_HALF:QK_ROPE_DIM]
    cT = cosT_ref[...]
    sT = sinT_ref[...]
    r1 = (x1 * cT - x2 * sT).astype(BF16)
    r2 = (x2 * cT + x1 * sT).astype(BF16)
    for h in range(N_HEADS):
        kT_ref[0, h, 0, 0:QK_NOPE_DIM, :] = knT[h * QK_NOPE_DIM:(h + 1) * QK_NOPE_DIM].astype(BF16)
        kT_ref[0, h, 0, QK_NOPE_DIM:QK_NOPE_DIM + ROPE_HALF, :] = r1
        kT_ref[0, h, 0, QK_NOPE_DIM + ROPE_HALF:QK_DIM, :] = r2
    v_ref[0] = _dot(ckvn.astype(BF16), wuv_ref[...]).astype(BF16)

    ub = _dot(hb, win_ref[:, _UF0:_UF1]).astype(BF16)
    a_ref[...] = _dot(ub, cc_ref[...]).astype(BF16)
    b_ref[...] = _dot(ub, sc_ref[...]).astype(BF16)

    gl = _dot(hb, win_ref[:, _G0:_G1])
    gate_ref[0] = (1.0 / (1.0 + jnp.exp(-gl))).astype(BF16)


def _in_proj(x, w, rope, ts):
    B, S, _ = x.shape
    nt = S // ts
    const = lambda shape: pl.BlockSpec(shape, lambda b, t: (0,) * len(shape))
    out_shape = (
        jax.ShapeDtypeStruct((B, N_HEADS, S, QK_DIM), BF16),
        jax.ShapeDtypeStruct((B, N_HEADS, nt, QK_DIM, ts), BF16),
        jax.ShapeDtypeStruct((B, S, N_HEADS * V_HEAD_DIM), BF16),
        jax.ShapeDtypeStruct((S, B * FNET_DIM), BF16),
        jax.ShapeDtypeStruct((S, B * FNET_DIM), BF16),
        jax.ShapeDtypeStruct((B, S, 2 * D_MODEL), BF16),
    )
    in_specs = [
        pl.BlockSpec((1, ts, D_MODEL), lambda b, t: (b, t, 0)),
        const((1, D_MODEL)),
        const((D_MODEL, IN_DIM_PADDED)),
        const((1, Q_LORA_RANK)),
        const((Q_LORA_RANK, N_HEADS * 2 * LANES)),
        const((1, KV_LORA_RANK)),
        const((N_HEADS * QK_NOPE_DIM, KV_LORA_RANK)),
        const((KV_LORA_RANK, N_HEADS * V_HEAD_DIM)),
        pl.BlockSpec((ts, LANES), lambda b, t: (t, 0)),
        pl.BlockSpec((ts, LANES), lambda b, t: (t, 0)),
        pl.BlockSpec((ROPE_HALF, ts), lambda b, t: (0, t)),
        pl.BlockSpec((ROPE_HALF, ts), lambda b, t: (0, t)),
        const((FNET_DIM, FNET_DIM)),
        const((FNET_DIM, FNET_DIM)),
    ]
    out_specs = (
        pl.BlockSpec((1, N_HEADS, ts, QK_DIM), lambda b, t: (b, 0, t, 0)),
        pl.BlockSpec((1, N_HEADS, 1, QK_DIM, ts), lambda b, t: (b, 0, t, 0, 0)),
        pl.BlockSpec((1, ts, N_HEADS * V_HEAD_DIM), lambda b, t: (b, t, 0)),
        pl.BlockSpec((ts, FNET_DIM), lambda b, t: (t, b)),
        pl.BlockSpec((ts, FNET_DIM), lambda b, t: (t, b)),
        pl.BlockSpec((1, ts, 2 * D_MODEL), lambda b, t: (b, t, 0)),
    )
    return pl.pallas_call(
        _in_proj_body, grid=(B, nt), in_specs=in_specs, out_specs=out_specs, out_shape=out_shape,
        compiler_params=_params(("parallel", "parallel")), name="in_proj",
    )(x, w["norm_attn"], w["w_in"], w["g_q"], w["w_uq"], w["g_kv"], w["w_ukT"], w["w_uv"],
      rope["cos_nat"], rope["sin_nat"], rope["cosT"], rope["sinT"], w["cc"], w["sc"])


def _attn_body(q_ref, kT_ref, v_ref, kmT_ref, vm_ref, o_ref, s_ref, p_ref, acc_ref, *, nk, tk):
    q = q_ref[0, 0]

    def scores(j, slot):
        s_ref[slot] = _dot(q, kT_ref[0, 0, j])

    def values(j, slot, alpha):
        v = v_ref[0, pl.ds(pl.multiple_of(j * tk, tk), tk), :]
        acc_ref[...] = alpha * acc_ref[...] + _dot(p_ref[slot], v)

    def softmax(slot, m, l):
        s = s_ref[slot]
        m_new = jnp.maximum(m, jnp.max(s, axis=1, keepdims=True))
        alpha = jnp.exp2(m - m_new)
        p = jnp.exp2(s - m_new)
        p_ref[slot] = p.astype(BF16)
        return m_new, alpha * l + jnp.sum(p, axis=1, keepdims=True), alpha

    s = _dot(q, kmT_ref[0])
    col = lax.broadcasted_iota(jnp.int32, s.shape, 1)
    s = jnp.where(col < N_META, s, NEG_BIG)
    m = jnp.max(s, axis=1, keepdims=True)
    p = jnp.exp2(s - m)
    l = jnp.sum(p, axis=1, keepdims=True)
    acc_ref[...] = _dot(p.astype(BF16), vm_ref[0])

    p_ref[1] = jnp.zeros(p_ref.shape[1:], BF16)
    scores(0, 0)

    def pair(jj, carry, last):
        m, l, alpha = carry
        a = 2 * jj
        values(jnp.maximum(a - 1, 0), 1, alpha)
        scores(a + 1, 1)
        m, l, alpha = softmax(0, m, l)
        values(a, 0, alpha)
        if not last:
            scores(a + 2, 0)
        return softmax(1, m, l)

    carry = (m, l, jnp.ones_like(m))
    carry = lax.fori_loop(0, nk // 2 - 1, functools.partial(pair, last=False), carry)
    m, l, alpha = pair(nk // 2 - 1, carry, last=True)
    values(nk - 1, 1, alpha)
    o_ref[0] = (acc_ref[...] / l).astype(BF16)


def _attention(q, kT, v, kT_meta, v_meta, tq):
    B, H, S, _ = q.shape
    nk, tk = kT.shape[2], kT.shape[4]
    assert nk % 2 == 0, "the key loop is unrolled in pairs"
    return pl.pallas_call(
        functools.partial(_attn_body, nk=nk, tk=tk),
        grid=(B, H, S // tq),
        in_specs=[
            pl.BlockSpec((1, 1, tq, QK_DIM), lambda b, h, i: (b, h, i, 0)),
            pl.BlockSpec((1, 1, nk, QK_DIM, tk), lambda b, h, i: (b, h, 0, 0, 0)),
            pl.BlockSpec((1, S, V_HEAD_DIM), lambda b, h, i: (b, 0, h)),
            pl.BlockSpec((1, QK_DIM, META_PAD), lambda b, h, i: (h, 0, 0)),
            pl.BlockSpec((1, META_PAD, V_HEAD_DIM), lambda b, h, i: (h, 0, 0)),
        ],
        out_specs=pl.BlockSpec((1, tq, V_HEAD_DIM), lambda b, h, i: (b, i, h)),
        out_shape=jax.ShapeDtypeStruct((B, S, H * V_HEAD_DIM), BF16),
        scratch_shapes=[
            pltpu.VMEM((2, tq, tk), F32), pltpu.VMEM((2, tq, tk), BF16), pltpu.VMEM((tq, V_HEAD_DIM), F32),
        ],
        compiler_params=_params(("parallel", "parallel", "parallel")), name="attention",
    )(q, kT, v, kT_meta, v_meta)


def _fnet_body(a_ref, b_ref, er_ref, ei_ref, pr_ref, pi_ref, wm_ref, abm_ref, o_ref, acc_ref, *, reps):
    ci = pl.program_id(2)

    @pl.when(ci == 0)
    def _():
        mt = _dot(wm_ref[...], abm_ref[...])
        acc_ref[...] = jnp.concatenate([mt] * reps, axis=1)

    lane = lax.broadcasted_iota(jnp.int32, pr_ref.shape, 1)
    sel = lane == ci
    pr = jnp.sum(jnp.where(sel, pr_ref[...], 0.0), axis=1, keepdims=True)
    pi = jnp.sum(jnp.where(sel, pi_ref[...], 0.0), axis=1, keepdims=True)
    er = er_ref[...]
    ei = ei_ref[...]
    wr = (pr * er - pi * ei).astype(BF16)
    wi = (pr * ei + pi * er).astype(BF16)
    acc_ref[...] += _dot(wr, a_ref[...]) + _dot(wi, b_ref[...])

    @pl.when(ci == pl.num_programs(2) - 1)
    def _():
        o_ref[...] = acc_ref[...].astype(BF16)


def _fnet(a, b, tabs, abm, tm, tkf, tn):
    S, NB = a.shape
    rows = tabs["er"].shape[0]
    nkt = S // tkf
    return pl.pallas_call(
        functools.partial(_fnet_body, reps=tn // FNET_DIM),
        grid=(NB // tn, rows // tm, nkt),
        in_specs=[
            pl.BlockSpec((tkf, tn), lambda n, r, c: (c, n)),
            pl.BlockSpec((tkf, tn), lambda n, r, c: (c, n)),
            pl.BlockSpec((tm, tkf), lambda n, r, c: (r, 0)),
            pl.BlockSpec((tm, tkf), lambda n, r, c: (r, 0)),
            pl.BlockSpec((tm, nkt), lambda n, r, c: (r, 0)),
            pl.BlockSpec((tm, nkt), lambda n, r, c: (r, 0)),
            pl.BlockSpec((tm, 2 * N_META), lambda n, r, c: (r, 0)),
            pl.BlockSpec((2 * N_META, FNET_DIM), lambda n, r, c: (0, 0)),
        ],
        out_specs=pl.BlockSpec((tm, tn), lambda n, r, c: (r, n)),
        out_shape=jax.ShapeDtypeStruct((rows, NB), BF16),
        scratch_shapes=[pltpu.VMEM((tm, tn), F32)],
        compiler_params=_params(("parallel", "parallel", "arbitrary")), name="fnet",
    )(a, b, tabs["er"], tabs["ei"], tabs["pr"], tabs["pi"], tabs["wm"], abm)


def _fnet_mirror_body(dk_ref, dkm_ref, x0_ref, j1_ref, j2_ref, m_ref):
    k = pl.num_programs(1) - 1 - pl.program_id(1)
    lower = jnp.where(k == 0, x0_ref[...], dkm_ref[...])
    m_ref[...] = (_dot(j1_ref[...], dk_ref[...]) + _dot(j2_ref[...], lower)).astype(BF16)


def _fnet_mirror(d, x0, tr, tn):
    half, NB = d.shape
    nt = half // tr
    a = jnp.arange(tr, dtype=jnp.int32)[:, None]
    c = jnp.arange(tr, dtype=jnp.int32)[None, :]
    j1 = (c == tr - N_META - a).astype(BF16)
    j2 = (c == 2 * tr - N_META - a).astype(BF16)
    x0_tile = jnp.zeros((tr, NB), BF16).at[tr - N_META:].set(x0)
    return pl.pallas_call(
        _fnet_mirror_body, grid=(NB // tn, nt),
        in_specs=[
            pl.BlockSpec((tr, tn), lambda n, i: (nt - 1 - i, n)),
            pl.BlockSpec((tr, tn), lambda n, i: (jnp.maximum(nt - 2 - i, 0), n)),
            pl.BlockSpec((tr, tn), lambda n, i: (0, n)),
            pl.BlockSpec((tr, tr), lambda n, i: (0, 0)),
            pl.BlockSpec((tr, tr), lambda n, i: (0, 0)),
        ],
        out_specs=pl.BlockSpec((tr, tn), lambda n, i: (i, n)),
        out_shape=jax.ShapeDtypeStruct((half, NB), BF16),
        compiler_params=_params(("parallel", "parallel")), name="fnet_mirror",
    )(d, d, x0_tile, j1, j2)


def _post_mix_body(x_ref, att_ref, flo_ref, fhi_ref, gate_ref, wo_ref, wfo_ref, wout_ref, nm_ref,
                   rhi_ref, rlo_ref, rb_ref, h1_ref, route_ref):
    a = _dot(att_ref[0], wo_ref[...])
    first_half = pl.program_id(1) < pl.num_programs(1) // 2
    f = jnp.where(first_half, flo_ref[...], fhi_ref[...])
    bb = _dot(f, wfo_ref[0])
    g = gate_ref[0].astype(F32)
    merged = g[:, :D_MODEL] * a + g[:, D_MODEL:] * bb
    h1 = x_ref[0] + _dot(merged.astype(BF16), wout_ref[...])
    h1_ref[0] = h1
    xn = _rms(h1, nm_ref[...])
    xh = xn.astype(BF16)

    xl = (xn - xh.astype(F32)).astype(BF16)
    logits = _dot(xh, rhi_ref[...]) + (_dot(xh, rlo_ref[...]) + _dot(xl, rhi_ref[...])) + rb_ref[...]
    lane = lax.broadcasted_iota(jnp.int32, logits.shape, 1)
    lanef = lane.astype(F32)
    far = float(2 * LANES)

    gl = jnp.where((lane >= N_EXPERTS) & (lane < N_EXPERTS + N_EXPERT_GROUPS), logits, NEG_BIG)
    gmax = jnp.max(gl, axis=1, keepdims=True)
    g_w = 1.0 / jnp.sum(jnp.exp(gl - gmax), axis=1, keepdims=True)
    gidx = jnp.min(jnp.where(gl == gmax, lanef, far), axis=1, keepdims=True) - float(N_EXPERTS)

    grp = (lane >> 3).astype(F32)
    el = jnp.where((lane < N_EXPERTS) & (grp == gidx), logits, NEG_BIG)
    m1 = jnp.max(el, axis=1, keepdims=True)
    i1 = jnp.min(jnp.where(el == m1, lanef, far), axis=1, keepdims=True)
    el2 = jnp.where(lanef == i1, NEG_BIG, el)
    m2 = jnp.max(el2, axis=1, keepdims=True)
    i2 = jnp.min(jnp.where(el2 == m2, lanef, far), axis=1, keepdims=True)
    r = jnp.exp(m2 - m1)
    w1 = g_w / (1.0 + r)
    w2 = g_w * r / (1.0 + r)

    first_is_lo = i1 < i2
    a_loc = jnp.minimum(i1, i2) - gidx * EXPERTS_PER_GROUP
    b_loc = jnp.maximum(i1, i2) - gidx * EXPERTS_PER_GROUP
    pair = a_loc * (2 * EXPERTS_PER_GROUP - 1 - a_loc) * 0.5 + (b_loc - a_loc - 1.0)
    bucket = gidx * PAIRS_PER_GROUP + pair
    w_lo = jnp.where(first_is_lo, w1, w2)
    w_hi = jnp.where(first_is_lo, w2, w1)
    route_ref[0] = jnp.where(lane == 0, bucket, jnp.where(lane == 1, w_lo, jnp.where(lane == 2, w_hi, 0.0)))


def _post_mix(x, att, f_lo, f_hi, gates, w, ts):
    B, S, _ = x.shape
    nh = S // ts // 2
    const = lambda shape: pl.BlockSpec(shape, lambda b, t: (0,) * len(shape))
    tok = lambda width: pl.BlockSpec((1, ts, width), lambda b, t: (b, t, 0))
    return pl.pallas_call(
        _post_mix_body, grid=(B, S // ts),
        in_specs=[
            tok(D_MODEL), tok(D_MODEL),
            pl.BlockSpec((ts, FNET_DIM), lambda b, t: (jnp.minimum(t, nh - 1), b)),
            pl.BlockSpec((ts, FNET_DIM), lambda b, t: (jnp.maximum(t - nh, 0), b)),
            tok(2 * D_MODEL),
            const((D_MODEL, D_MODEL)),
            pl.BlockSpec((1, FNET_DIM, D_MODEL), lambda b, t: (jnp.where(t < nh, 0, 1), 0, 0)),
            const((D_MODEL, D_MODEL)),
            const((1, D_MODEL)), const((D_MODEL, LANES)), const((D_MODEL, LANES)), const((1, LANES)),
        ],
        out_specs=(tok(D_MODEL), tok(LANES)),
        out_shape=(
            jax.ShapeDtypeStruct((B, S, D_MODEL), F32),
            jax.ShapeDtypeStruct((B, S, LANES), F32),
        ),
        compiler_params=_params(("parallel", "parallel")), name="post_mix",
    )(x, att, f_lo, f_hi, gates, w["w_o"], w["w_fo"], w["w_out"], w["norm_moe"], w["r_hi"], w["r_lo"], w["r_b"])


def _moe_body(lo_ref, hi_ref, nv_ref, src_ref, nxt_ref, rt_ref, h1_hbm,
              wgl_ref, wul_ref, wdl_ref, wgh_ref, wuh_ref, wdh_ref, nm_ref, nf_ref,
              out_hbm, xbuf, ybuf, gsem, ssem, *, tb):
    i = pl.program_id(0)
    slot = lax.rem(i, 2)
    other = 1 - slot
    cnt = nv_ref[i]
    prev = jnp.where(i > 0, nv_ref[jnp.maximum(i - 1, 0)], 0)

    def gather(idx_ref, s, r):
        return pltpu.make_async_copy(h1_hbm.at[idx_ref[0, 0, r]], xbuf.at[s, r], gsem.at[s])

    def scatter_wait(s, rows):
        full = pl.multiple_of(lax.shift_left(lax.shift_right_logical(rows, 3), 3), 8)

        @pl.when(full > 0)
        def _():
            pltpu.make_async_copy(ybuf.at[s, pl.ds(0, full)], out_hbm.at[pl.ds(0, full)], ssem.at[s]).wait()

        def one(r, c):
            pltpu.make_async_copy(ybuf.at[s, r], out_hbm.at[0], ssem.at[s]).wait()
            return c

        lax.fori_loop(full, rows, one, 0)

    @pl.when(i == 0)
    def _():
        for r in range(tb):
            gather(src_ref, 0, r).start()

    @pl.when(cnt > 0)
    def _():
        for r in range(tb):
            gather(src_ref, slot, r).wait()
        for r in range(tb):
            gather(nxt_ref, other, r).start()
        x = xbuf[slot]
        xb = _rms(x, nm_ref[...]).astype(BF16)
        rt = rt_ref[0]

        def expert(wg_ref, wu_ref, wd_ref, wcol):
            gt = _dot(xb, wg_ref[0])
            up = _dot(xb, wu_ref[0])
            act = (gt / (1.0 + jnp.exp(-gt))) * up * wcol
            return _dot(act.astype(BF16), wd_ref[0])

        y = expert(wgl_ref, wul_ref, wdl_ref, rt[:, 0:1]) + expert(wgh_ref, wuh_ref, wdh_ref, rt[:, 1:2])
        ybuf[slot] = _rms(x + y, nf_ref[...])

        scatter_wait(other, prev)

        def issue(r):
            pltpu.make_async_copy(ybuf.at[slot, r], out_hbm.at[src_ref[0, 0, r]], ssem.at[slot]).start()

        def issue8(g, c):
            for u in range(8):
                issue(g * 8 + u)
            return c

        def issue1(r, c):
            issue(r)
            return c

        groups = lax.shift_right_logical(cnt, 3)
        lax.fori_loop(0, groups, issue8, 0)
        lax.fori_loop(groups * 8, cnt, issue1, 0)

    @pl.when((cnt == 0) & (prev > 0))
    def _():
        for r in range(tb):
            gather(src_ref, slot, r).wait()
        scatter_wait(other, prev)


def _moe(h1, route, w, tb):
    N = h1.shape[0]
    t = _route_tables(route, tb)
    nblk = t["src"].shape[0]
    idx_spec = lambda f: pl.BlockSpec((1, 1, tb), f, memory_space=pltpu.SMEM)
    wspec = lambda shape, which: pl.BlockSpec(
        (1,) + shape, (lambda i, lo, hi, nv: (lo[i], 0, 0)) if which == 0 else (lambda i, lo, hi, nv: (hi[i], 0, 0)))
    up_shape, down_shape = (D_MODEL, D_FF_EXPERT), (D_FF_EXPERT, D_MODEL)
    grid_spec = pltpu.PrefetchScalarGridSpec(
        num_scalar_prefetch=3, grid=(nblk,),
        in_specs=[
            idx_spec(lambda i, lo, hi, nv: (i, 0, 0)),
            idx_spec(lambda i, lo, hi, nv: (jnp.minimum(i + 1, nblk - 1), 0, 0)),
            pl.BlockSpec((1, tb, 2), lambda i, lo, hi, nv: (i, 0, 0)),
            pl.BlockSpec(memory_space=pl.ANY),
            wspec(up_shape, 0), wspec(up_shape, 0), wspec(down_shape, 0),
            wspec(up_shape, 1), wspec(up_shape, 1), wspec(down_shape, 1),
            pl.BlockSpec((1, D_MODEL), lambda i, lo, hi, nv: (0, 0)),
            pl.BlockSpec((1, D_MODEL), lambda i, lo, hi, nv: (0, 0)),
        ],
        out_specs=pl.BlockSpec(memory_space=pl.ANY),
        scratch_shapes=[
            pltpu.VMEM((2, tb, D_MODEL), F32), pltpu.VMEM((2, tb, D_MODEL), F32),
            pltpu.SemaphoreType.DMA((2,)), pltpu.SemaphoreType.DMA((2,)),
        ],
    )
    return pl.pallas_call(
        functools.partial(_moe_body, tb=tb), grid_spec=grid_spec,
        out_shape=jax.ShapeDtypeStruct((N, D_MODEL), F32),
        compiler_params=_params(("arbitrary",)), name="moe",
    )(t["lo"], t["hi"], t["nvalid"], t["src"], t["src"], t["rt"], h1,
      w["w_eg"], w["w_eu"], w["w_ed"], w["w_eg"], w["w_eu"], w["w_ed"], w["norm_moe"], w["norm_final"])


def _route_tables(route, tb):
    N = route.shape[0]
    bucket = route[:, 0].astype(jnp.int32)
    order = jnp.argsort(bucket).astype(jnp.int32)
    ids = jnp.arange(N_BUCKETS, dtype=jnp.int32)
    counts = jnp.sum((bucket[:, None] == ids[None, :]).astype(jnp.int32), axis=0)
    starts = jnp.cumsum(counts) - counts
    padded = ((counts + tb - 1) // tb) * tb
    pend = jnp.cumsum(padded)
    poff = pend - padded
    nblk = N // tb + N_BUCKETS + 1
    blk_start = jnp.arange(nblk, dtype=jnp.int32) * tb
    active = blk_start < pend[-1]
    bb = jnp.minimum(jnp.sum((pend[None, :] <= blk_start[:, None]).astype(jnp.int32), axis=1), N_BUCKETS - 1)
    r0 = blk_start - poff[bb]
    nvalid = jnp.where(active, jnp.clip(counts[bb] - r0, 0, tb), 0).astype(jnp.int32)
    bb_w = jnp.where(active, bb, bb[jnp.maximum(jnp.sum(active.astype(jnp.int32)) - 1, 0)])
    rows = starts[bb][:, None] + r0[:, None] + jnp.arange(tb, dtype=jnp.int32)[None, :]
    src = order[jnp.clip(rows, 0, N - 1)]
    return {
        "lo": jnp.asarray(_BUCKET_LO)[bb_w], "hi": jnp.asarray(_BUCKET_HI)[bb_w], "nvalid": nvalid,
        "src": src[:, None, :], "rt": route[:, 1:3][src],
    }


def _rope_tables(pos):
    inv_freq = 1.0 / (ROPE_THETA ** (jnp.arange(ROPE_HALF, dtype=F32) / ROPE_HALF))
    ang = pos.astype(F32)[:, None] * inv_freq[None, :]
    c, s = jnp.cos(ang), jnp.sin(ang)
    z = jnp.zeros((pos.shape[0], LANES - QK_ROPE_DIM), F32)
    return {
        "cos_nat": jnp.concatenate([c, c, z], axis=1), "sin_nat": jnp.concatenate([s, s, z], axis=1),
        "cosT": c.T, "sinT": s.T,
    }


def _unit(k, n):
    ang = k.astype(F32) * (-2.0 * math.pi / n)
    return jnp.cos(ang), jnp.sin(ang)


def _fnet_tables(S, tkf, freqs):
    L = S + N_META
    j = freqs.astype(jnp.int32)[:, None]
    er, ei = _unit((j * jnp.arange(tkf, dtype=jnp.int32)[None, :]) % L, L)
    col0 = jnp.arange(S // tkf, dtype=jnp.int32)[None, :] * tkf + N_META
    pr, pi = _unit((j * col0) % L, L)
    mr, mi = _unit((j * jnp.arange(N_META, dtype=jnp.int32)[None, :]) % L, L)
    return {"er": er, "ei": ei, "pr": pr, "pi": pi, "wm": jnp.concatenate([mr, mi], axis=1).astype(BF16)}


def _channel_dft(L):
    c = jnp.arange(FNET_GROUP_DIM, dtype=jnp.int32)
    cr, ci = _unit((c[:, None] * c[None, :]) % FNET_GROUP_DIM, FNET_GROUP_DIM)
    norm = 1.0 / math.sqrt(L * FNET_GROUP_DIM)
    eye = jnp.eye(N_FNET_GROUPS, dtype=F32)
    return (jnp.kron(eye, cr * norm).astype(BF16), jnp.kron(eye, -ci * norm).astype(BF16))


def _prep_weights(norm_attn, w_in, g_q, w_uq, g_kv, w_uk, w_uv, w_o_mla, w_fo, w_out, norm_moe,
                  w_rg, b_rg, w_re, b_re, w_eg, w_eu, w_ed, norm_final):
    s1 = Q_LORA_RANK
    s2 = s1 + KV_LORA_RANK
    s3 = s2 + QK_ROPE_DIM
    s4 = s3 + FNET_DIM
    w_in_p = jnp.concatenate(
        [w_in[:, :s1], w_in[:, s1:s2], w_in[:, s3:s4], w_in[:, s4:], w_in[:, s2:s3],
         jnp.zeros((D_MODEL, LANES - QK_ROPE_DIM), w_in.dtype)], axis=1)
    w_uq_p = jnp.pad(w_uq, ((0, 0), (0, 0), (0, 2 * LANES - QK_DIM)))
    r = jnp.concatenate([w_re, w_rg, jnp.zeros((D_MODEL, LANES - N_EXPERTS - N_EXPERT_GROUPS), F32)], axis=1)
    r_hi = r.astype(BF16)
    r_b = jnp.concatenate([b_re, b_rg, jnp.zeros((LANES - N_EXPERTS - N_EXPERT_GROUPS,), F32)])[None, :]
    return {
        "norm_attn": norm_attn[None, :], "w_in": w_in_p.astype(BF16),
        "g_q": g_q[None, :], "w_uq": w_uq_p.reshape(Q_LORA_RANK, N_HEADS * 2 * LANES).astype(BF16),
        "g_kv": g_kv[None, :], "w_ukT": w_uk.reshape(KV_LORA_RANK, N_HEADS * QK_NOPE_DIM).T.astype(BF16),
        "w_uv": w_uv.reshape(KV_LORA_RANK, N_HEADS * V_HEAD_DIM).astype(BF16),
        "w_o": w_o_mla.astype(BF16), "w_out": w_out.astype(BF16),
        "w_fo": jnp.stack([w_fo, w_fo[_CHANNEL_FLIP]]).astype(BF16),
        "norm_moe": norm_moe[None, :], "r_hi": r_hi, "r_lo": (r - r_hi.astype(F32)).astype(BF16), "r_b": r_b,
        "w_eg": w_eg.reshape(N_EXPERTS, D_MODEL, D_FF_EXPERT).astype(BF16),
        "w_eu": w_eu.reshape(N_EXPERTS, D_MODEL, D_FF_EXPERT).astype(BF16),
        "w_ed": w_ed.reshape(N_EXPERTS, D_FF_EXPERT, D_MODEL).astype(BF16),
        "norm_final": norm_final[None, :],
    }


def _tiles(B, S):
    ts = min(512, S // 2)
    return {
        "ts": ts, "tq": min(2048, S), "tm": min(1024, S // 2), "tkf": min(512, S),
        "tn": min(2048, B * FNET_DIM), "tb": 128,
    }


def _trunk(x, meta, w):
    B, S, _ = x.shape
    L = S + N_META
    t = _tiles(B, S)
    wl = dict(w)
    wl["cc"], wl["sc"] = _channel_dft(L)

    meta_pad = jnp.zeros((1, META_PAD, D_MODEL), F32).at[0, :N_META].set(meta)
    _, kT_m, v_m, a_m, b_m, _ = _in_proj(meta_pad, wl, _rope_tables(jnp.arange(META_PAD)), META_PAD)
    q, kT, v, a, b, gates = _in_proj(x, wl, _rope_tables(jnp.arange(S) + N_META), t["ts"])

    v_meta = v_m[0].reshape(META_PAD, N_HEADS, V_HEAD_DIM).transpose(1, 0, 2)
    att = _attention(q, kT, v, kT_m[0, :, 0], v_meta, t["tq"])
    abm = jnp.concatenate([a_m[:N_META], b_m[:N_META]], axis=0)
    half = S // 2
    f_lo = _fnet(a, b, _fnet_tables(S, t["tkf"], jnp.arange(half) + N_META), abm, t["tm"], t["tkf"], t["tn"])
    f_meta = _fnet(a, b, _fnet_tables(S, t["tkf"], jnp.arange(N_META)), abm, N_META, t["tkf"], t["tn"])
    f_hi = _fnet_mirror(f_lo, f_meta, min(512, half), t["tn"])
    h1, route = _post_mix(x, att, f_lo, f_hi, gates, wl, t["ts"])
    y = _moe(h1.reshape(B * S, D_MODEL), route.reshape(B * S, LANES), wl, t["tb"])
    return y.reshape(B, S, D_MODEL)


def kernel(x_prompt, x_sample, meta_tokens, norm_attn, w_in, g_q, w_uq, g_kv, w_uk, w_uv, w_o_mla, w_fo, w_out,
           norm_moe, w_router_group, b_router_group, w_router_expert, b_router_expert, w_exp_gate, w_exp_up,
           w_exp_down, norm_final):
    w = _prep_weights(norm_attn[0], w_in[0], g_q[0], w_uq[0], g_kv[0], w_uk[0], w_uv[0], w_o_mla[0], w_fo[0],
                      w_out[0], norm_moe[0], w_router_group[0], b_router_group[0], w_router_expert[0],
                      b_router_expert[0], w_exp_gate[0], w_exp_up[0], w_exp_down[0], norm_final)
    return _trunk(x_prompt, meta_tokens, w), _trunk(x_sample, meta_tokens, w)
```

```python
import functools
import math

import jax
import jax.numpy as jnp
import numpy as np
from jax import lax
from jax.experimental import pallas as pl
from jax.experimental.pallas import tpu as pltpu

D_MODEL = 1024
N_META = 16
N_HEADS = 8
QK_NOPE_DIM = 128
QK_ROPE_DIM = 64
ROPE_HALF = QK_ROPE_DIM // 2
QK_DIM = QK_NOPE_DIM + QK_ROPE_DIM
V_HEAD_DIM = 128
Q_LORA_RANK = 384
KV_LORA_RANK = 256
ROPE_THETA = 10000.0
N_FNET_GROUPS = 4
FNET_GROUP_DIM = 128
FNET_DIM = N_FNET_GROUPS * FNET_GROUP_DIM
N_EXPERT_GROUPS = 4
EXPERTS_PER_GROUP = 8
N_EXPERTS = N_EXPERT_GROUPS * EXPERTS_PER_GROUP
D_FF_EXPERT = 256
EPS = 1e-6
PAIRS_PER_GROUP = EXPERTS_PER_GROUP * (EXPERTS_PER_GROUP - 1) // 2
N_BUCKETS = N_EXPERT_GROUPS * PAIRS_PER_GROUP
_CHANNEL_FLIP = np.array([g * FNET_GROUP_DIM + (FNET_GROUP_DIM - c) % FNET_GROUP_DIM
                          for g in range(N_FNET_GROUPS) for c in range(FNET_GROUP_DIM)], np.int32)
_PAIRS = [(a, b) for a in range(EXPERTS_PER_GROUP) for b in range(a + 1, EXPERTS_PER_GROUP)]
_BUCKET_LO = np.array([g * EXPERTS_PER_GROUP + a for g in range(N_EXPERT_GROUPS) for a, _ in _PAIRS], np.int32)
_BUCKET_HI = np.array([g * EXPERTS_PER_GROUP + b for g in range(N_EXPERT_GROUPS) for _, b in _PAIRS], np.int32)

LANES = 128
META_PAD = 128
VMEM_LIMIT_BYTES = 56 * 1024 * 1024
NEG_BIG = -1e30
FAST_ROW_SUM_LIMIT = 2.0 ** 24

_CQ0, _CQ1 = 0, Q_LORA_RANK
_CKV0, _CKV1 = _CQ1, _CQ1 + KV_LORA_RANK
_UF0, _UF1 = _CKV1, _CKV1 + FNET_DIM
_G0, _G1 = _UF1, _UF1 + 2 * D_MODEL
_KR0, _KR1 = _G1, _G1 + LANES
IN_DIM_PADDED = _KR1

F32 = jnp.float32
BF16 = jnp.bfloat16


def _dot(a, b):
    return jnp.dot(a, b, preferred_element_type=F32)


def _rms(x, g):
    ms = jnp.mean(x * x, axis=-1, keepdims=True)
    return x * lax.rsqrt(ms + EPS) * g


def _params(semantics):
    return pltpu.CompilerParams(dimension_semantics=semantics, vmem_limit_bytes=VMEM_LIMIT_BYTES)


def _in_proj_body(x_ref, na_ref, win_ref, gq_ref, wuq_ref, gkv_ref, wukT_ref, wuv_ref,
                  cosn_ref, sinn_ref, cosT_ref, sinT_ref, cc_ref, sc_ref,
                  q_ref, kT_ref, v_ref, a_ref, b_ref, gate_ref):
    x = x_ref[0]
    hb = _rms(x, na_ref[...]).astype(BF16)

    cqn = _rms(_dot(hb, win_ref[:, _CQ0:_CQ1]), gq_ref[...])
    qa = _dot(cqn.astype(BF16), wuq_ref[...])
    cn = cosn_ref[...]
    sn = sinn_ref[...]
    first_half = lax.broadcasted_iota(jnp.int32, cn.shape, 1) < ROPE_HALF
    scale = math.log2(math.e) / math.sqrt(QK_DIM)
    for h in range(N_HEADS):
        c0 = h * 2 * LANES
        qr = qa[:, c0 + LANES:c0 + 2 * LANES]
        rot = jnp.where(first_half, -pltpu.roll(qr, LANES - ROPE_HALF, 1), pltpu.roll(qr, ROPE_HALF, 1))
        q_ref[0, h, :, 0:QK_NOPE_DIM] = (qa[:, c0:c0 + LANES] * scale).astype(BF16)
        q_ref[0, h, :, QK_NOPE_DIM:QK_DIM] = ((qr * cn + rot * sn) * scale)[:, 0:QK_ROPE_DIM].astype(BF16)

    ckvn = _rms(_dot(hb, win_ref[:, _CKV0:_CKV1]), gkv_ref[...])
    knT = _dot(wukT_ref[...], ckvn.T.astype(BF16))
    krT = _dot(hb, win_ref[:, _KR0:_KR1]).T
    x1 = krT[0:ROPE_HALF]
    x2 = krT[ROPE_HALF:QK_ROPE_DIM]
    cT = cosT_ref[...]
    sT = sinT_ref[...]
    r1 = (x1 * cT - x2 * sT).astype(BF16)
    r2 = (x2 * cT + x1 * sT).astype(BF16)
    for h in range(N_HEADS):
        kT_ref[0, h, 0, 0:QK_NOPE_DIM, :] = knT[h * QK_NOPE_DIM:(h + 1) * QK_NOPE_DIM].astype(BF16)
        kT_ref[0, h, 0, QK_NOPE_DIM:QK_NOPE_DIM + ROPE_HALF, :] = r1
        kT_ref[0, h, 0, QK_NOPE_DIM + ROPE_HALF:QK_DIM, :] = r2
    v_ref[0] = _dot(ckvn.astype(BF16), wuv_ref[...]).astype(BF16)

    ub = _dot(hb, win_ref[:, _UF0:_UF1]).astype(BF16)
    a_ref[...] = _dot(ub, cc_ref[...]).astype(BF16)
    b_ref[...] = _dot(ub, sc_ref[...]).astype(BF16)

    gl = _dot(hb, win_ref[:, _G0:_G1])
    gate_ref[0] = (1.0 / (1.0 + jnp.exp(-gl))).astype(BF16)


def _in_proj(x, w, rope, ts):
    B, S, _ = x.shape
    nt = S // ts
    const = lambda shape: pl.BlockSpec(shape, lambda b, t: (0,) * len(shape))
    out_shape = (
        jax.ShapeDtypeStruct((B, N_HEADS, S, QK_DIM), BF16),
        jax.ShapeDtypeStruct((B, N_HEADS, nt, QK_DIM, ts), BF16),
        jax.ShapeDtypeStruct((B, S, N_HEADS * V_HEAD_DIM), BF16),
        jax.ShapeDtypeStruct((S, B * FNET_DIM), BF16),
        jax.ShapeDtypeStruct((S, B * FNET_DIM), BF16),
        jax.ShapeDtypeStruct((B, S, 2 * D_MODEL), BF16),
    )
    in_specs = [
        pl.BlockSpec((1, ts, D_MODEL), lambda b, t: (b, t, 0)),
        const((1, D_MODEL)),
        const((D_MODEL, IN_DIM_PADDED)),
        const((1, Q_LORA_RANK)),
        const((Q_LORA_RANK, N_HEADS * 2 * LANES)),
        const((1, KV_LORA_RANK)),
        const((N_HEADS * QK_NOPE_DIM, KV_LORA_RANK)),
        const((KV_LORA_RANK, N_HEADS * V_HEAD_DIM)),
        pl.BlockSpec((ts, LANES), lambda b, t: (t, 0)),
        pl.BlockSpec((ts, LANES), lambda b, t: (t, 0)),
        pl.BlockSpec((ROPE_HALF, ts), lambda b, t: (0, t)),
        pl.BlockSpec((ROPE_HALF, ts), lambda b, t: (0, t)),
        const((FNET_DIM, FNET_DIM)),
        const((FNET_DIM, FNET_DIM)),
    ]
    out_specs = (
        pl.BlockSpec((1, N_HEADS, ts, QK_DIM), lambda b, t: (b, 0, t, 0)),
        pl.BlockSpec((1, N_HEADS, 1, QK_DIM, ts), lambda b, t: (b, 0, t, 0, 0)),
        pl.BlockSpec((1, ts, N_HEADS * V_HEAD_DIM), lambda b, t: (b, t, 0)),
        pl.BlockSpec((ts, FNET_DIM), lambda b, t: (t, b)),
        pl.BlockSpec((ts, FNET_DIM), lambda b, t: (t, b)),
        pl.BlockSpec((1, ts, 2 * D_MODEL), lambda b, t: (b, t, 0)),
    )
    return pl.pallas_call(
        _in_proj_body, grid=(B, nt), in_specs=in_specs, out_specs=out_specs, out_shape=out_shape,
        compiler_params=_params(("parallel", "parallel")), name="in_proj",
    )(x, w["norm_attn"], w["w_in"], w["g_q"], w["w_uq"], w["g_kv"], w["w_ukT"], w["w_uv"],
      rope["cos_nat"], rope["sin_nat"], rope["cosT"], rope["sinT"], w["cc"], w["sc"])


def _attn_body(q_ref, kT_ref, v_ref, kmT_ref, vm_ref, o_ref, s_ref, p_ref, acc_ref, accx_ref, *, nk, tk):
    q = q_ref[0, 0]
    col = lax.broadcasted_iota(jnp.int32, (q.shape[0], META_PAD), 1)

    def meta_scores():
        return jnp.where(col < N_META, _dot(q, kmT_ref[0]), NEG_BIG)

    def v_chunk(j):
        return v_ref[0, pl.ds(pl.multiple_of(j * tk, tk), tk), :]

    def ones_column(rows):
        return jnp.where(lax.broadcasted_iota(jnp.int32, (rows, LANES), 1) == 0, 1.0, 0.0).astype(BF16)

    ones_col = ones_column(tk)
    s_meta = meta_scores()
    s_ref[0] = _dot(q, kT_ref[0, 0, 0])
    m0 = jnp.maximum(jnp.max(s_meta, axis=1, keepdims=True), jnp.max(s_ref[0], axis=1, keepdims=True))
    vmx = jnp.concatenate([vm_ref[0], ones_column(META_PAD)], axis=1)
    accx_ref[...] = _dot(jnp.exp2(s_meta - m0).astype(BF16), vmx)
    p_ref[0] = jnp.exp2(s_ref[0] - m0).astype(BF16)

    def probs(j, slot):
        p_ref[slot] = jnp.exp2(_dot(q, kT_ref[0, 0, j]) - m0).astype(BF16)

    def accumulate(j, slot):
        accx_ref[...] += _dot(p_ref[slot], jnp.concatenate([v_chunk(j), ones_col], axis=1))

    def fast_pair(jj, c):
        a = 2 * jj + 1
        accumulate(a - 1, 0)
        probs(a, 1)
        accumulate(a, 1)
        probs(a + 1, 0)
        return c

    lax.fori_loop(0, (nk - 2) // 2, fast_pair, 0)
    accumulate(nk - 2, 0)
    probs(nk - 1, 1)
    accumulate(nk - 1, 1)
    l_fast = accx_ref[:, V_HEAD_DIM:2 * LANES][:, 0:1]
    fast_ok = jnp.max(l_fast) < FAST_ROW_SUM_LIMIT

    @pl.when(fast_ok)
    def _():
        o_ref[0] = (accx_ref[:, 0:V_HEAD_DIM] / l_fast).astype(BF16)

    def scores(j, slot):
        s_ref[slot] = _dot(q, kT_ref[0, 0, j])

    def values(j, slot, alpha):
        acc_ref[...] = alpha * acc_ref[...] + _dot(p_ref[slot], v_chunk(j))

    def softmax(slot, m, l):
        s = s_ref[slot]
        m_new = jnp.maximum(m, jnp.max(s, axis=1, keepdims=True))
        alpha = jnp.exp2(m - m_new)
        p = jnp.exp2(s - m_new)
        p_ref[slot] = p.astype(BF16)
        return m_new, alpha * l + jnp.sum(p, axis=1, keepdims=True), alpha

    def pair(jj, carry, last):
        m, l, alpha = carry
        a = 2 * jj
        values(jnp.maximum(a - 1, 0), 1, alpha)
        scores(a + 1, 1)
        m, l, alpha = softmax(0, m, l)
        values(a, 0, alpha)
        if not last:
            scores(a + 2, 0)
        return softmax(1, m, l)

    @pl.when(jnp.logical_not(fast_ok))
    def _():
        s = meta_scores()
        m = jnp.max(s, axis=1, keepdims=True)
        p = jnp.exp2(s - m)
        l = jnp.sum(p, axis=1, keepdims=True)
        acc_ref[...] = _dot(p.astype(BF16), vm_ref[0])
        p_ref[1] = jnp.zeros(p_ref.shape[1:], BF16)
        scores(0, 0)
        carry = (m, l, jnp.ones_like(m))
        carry = lax.fori_loop(0, nk // 2 - 1, functools.partial(pair, last=False), carry)
        m, l, alpha = pair(nk // 2 - 1, carry, last=True)
        values(nk - 1, 1, alpha)
        o_ref[0] = (acc_ref[...] / l).astype(BF16)


def _attention(q, kT, v, kT_meta, v_meta, tq):
    B, H, S, _ = q.shape
    nk, tk = kT.shape[2], kT.shape[4]
    assert nk % 2 == 0, "the key loop is unrolled in pairs"
    return pl.pallas_call(
        functools.partial(_attn_body, nk=nk, tk=tk),
        grid=(B, H, S // tq),
        in_specs=[
            pl.BlockSpec((1, 1, tq, QK_DIM), lambda b, h, i: (b, h, i, 0)),
            pl.BlockSpec((1, 1, nk, QK_DIM, tk), lambda b, h, i: (b, h, 0, 0, 0)),
            pl.BlockSpec((1, S, V_HEAD_DIM), lambda b, h, i: (b, 0, h)),
            pl.BlockSpec((1, QK_DIM, META_PAD), lambda b, h, i: (h, 0, 0)),
            pl.BlockSpec((1, META_PAD, V_HEAD_DIM), lambda b, h, i: (h, 0, 0)),
        ],
        out_specs=pl.BlockSpec((1, tq, V_HEAD_DIM), lambda b, h, i: (b, i, h)),
        out_shape=jax.ShapeDtypeStruct((B, S, H * V_HEAD_DIM), BF16),
        scratch_shapes=[
            pltpu.VMEM((2, tq, tk), F32), pltpu.VMEM((2, tq, tk), BF16), pltpu.VMEM((tq, V_HEAD_DIM), F32),
            pltpu.VMEM((tq, 2 * LANES), F32),
        ],
        compiler_params=_params(("parallel", "parallel", "parallel")), name="attention",
    )(q, kT, v, kT_meta, v_meta)


def _fnet_body(a_ref, b_ref, er_ref, ei_ref, pr_ref, pi_ref, wm_ref, abm_ref, o_ref, acc_ref, *, reps):
    ci = pl.program_id(2)

    @pl.when(ci == 0)
    def _():
        mt = _dot(wm_ref[...], abm_ref[...])
        acc_ref[...] = jnp.concatenate([mt] * reps, axis=1)

    lane = lax.broadcasted_iota(jnp.int32, pr_ref.shape, 1)
    sel = lane == ci
    pr = jnp.sum(jnp.where(sel, pr_ref[...], 0.0), axis=1, keepdims=True)
    pi = jnp.sum(jnp.where(sel, pi_ref[...], 0.0), axis=1, keepdims=True)
    er = er_ref[...]
    ei = ei_ref[...]
    wr = (pr * er - pi * ei).astype(BF16)
    wi = (pr * ei + pi * er).astype(BF16)
    acc_ref[...] += _dot(wr, a_ref[...]) + _dot(wi, b_ref[...])

    @pl.when(ci == pl.num_programs(2) - 1)
    def _():
        o_ref[...] = acc_ref[...].astype(BF16)


def _fnet(a, b, tabs, abm, tm, tkf, tn):
    S, NB = a.shape
    rows = tabs["er"].shape[0]
    nkt = S // tkf
    return pl.pallas_call(
        functools.partial(_fnet_body, reps=tn // FNET_DIM),
        grid=(NB // tn, rows // tm, nkt),
        in_specs=[
            pl.BlockSpec((tkf, tn), lambda n, r, c: (c, n)),
            pl.BlockSpec((tkf, tn), lambda n, r, c: (c, n)),
            pl.BlockSpec((tm, tkf), lambda n, r, c: (r, 0)),
            pl.BlockSpec((tm, tkf), lambda n, r, c: (r, 0)),
            pl.BlockSpec((tm, nkt), lambda n, r, c: (r, 0)),
            pl.BlockSpec((tm, nkt), lambda n, r, c: (r, 0)),
            pl.BlockSpec((tm, 2 * N_META), lambda n, r, c: (r, 0)),
            pl.BlockSpec((2 * N_META, FNET_DIM), lambda n, r, c: (0, 0)),
        ],
        out_specs=pl.BlockSpec((tm, tn), lambda n, r, c: (r, n)),
        out_shape=jax.ShapeDtypeStruct((rows, NB), BF16),
        scratch_shapes=[pltpu.VMEM((tm, tn), F32)],
        compiler_params=_params(("parallel", "parallel", "arbitrary")), name="fnet",
    )(a, b, tabs["er"], tabs["ei"], tabs["pr"], tabs["pi"], tabs["wm"], abm)


def _fnet_mirror_body(dk_ref, dkm_ref, x0_ref, j1_ref, j2_ref, m_ref):
    k = pl.num_programs(1) - 1 - pl.program_id(1)
    lower = jnp.where(k == 0, x0_ref[...], dkm_ref[...])
    m_ref[...] = (_dot(j1_ref[...], dk_ref[...]) + _dot(j2_ref[...], lower)).astype(BF16)


def _fnet_mirror(d, x0, tr, tn):
    half, NB = d.shape
    nt = half // tr
    a = jnp.arange(tr, dtype=jnp.int32)[:, None]
    c = jnp.arange(tr, dtype=jnp.int32)[None, :]
    j1 = (c == tr - N_META - a).astype(BF16)
    j2 = (c == 2 * tr - N_META - a).astype(BF16)
    x0_tile = jnp.zeros((tr, NB), BF16).at[tr - N_META:].set(x0)
    return pl.pallas_call(
        _fnet_mirror_body, grid=(NB // tn, nt),
        in_specs=[
            pl.BlockSpec((tr, tn), lambda n, i: (nt - 1 - i, n)),
            pl.BlockSpec((tr, tn), lambda n, i: (jnp.maximum(nt - 2 - i, 0), n)),
            pl.BlockSpec((tr, tn), lambda n, i: (0, n)),
            pl.BlockSpec((tr, tr), lambda n, i: (0, 0)),
            pl.BlockSpec((tr, tr), lambda n, i: (0, 0)),
        ],
        out_specs=pl.BlockSpec((tr, tn), lambda n, i: (i, n)),
        out_shape=jax.ShapeDtypeStruct((half, NB), BF16),
        compiler_params=_params(("parallel", "parallel")), name="fnet_mirror",
    )(d, d, x0_tile, j1, j2)


def _post_mix_body(x_ref, att_ref, flo_ref, fhi_ref, gate_ref, wo_ref, wfo_ref, wout_ref, nm_ref,
                   rhi_ref, rlo_ref, rb_ref, h1_ref, route_ref):
    a = _dot(att_ref[0], wo_ref[...])
    first_half = pl.program_id(1) < pl.num_programs(1) // 2
    f = jnp.where(first_half, flo_ref[...], fhi_ref[...])
    bb = _dot(f, wfo_ref[0])
    g = gate_ref[0].astype(F32)
    merged = g[:, :D_MODEL] * a + g[:, D_MODEL:] * bb
    h1 = x_ref[0] + _dot(merged.astype(BF16), wout_ref[...])
    h1_ref[0] = h1
    xn = _rms(h1, nm_ref[...])
    xh = xn.astype(BF16)

    xl = (xn - xh.astype(F32)).astype(BF16)
    logits = _dot(xh, rhi_ref[...]) + (_dot(xh, rlo_ref[...]) + _dot(xl, rhi_ref[...])) + rb_ref[...]
    lane = lax.broadcasted_iota(jnp.int32, logits.shape, 1)
    lanef = lane.astype(F32)
    far = float(2 * LANES)

    gl = jnp.where((lane >= N_EXPERTS) & (lane < N_EXPERTS + N_EXPERT_GROUPS), logits, NEG_BIG)
    gmax = jnp.max(gl, axis=1, keepdims=True)
    g_w = 1.0 / jnp.sum(jnp.exp(gl - gmax), axis=1, keepdims=True)
    gidx = jnp.min(jnp.where(gl == gmax, lanef, far), axis=1, keepdims=True) - float(N_EXPERTS)

    grp = (lane >> 3).astype(F32)
    el = jnp.where((lane < N_EXPERTS) & (grp == gidx), logits, NEG_BIG)
    m1 = jnp.max(el, axis=1, keepdims=True)
    i1 = jnp.min(jnp.where(el == m1, lanef, far), axis=1, keepdims=True)
    el2 = jnp.where(lanef == i1, NEG_BIG, el)
    m2 = jnp.max(el2, axis=1, keepdims=True)
    i2 = jnp.min(jnp.where(el2 == m2, lanef, far), axis=1, keepdims=True)
    r = jnp.exp(m2 - m1)
    w1 = g_w / (1.0 + r)
    w2 = g_w * r / (1.0 + r)

    first_is_lo = i1 < i2
    a_loc = jnp.minimum(i1, i2) - gidx * EXPERTS_PER_GROUP
    b_loc = jnp.maximum(i1, i2) - gidx * EXPERTS_PER_GROUP
    pair = a_loc * (2 * EXPERTS_PER_GROUP - 1 - a_loc) * 0.5 + (b_loc - a_loc - 1.0)
    bucket = gidx * PAIRS_PER_GROUP + pair
    w_lo = jnp.where(first_is_lo, w1, w2)
    w_hi = jnp.where(first_is_lo, w2, w1)
    route_ref[0] = jnp.where(lane == 0, bucket, jnp.where(lane == 1, w_lo, jnp.where(lane == 2, w_hi, 0.0)))


def _post_mix(x, att, f_lo, f_hi, gates, w, ts):
    B, S, _ = x.shape
    nh = S // ts // 2
    const = lambda shape: pl.BlockSpec(shape, lambda b, t: (0,) * len(shape))
    tok = lambda width: pl.BlockSpec((1, ts, width), lambda b, t: (b, t, 0))
    return pl.pallas_call(
        _post_mix_body, grid=(B, S // ts),
        in_specs=[
            tok(D_MODEL), tok(D_MODEL),
            pl.BlockSpec((ts, FNET_DIM), lambda b, t: (jnp.minimum(t, nh - 1), b)),
            pl.BlockSpec((ts, FNET_DIM), lambda b, t: (jnp.maximum(t - nh, 0), b)),
            tok(2 * D_MODEL),
            const((D_MODEL, D_MODEL)),
            pl.BlockSpec((1, FNET_DIM, D_MODEL), lambda b, t: (jnp.where(t < nh, 0, 1), 0, 0)),
            const((D_MODEL, D_MODEL)),
            const((1, D_MODEL)), const((D_MODEL, LANES)), const((D_MODEL, LANES)), const((1, LANES)),
        ],
        out_specs=(tok(D_MODEL), tok(LANES)),
        out_shape=(
            jax.ShapeDtypeStruct((B, S, D_MODEL), F32),
            jax.ShapeDtypeStruct((B, S, LANES), F32),
        ),
        compiler_params=_params(("parallel", "parallel")), name="post_mix",
    )(x, att, f_lo, f_hi, gates, w["w_o"], w["w_fo"], w["w_out"], w["norm_moe"], w["r_hi"], w["r_lo"], w["r_b"])


def _moe_body(lo_ref, hi_ref, nv_ref, src_ref, nxt_ref, rt_ref, h1_hbm,
              wgl_ref, wul_ref, wdl_ref, wgh_ref, wuh_ref, wdh_ref, nm_ref, nf_ref,
              out_hbm, xbuf, ybuf, gsem, ssem, *, tb):
    i = pl.program_id(0)
    slot = lax.rem(i, 2)
    other = 1 - slot
    cnt = nv_ref[i]
    prev = jnp.where(i > 0, nv_ref[jnp.maximum(i - 1, 0)], 0)

    def gather(idx_ref, s, r):
        return pltpu.make_async_copy(h1_hbm.at[idx_ref[0, 0, r]], xbuf.at[s, r], gsem.at[s])

    def scatter_wait(s, rows):
        full = pl.multiple_of(lax.shift_left(lax.shift_right_logical(rows, 3), 3), 8)

        @pl.when(full > 0)
        def _():
            pltpu.make_async_copy(ybuf.at[s, pl.ds(0, full)], out_hbm.at[pl.ds(0, full)], ssem.at[s]).wait()

        def one(r, c):
            pltpu.make_async_copy(ybuf.at[s, r], out_hbm.at[0], ssem.at[s]).wait()
            return c

        lax.fori_loop(full, rows, one, 0)

    @pl.when(i == 0)
    def _():
        for r in range(tb):
            gather(src_ref, 0, r).start()

    @pl.when(cnt > 0)
    def _():
        for r in range(tb):
            gather(src_ref, slot, r).wait()
        for r in range(tb):
            gather(nxt_ref, other, r).start()
        x = xbuf[slot]
        xb = _rms(x, nm_ref[...]).astype(BF16)
        rt = rt_ref[0]

        def expert(wg_ref, wu_ref, wd_ref, wcol):
            gt = _dot(xb, wg_ref[0])
            up = _dot(xb, wu_ref[0])
            act = (gt / (1.0 + jnp.exp(-gt))) * up * wcol
            return _dot(act.astype(BF16), wd_ref[0])

        y = expert(wgl_ref, wul_ref, wdl_ref, rt[:, 0:1]) + expert(wgh_ref, wuh_ref, wdh_ref, rt[:, 1:2])
        ybuf[slot] = _rms(x + y, nf_ref[...])

        scatter_wait(other, prev)

        def issue(r):
            pltpu.make_async_copy(ybuf.at[slot, r], out_hbm.at[src_ref[0, 0, r]], ssem.at[slot]).start()

        def issue8(g, c):
            for u in range(8):
                issue(g * 8 + u)
            return c

        def issue1(r, c):
            issue(r)
            return c

        groups = lax.shift_right_logical(cnt, 3)
        lax.fori_loop(0, groups, issue8, 0)
        lax.fori_loop(groups * 8, cnt, issue1, 0)

    @pl.when((cnt == 0) & (prev > 0))
    def _():
        for r in range(tb):
            gather(src_ref, slot, r).wait()
        scatter_wait(other, prev)


def _moe(h1, route, w, tb):
    N = h1.shape[0]
    t = _route_tables(route, tb)
    nblk = t["src"].shape[0]
    idx_spec = lambda f: pl.BlockSpec((1, 1, tb), f, memory_space=pltpu.SMEM)
    wspec = lambda shape, which: pl.BlockSpec(
        (1,) + shape, (lambda i, lo, hi, nv: (lo[i], 0, 0)) if which == 0 else (lambda i, lo, hi, nv: (hi[i], 0, 0)))
    up_shape, down_shape = (D_MODEL, D_FF_EXPERT), (D_FF_EXPERT, D_MODEL)
    grid_spec = pltpu.PrefetchScalarGridSpec(
        num_scalar_prefetch=3, grid=(nblk,),
        in_specs=[
            idx_spec(lambda i, lo, hi, nv: (i, 0, 0)),
            idx_spec(lambda i, lo, hi, nv: (jnp.minimum(i + 1, nblk - 1), 0, 0)),
            pl.BlockSpec((1, tb, 2), lambda i, lo, hi, nv: (i, 0, 0)),
            pl.BlockSpec(memory_space=pl.ANY),
            wspec(up_shape, 0), wspec(up_shape, 0), wspec(down_shape, 0),
            wspec(up_shape, 1), wspec(up_shape, 1), wspec(down_shape, 1),
            pl.BlockSpec((1, D_MODEL), lambda i, lo, hi, nv: (0, 0)),
            pl.BlockSpec((1, D_MODEL), lambda i, lo, hi, nv: (0, 0)),
        ],
        out_specs=pl.BlockSpec(memory_space=pl.ANY),
        scratch_shapes=[
            pltpu.VMEM((2, tb, D_MODEL), F32), pltpu.VMEM((2, tb, D_MODEL), F32),
            pltpu.SemaphoreType.DMA((2,)), pltpu.SemaphoreType.DMA((2,)),
        ],
    )
    return pl.pallas_call(
        functools.partial(_moe_body, tb=tb), grid_spec=grid_spec,
        out_shape=jax.ShapeDtypeStruct((N, D_MODEL), F32),
        compiler_params=_params(("arbitrary",)), name="moe",
    )(t["lo"], t["hi"], t["nvalid"], t["src"], t["src"], t["rt"], h1,
      w["w_eg"], w["w_eu"], w["w_ed"], w["w_eg"], w["w_eu"], w["w_ed"], w["norm_moe"], w["norm_final"])


def _route_tables(route, tb):
    N = route.shape[0]
    bucket = route[:, 0].astype(jnp.int32)
    order = jnp.argsort(bucket).astype(jnp.int32)
    ids = jnp.arange(N_BUCKETS, dtype=jnp.int32)
    counts = jnp.sum((bucket[:, None] == ids[None, :]).astype(jnp.int32), axis=0)
    starts = jnp.cumsum(counts) - counts
    padded = ((counts + tb - 1) // tb) * tb
    pend = jnp.cumsum(padded)
    poff = pend - padded
    nblk = N // tb + N_BUCKETS + 1
    blk_start = jnp.arange(nblk, dtype=jnp.int32) * tb
    active = blk_start < pend[-1]
    bb = jnp.minimum(jnp.sum((pend[None, :] <= blk_start[:, None]).astype(jnp.int32), axis=1), N_BUCKETS - 1)
    r0 = blk_start - poff[bb]
    nvalid = jnp.where(active, jnp.clip(counts[bb] - r0, 0, tb), 0).astype(jnp.int32)
    bb_w = jnp.where(active, bb, bb[jnp.maximum(jnp.sum(active.astype(jnp.int32)) - 1, 0)])
    rows = starts[bb][:, None] + r0[:, None] + jnp.arange(tb, dtype=jnp.int32)[None, :]
    src = order[jnp.clip(rows, 0, N - 1)]
    return {
        "lo": jnp.asarray(_BUCKET_LO)[bb_w], "hi": jnp.asarray(_BUCKET_HI)[bb_w], "nvalid": nvalid,
        "src": src[:, None, :], "rt": route[:, 1:3][src],
    }


def _rope_tables(pos):
    inv_freq = 1.0 / (ROPE_THETA ** (jnp.arange(ROPE_HALF, dtype=F32) / ROPE_HALF))
    ang = pos.astype(F32)[:, None] * inv_freq[None, :]
    c, s = jnp.cos(ang), jnp.sin(ang)
    z = jnp.zeros((pos.shape[0], LANES - QK_ROPE_DIM), F32)
    return {
        "cos_nat": jnp.concatenate([c, c, z], axis=1), "sin_nat": jnp.concatenate([s, s, z], axis=1),
        "cosT": c.T, "sinT": s.T,
    }


def _unit(k, n):
    ang = k.astype(F32) * (-2.0 * math.pi / n)
    return jnp.cos(ang), jnp.sin(ang)


def _fnet_tables(S, tkf, freqs):
    L = S + N_META
    j = freqs.astype(jnp.int32)[:, None]
    er, ei = _unit((j * jnp.arange(tkf, dtype=jnp.int32)[None, :]) % L, L)
    col0 = jnp.arange(S // tkf, dtype=jnp.int32)[None, :] * tkf + N_META
    pr, pi = _unit((j * col0) % L, L)
    mr, mi = _unit((j * jnp.arange(N_META, dtype=jnp.int32)[None, :]) % L, L)
    return {"er": er, "ei": ei, "pr": pr, "pi": pi, "wm": jnp.concatenate([mr, mi], axis=1).astype(BF16)}


def _channel_dft(L):
    c = jnp.arange(FNET_GROUP_DIM, dtype=jnp.int32)
    cr, ci = _unit((c[:, None] * c[None, :]) % FNET_GROUP_DIM, FNET_GROUP_DIM)
    norm = 1.0 / math.sqrt(L * FNET_GROUP_DIM)
    eye = jnp.eye(N_FNET_GROUPS, dtype=F32)
    return (jnp.kron(eye, cr * norm).astype(BF16), jnp.kron(eye, -ci * norm).astype(BF16))


def _prep_weights(norm_attn, w_in, g_q, w_uq, g_kv, w_uk, w_uv, w_o_mla, w_fo, w_out, norm_moe,
                  w_rg, b_rg, w_re, b_re, w_eg, w_eu, w_ed, norm_final):
    s1 = Q_LORA_RANK
    s2 = s1 + KV_LORA_RANK
    s3 = s2 + QK_ROPE_DIM
    s4 = s3 + FNET_DIM
    w_in_p = jnp.concatenate(
        [w_in[:, :s1], w_in[:, s1:s2], w_in[:, s3:s4], w_in[:, s4:], w_in[:, s2:s3],
         jnp.zeros((D_MODEL, LANES - QK_ROPE_DIM), w_in.dtype)], axis=1)
    w_uq_p = jnp.pad(w_uq, ((0, 0), (0, 0), (0, 2 * LANES - QK_DIM)))
    r = jnp.concatenate([w_re, w_rg, jnp.zeros((D_MODEL, LANES - N_EXPERTS - N_EXPERT_GROUPS), F32)], axis=1)
    r_hi = r.astype(BF16)
    r_b = jnp.concatenate([b_re, b_rg, jnp.zeros((LANES - N_EXPERTS - N_EXPERT_GROUPS,), F32)])[None, :]
    return {
        "norm_attn": norm_attn[None, :], "w_in": w_in_p.astype(BF16),
        "g_q": g_q[None, :], "w_uq": w_uq_p.reshape(Q_LORA_RANK, N_HEADS * 2 * LANES).astype(BF16),
        "g_kv": g_kv[None, :], "w_ukT": w_uk.reshape(KV_LORA_RANK, N_HEADS * QK_NOPE_DIM).T.astype(BF16),
        "w_uv": w_uv.reshape(KV_LORA_RANK, N_HEADS * V_HEAD_DIM).astype(BF16),
        "w_o": w_o_mla.astype(BF16), "w_out": w_out.astype(BF16),
        "w_fo": jnp.stack([w_fo, w_fo[_CHANNEL_FLIP]]).astype(BF16),
        "norm_moe": norm_moe[None, :], "r_hi": r_hi, "r_lo": (r - r_hi.astype(F32)).astype(BF16), "r_b": r_b,
        "w_eg": w_eg.reshape(N_EXPERTS, D_MODEL, D_FF_EXPERT).astype(BF16),
        "w_eu": w_eu.reshape(N_EXPERTS, D_MODEL, D_FF_EXPERT).astype(BF16),
        "w_ed": w_ed.reshape(N_EXPERTS, D_FF_EXPERT, D_MODEL).astype(BF16),
        "norm_final": norm_final[None, :],
    }


def _tiles(B, S):
    ts = min(512, S // 2)
    return {
        "ts": ts, "tq": min(2048, S), "tm": min(1024, S // 2), "tkf": min(512, S),
        "tn": min(2048, B * FNET_DIM), "tb": 128,
    }


def _trunk(x, meta, w):
    B, S, _ = x.shape
    L = S + N_META
    t = _tiles(B, S)
    wl = dict(w)
    wl["cc"], wl["sc"] = _channel_dft(L)

    meta_pad = jnp.zeros((1, META_PAD, D_MODEL), F32).at[0, :N_META].set(meta)
    _, kT_m, v_m, a_m, b_m, _ = _in_proj(meta_pad, wl, _rope_tables(jnp.arange(META_PAD)), META_PAD)
    q, kT, v, a, b, gates = _in_proj(x, wl, _rope_tables(jnp.arange(S) + N_META), t["ts"])

    v_meta = v_m[0].reshape(META_PAD, N_HEADS, V_HEAD_DIM).transpose(1, 0, 2)
    att = _attention(q, kT, v, kT_m[0, :, 0], v_meta, t["tq"])
    abm = jnp.concatenate([a_m[:N_META], b_m[:N_META]], axis=0)
    half = S // 2
    f_lo = _fnet(a, b, _fnet_tables(S, t["tkf"], jnp.arange(half) + N_META), abm, t["tm"], t["tkf"], t["tn"])
    f_meta = _fnet(a, b, _fnet_tables(S, t["tkf"], jnp.arange(N_META)), abm, N_META, t["tkf"], t["tn"])
    f_hi = _fnet_mirror(f_lo, f_meta, min(512, half), t["tn"])
    h1, route = _post_mix(x, att, f_lo, f_hi, gates, wl, t["ts"])
    y = _moe(h1.reshape(B * S, D_MODEL), route.reshape(B * S, LANES), wl, t["tb"])
    return y.reshape(B, S, D_MODEL)


def kernel(x_prompt, x_sample, meta_tokens, norm_attn, w_in, g_q, w_uq, g_kv, w_uk, w_uv, w_o_mla, w_fo, w_out,
           norm_moe, w_router_group, b_router_group, w_router_expert, b_router_expert, w_exp_gate, w_exp_up,
           w_exp_down, norm_final):
    w = _prep_weights(norm_attn[0], w_in[0], g_q[0], w_uq[0], g_kv[0], w_uk[0], w_uv[0], w_o_mla[0], w_fo[0],
                      w_out[0], norm_moe[0], w_router_group[0], b_router_group[0], w_router_expert[0],
                      b_router_expert[0], w_exp_gate[0], w_exp_up[0], w_exp_down[0], norm_final)
    return _trunk(x_prompt, meta_tokens, w), _trunk(x_sample, meta_tokens, w)
```

```python
import functools
import math

import jax
import jax.numpy as jnp
import numpy as np
from jax import lax
from jax.experimental import pallas as pl
from jax.experimental.pallas import tpu as pltpu

D_MODEL = 1024
N_META = 16
N_HEADS = 8
QK_NOPE_DIM = 128
QK_ROPE_DIM = 64
ROPE_HALF = QK_ROPE_DIM // 2
QK_DIM = QK_NOPE_DIM + QK_ROPE_DIM
QK_PAD = 256
V_HEAD_DIM = 128
Q_LORA_RANK = 384
KV_LORA_RANK = 256
ROPE_THETA = 10000.0
N_FNET_GROUPS = 4
FNET_GROUP_DIM = 128
FNET_DIM = N_FNET_GROUPS * FNET_GROUP_DIM
N_EXPERT_GROUPS = 4
EXPERTS_PER_GROUP = 8
N_EXPERTS = N_EXPERT_GROUPS * EXPERTS_PER_GROUP
D_FF_EXPERT = 256
EPS = 1e-6
PAIRS_PER_GROUP = EXPERTS_PER_GROUP * (EXPERTS_PER_GROUP - 1) // 2
N_BUCKETS = N_EXPERT_GROUPS * PAIRS_PER_GROUP
_CHANNEL_FLIP = np.array([g * FNET_GROUP_DIM + (FNET_GROUP_DIM - c) % FNET_GROUP_DIM
                          for g in range(N_FNET_GROUPS) for c in range(FNET_GROUP_DIM)], np.int32)
_PAIRS = [(a, b) for a in range(EXPERTS_PER_GROUP) for b in range(a + 1, EXPERTS_PER_GROUP)]
_BUCKET_LO = np.array([g * EXPERTS_PER_GROUP + a for g in range(N_EXPERT_GROUPS) for a, _ in _PAIRS], np.int32)
_BUCKET_HI = np.array([g * EXPERTS_PER_GROUP + b for g in range(N_EXPERT_GROUPS) for _, b in _PAIRS], np.int32)

LANES = 128
META_PAD = 128
VMEM_LIMIT_BYTES = 56 * 1024 * 1024
NEG_BIG = -1e30
FAST_ROW_SUM_LIMIT = 2.0 ** 24

_CQ0, _CQ1 = 0, Q_LORA_RANK
_CKV0, _CKV1 = _CQ1, _CQ1 + KV_LORA_RANK
_UF0, _UF1 = _CKV1, _CKV1 + FNET_DIM
_G0, _G1 = _UF1, _UF1 + 2 * D_MODEL
_KR0, _KR1 = _G1, _G1 + LANES
IN_DIM_PADDED = _KR1

F32 = jnp.float32
BF16 = jnp.bfloat16


def _dot(a, b):
    return jnp.dot(a, b, preferred_element_type=F32)


def _rms(x, g):
    ms = jnp.mean(x * x, axis=-1, keepdims=True)
    return x * lax.rsqrt(ms + EPS) * g


def _params(semantics):
    return pltpu.CompilerParams(dimension_semantics=semantics, vmem_limit_bytes=VMEM_LIMIT_BYTES)


def _in_proj_body(x_ref, na_ref, win_ref, gq_ref, wuq_ref, gkv_ref, wukT_ref, wuv_ref,
                  cosn_ref, sinn_ref, cosT_ref, sinT_ref, cc_ref, sc_ref,
                  q_ref, kT_ref, v_ref, a_ref, b_ref, gate_ref):
    x = x_ref[0]
    hb = _rms(x, na_ref[...]).astype(BF16)

    cqn = _rms(_dot(hb, win_ref[:, _CQ0:_CQ1]), gq_ref[...])
    qa = _dot(cqn.astype(BF16), wuq_ref[...])
    cn = cosn_ref[...]
    sn = sinn_ref[...]
    first_half = lax.broadcasted_iota(jnp.int32, cn.shape, 1) < ROPE_HALF
    scale = math.log2(math.e) / math.sqrt(QK_DIM)
    for h in range(N_HEADS):
        c0 = h * 2 * LANES
        qr = qa[:, c0 + LANES:c0 + 2 * LANES]
        rot = jnp.where(first_half, -pltpu.roll(qr, LANES - ROPE_HALF, 1), pltpu.roll(qr, ROPE_HALF, 1))
        q_ref[0, h, :, 0:QK_NOPE_DIM] = (qa[:, c0:c0 + LANES] * scale).astype(BF16)
        q_ref[0, h, :, QK_NOPE_DIM:QK_PAD] = ((qr * cn + rot * sn) * scale).astype(BF16)

    ckvn = _rms(_dot(hb, win_ref[:, _CKV0:_CKV1]), gkv_ref[...])
    knT = _dot(wukT_ref[...], ckvn.T.astype(BF16))
    krT = _dot(hb, win_ref[:, _KR0:_KR1]).T
    x1 = krT[0:ROPE_HALF]
    x2 = krT[ROPE_HALF:QK_ROPE_DIM]
    cT = cosT_ref[...]
    sT = sinT_ref[...]
    r1 = (x1 * cT - x2 * sT).astype(BF16)
    r2 = (x2 * cT + x1 * sT).astype(BF16)
    for h in range(N_HEADS):
        kT_ref[0, h, 0, 0:QK_NOPE_DIM, :] = knT[h * QK_NOPE_DIM:(h + 1) * QK_NOPE_DIM].astype(BF16)
        kT_ref[0, h, 0, QK_NOPE_DIM:QK_NOPE_DIM + ROPE_HALF, :] = r1
        kT_ref[0, h, 0, QK_NOPE_DIM + ROPE_HALF:QK_DIM, :] = r2
        kT_ref[0, h, 0, QK_DIM:QK_PAD, :] = jnp.zeros((QK_PAD - QK_DIM, r2.shape[1]), BF16)
    v_ref[0] = _dot(ckvn.astype(BF16), wuv_ref[...]).astype(BF16)

    ub = _dot(hb, win_ref[:, _UF0:_UF1]).astype(BF16)
    a_ref[...] = _dot(ub, cc_ref[...]).astype(BF16)
    b_ref[...] = _dot(ub, sc_ref[...]).astype(BF16)

    gl = _dot(hb, win_ref[:, _G0:_G1])
    gate_ref[0] = (1.0 / (1.0 + jnp.exp(-gl))).astype(BF16)


def _in_proj(x, w, rope, ts):
    B, S, _ = x.shape
    nt = S // ts
    const = lambda shape: pl.BlockSpec(shape, lambda b, t: (0,) * len(shape))
    out_shape = (
        jax.ShapeDtypeStruct((B, N_HEADS, S, QK_PAD), BF16),
        jax.ShapeDtypeStruct((B, N_HEADS, nt, QK_PAD, ts), BF16),
        jax.ShapeDtypeStruct((B, S, N_HEADS * V_HEAD_DIM), BF16),
        jax.ShapeDtypeStruct((S, B * FNET_DIM), BF16),
        jax.ShapeDtypeStruct((S, B * FNET_DIM), BF16),
        jax.ShapeDtypeStruct((B, S, 2 * D_MODEL), BF16),
    )
    in_specs = [
        pl.BlockSpec((1, ts, D_MODEL), lambda b, t: (b, t, 0)),
        const((1, D_MODEL)),
        const((D_MODEL, IN_DIM_PADDED)),
        const((1, Q_LORA_RANK)),
        const((Q_LORA_RANK, N_HEADS * 2 * LANES)),
        const((1, KV_LORA_RANK)),
        const((N_HEADS * QK_NOPE_DIM, KV_LORA_RANK)),
        const((KV_LORA_RANK, N_HEADS * V_HEAD_DIM)),
        pl.BlockSpec((ts, LANES), lambda b, t: (t, 0)),
        pl.BlockSpec((ts, LANES), lambda b, t: (t, 0)),
        pl.BlockSpec((ROPE_HALF, ts), lambda b, t: (0, t)),
        pl.BlockSpec((ROPE_HALF, ts), lambda b, t: (0, t)),
        const((FNET_DIM, FNET_DIM)),
        const((FNET_DIM, FNET_DIM)),
    ]
    out_specs = (
        pl.BlockSpec((1, N_HEADS, ts, QK_PAD), lambda b, t: (b, 0, t, 0)),
        pl.BlockSpec((1, N_HEADS, 1, QK_PAD, ts), lambda b, t: (b, 0, t, 0, 0)),
        pl.BlockSpec((1, ts, N_HEADS * V_HEAD_DIM), lambda b, t: (b, t, 0)),
        pl.BlockSpec((ts, FNET_DIM), lambda b, t: (t, b)),
        pl.BlockSpec((ts, FNET_DIM), lambda b, t: (t, b)),
        pl.BlockSpec((1, ts, 2 * D_MODEL), lambda b, t: (b, t, 0)),
    )
    return pl.pallas_call(
        _in_proj_body, grid=(B, nt), in_specs=in_specs, out_specs=out_specs, out_shape=out_shape,
        compiler_params=_params(("parallel", "parallel")), name="in_proj",
    )(x, w["norm_attn"], w["w_in"], w["g_q"], w["w_uq"], w["g_kv"], w["w_ukT"], w["w_uv"],
      rope["cos_nat"], rope["sin_nat"], rope["cosT"], rope["sinT"], w["cc"], w["sc"])


def _attn_body(q_ref, kT_ref, v_ref, kmT_ref, vm_ref, o_ref, s_ref, p_ref, acc_ref, accx_ref, *, nk, tk):
    q = q_ref[0, 0]
    col = lax.broadcasted_iota(jnp.int32, (q.shape[0], META_PAD), 1)

    def meta_scores():
        return jnp.where(col < N_META, _dot(q, kmT_ref[0]), NEG_BIG)

    def v_chunk(j):
        return v_ref[0, pl.ds(pl.multiple_of(j * tk, tk), tk), :]

    def ones_column(rows):
        return jnp.where(lax.broadcasted_iota(jnp.int32, (rows, LANES), 1) == 0, 1.0, 0.0).astype(BF16)

    ones_col = ones_column(tk)
    s_meta = meta_scores()
    s_ref[0] = _dot(q, kT_ref[0, 0, 0])
    m0 = jnp.maximum(jnp.max(s_meta, axis=1, keepdims=True), jnp.max(s_ref[0], axis=1, keepdims=True))
    vmx = jnp.concatenate([vm_ref[0], ones_column(META_PAD)], axis=1)
    accx_ref[...] = _dot(jnp.exp2(s_meta - m0).astype(BF16), vmx)
    p_ref[0] = jnp.exp2(s_ref[0] - m0).astype(BF16)

    def probs(j, slot):
        p_ref[slot] = jnp.exp2(_dot(q, kT_ref[0, 0, j]) - m0).astype(BF16)

    def accumulate(j, slot):
        accx_ref[...] += _dot(p_ref[slot], jnp.concatenate([v_chunk(j), ones_col], axis=1))

    def fast_pair(jj, c):
        a = 2 * jj + 1
        accumulate(a - 1, 0)
        probs(a, 1)
        accumulate(a, 1)
        probs(a + 1, 0)
        return c

    lax.fori_loop(0, (nk - 2) // 2, fast_pair, 0)
    accumulate(nk - 2, 0)
    probs(nk - 1, 1)
    accumulate(nk - 1, 1)
    l_fast = accx_ref[:, V_HEAD_DIM:2 * LANES][:, 0:1]
    fast_ok = jnp.max(l_fast) < FAST_ROW_SUM_LIMIT

    @pl.when(fast_ok)
    def _():
        o_ref[0] = (accx_ref[:, 0:V_HEAD_DIM] / l_fast).astype(BF16)

    def scores(j, slot):
        s_ref[slot] = _dot(q, kT_ref[0, 0, j])

    def values(j, slot, alpha):
        acc_ref[...] = alpha * acc_ref[...] + _dot(p_ref[slot], v_chunk(j))

    def softmax(slot, m, l):
        s = s_ref[slot]
        m_new = jnp.maximum(m, jnp.max(s, axis=1, keepdims=True))
        alpha = jnp.exp2(m - m_new)
        p = jnp.exp2(s - m_new)
        p_ref[slot] = p.astype(BF16)
        return m_new, alpha * l + jnp.sum(p, axis=1, keepdims=True), alpha

    def pair(jj, carry, last):
        m, l, alpha = carry
        a = 2 * jj
        values(jnp.maximum(a - 1, 0), 1, alpha)
        scores(a + 1, 1)
        m, l, alpha = softmax(0, m, l)
        values(a, 0, alpha)
        if not last:
            scores(a + 2, 0)
        return softmax(1, m, l)

    @pl.when(jnp.logical_not(fast_ok))
    def _():
        s = meta_scores()
        m = jnp.max(s, axis=1, keepdims=True)
        p = jnp.exp2(s - m)
        l = jnp.sum(p, axis=1, keepdims=True)
        acc_ref[...] = _dot(p.astype(BF16), vm_ref[0])
        p_ref[1] = jnp.zeros(p_ref.shape[1:], BF16)
        scores(0, 0)
        carry = (m, l, jnp.ones_like(m))
        carry = lax.fori_loop(0, nk // 2 - 1, functools.partial(pair, last=False), carry)
        m, l, alpha = pair(nk // 2 - 1, carry, last=True)
        values(nk - 1, 1, alpha)
        o_ref[0] = (acc_ref[...] / l).astype(BF16)


def _attention(q, kT, v, kT_meta, v_meta, tq):
    B, H, S, _ = q.shape
    nk, tk = kT.shape[2], kT.shape[4]
    assert nk % 2 == 0, "the key loop is unrolled in pairs"
    return pl.pallas_call(
        functools.partial(_attn_body, nk=nk, tk=tk),
        grid=(B, H, S // tq),
        in_specs=[
            pl.BlockSpec((1, 1, tq, QK_PAD), lambda b, h, i: (b, h, i, 0)),
            pl.BlockSpec((1, 1, nk, QK_PAD, tk), lambda b, h, i: (b, h, 0, 0, 0)),
            pl.BlockSpec((1, S, V_HEAD_DIM), lambda b, h, i: (b, 0, h)),
            pl.BlockSpec((1, QK_PAD, META_PAD), lambda b, h, i: (h, 0, 0)),
            pl.BlockSpec((1, META_PAD, V_HEAD_DIM), lambda b, h, i: (h, 0, 0)),
        ],
        out_specs=pl.BlockSpec((1, tq, V_HEAD_DIM), lambda b, h, i: (b, i, h)),
        out_shape=jax.ShapeDtypeStruct((B, S, H * V_HEAD_DIM), BF16),
        scratch_shapes=[
            pltpu.VMEM((2, tq, tk), F32), pltpu.VMEM((2, tq, tk), BF16), pltpu.VMEM((tq, V_HEAD_DIM), F32),
            pltpu.VMEM((tq, 2 * LANES), F32),
        ],
        compiler_params=_params(("parallel", "parallel", "parallel")), name="attention",
    )(q, kT, v, kT_meta, v_meta)


def _fnet_body(a_ref, b_ref, er_ref, ei_ref, pr_ref, pi_ref, wm_ref, abm_ref, o_ref, acc_ref, *, reps):
    ci = pl.program_id(2)

    @pl.when(ci == 0)
    def _():
        mt = _dot(wm_ref[...], abm_ref[...])
        acc_ref[...] = jnp.concatenate([mt] * reps, axis=1)

    lane = lax.broadcasted_iota(jnp.int32, pr_ref.shape, 1)
    sel = lane == ci
    pr = jnp.sum(jnp.where(sel, pr_ref[...], 0.0), axis=1, keepdims=True)
    pi = jnp.sum(jnp.where(sel, pi_ref[...], 0.0), axis=1, keepdims=True)
    er = er_ref[...]
    ei = ei_ref[...]
    wr = (pr * er - pi * ei).astype(BF16)
    wi = (pr * ei + pi * er).astype(BF16)
    acc_ref[...] += _dot(wr, a_ref[...]) + _dot(wi, b_ref[...])

    @pl.when(ci == pl.num_programs(2) - 1)
    def _():
        o_ref[...] = acc_ref[...].astype(BF16)


def _fnet(a, b, tabs, abm, tm, tkf, tn):
    S, NB = a.shape
    rows = tabs["er"].shape[0]
    nkt = S // tkf
    return pl.pallas_call(
        functools.partial(_fnet_body, reps=tn // FNET_DIM),
        grid=(NB // tn, rows // tm, nkt),
        in_specs=[
            pl.BlockSpec((tkf, tn), lambda n, r, c: (c, n)),
            pl.BlockSpec((tkf, tn), lambda n, r, c: (c, n)),
            pl.BlockSpec((tm, tkf), lambda n, r, c: (r, 0)),
            pl.BlockSpec((tm, tkf), lambda n, r, c: (r, 0)),
            pl.BlockSpec((tm, nkt), lambda n, r, c: (r, 0)),
            pl.BlockSpec((tm, nkt), lambda n, r, c: (r, 0)),
            pl.BlockSpec((tm, 2 * N_META), lambda n, r, c: (r, 0)),
            pl.BlockSpec((2 * N_META, FNET_DIM), lambda n, r, c: (0, 0)),
        ],
        out_specs=pl.BlockSpec((tm, tn), lambda n, r, c: (r, n)),
        out_shape=jax.ShapeDtypeStruct((rows, NB), BF16),
        scratch_shapes=[pltpu.VMEM((tm, tn), F32)],
        compiler_params=_params(("parallel", "parallel", "arbitrary")), name="fnet",
    )(a, b, tabs["er"], tabs["ei"], tabs["pr"], tabs["pi"], tabs["wm"], abm)


def _fnet_mirror_body(dk_ref, dkm_ref, x0_ref, j1_ref, j2_ref, m_ref):
    k = pl.num_programs(1) - 1 - pl.program_id(1)
    lower = jnp.where(k == 0, x0_ref[...], dkm_ref[...])
    m_ref[...] = (_dot(j1_ref[...], dk_ref[...]) + _dot(j2_ref[...], lower)).astype(BF16)


def _fnet_mirror(d, x0, tr, tn):
    half, NB = d.shape
    nt = half // tr
    a = jnp.arange(tr, dtype=jnp.int32)[:, None]
    c = jnp.arange(tr, dtype=jnp.int32)[None, :]
    j1 = (c == tr - N_META - a).astype(BF16)
    j2 = (c == 2 * tr - N_META - a).astype(BF16)
    x0_tile = jnp.zeros((tr, NB), BF16).at[tr - N_META:].set(x0)
    return pl.pallas_call(
        _fnet_mirror_body, grid=(NB // tn, nt),
        in_specs=[
            pl.BlockSpec((tr, tn), lambda n, i: (nt - 1 - i, n)),
            pl.BlockSpec((tr, tn), lambda n, i: (jnp.maximum(nt - 2 - i, 0), n)),
            pl.BlockSpec((tr, tn), lambda n, i: (0, n)),
            pl.BlockSpec((tr, tr), lambda n, i: (0, 0)),
            pl.BlockSpec((tr, tr), lambda n, i: (0, 0)),
        ],
        out_specs=pl.BlockSpec((tr, tn), lambda n, i: (i, n)),
        out_shape=jax.ShapeDtypeStruct((half, NB), BF16),
        compiler_params=_params(("parallel", "parallel")), name="fnet_mirror",
    )(d, d, x0_tile, j1, j2)


def _post_mix_body(x_ref, att_ref, flo_ref, fhi_ref, gate_ref, wo_ref, wfo_ref, wout_ref, nm_ref,
                   rhi_ref, rlo_ref, rb_ref, h1_ref, route_ref):
    a = _dot(att_ref[0], wo_ref[...])
    first_half = pl.program_id(1) < pl.num_programs(1) // 2
    f = jnp.where(first_half, flo_ref[...], fhi_ref[...])
    bb = _dot(f, wfo_ref[0])
    g = gate_ref[0].astype(F32)
    merged = g[:, :D_MODEL] * a + g[:, D_MODEL:] * bb
    h1 = x_ref[0] + _dot(merged.astype(BF16), wout_ref[...])
    h1_ref[0] = h1
    xn = _rms(h1, nm_ref[...])
    xh = xn.astype(BF16)

    xl = (xn - xh.astype(F32)).astype(BF16)
    logits = _dot(xh, rhi_ref[...]) + (_dot(xh, rlo_ref[...]) + _dot(xl, rhi_ref[...])) + rb_ref[...]
    lane = lax.broadcasted_iota(jnp.int32, logits.shape, 1)
    lanef = lane.astype(F32)
    far = float(2 * LANES)

    gl = jnp.where((lane >= N_EXPERTS) & (lane < N_EXPERTS + N_EXPERT_GROUPS), logits, NEG_BIG)
    gmax = jnp.max(gl, axis=1, keepdims=True)
    g_w = 1.0 / jnp.sum(jnp.exp(gl - gmax), axis=1, keepdims=True)
    gidx = jnp.min(jnp.where(gl == gmax, lanef, far), axis=1, keepdims=True) - float(N_EXPERTS)

    grp = (lane >> 3).astype(F32)
    el = jnp.where((lane < N_EXPERTS) & (grp == gidx), logits, NEG_BIG)
    m1 = jnp.max(el, axis=1, keepdims=True)
    i1 = jnp.min(jnp.where(el == m1, lanef, far), axis=1, keepdims=True)
    el2 = jnp.where(lanef == i1, NEG_BIG, el)
    m2 = jnp.max(el2, axis=1, keepdims=True)
    i2 = jnp.min(jnp.where(el2 == m2, lanef, far), axis=1, keepdims=True)
    r = jnp.exp(m2 - m1)
    w1 = g_w / (1.0 + r)
    w2 = g_w * r / (1.0 + r)

    first_is_lo = i1 < i2
    a_loc = jnp.minimum(i1, i2) - gidx * EXPERTS_PER_GROUP
    b_loc = jnp.maximum(i1, i2) - gidx * EXPERTS_PER_GROUP
    pair = a_loc * (2 * EXPERTS_PER_GROUP - 1 - a_loc) * 0.5 + (b_loc - a_loc - 1.0)
    bucket = gidx * PAIRS_PER_GROUP + pair
    w_lo = jnp.where(first_is_lo, w1, w2)
    w_hi = jnp.where(first_is_lo, w2, w1)
    route_ref[0] = jnp.where(lane == 0, bucket, jnp.where(lane == 1, w_lo, jnp.where(lane == 2, w_hi, 0.0)))


def _post_mix(x, att, f_lo, f_hi, gates, w, ts):
    B, S, _ = x.shape
    nh = S // ts // 2
    const = lambda shape: pl.BlockSpec(shape, lambda b, t: (0,) * len(shape))
    tok = lambda width: pl.BlockSpec((1, ts, width), lambda b, t: (b, t, 0))
    return pl.pallas_call(
        _post_mix_body, grid=(B, S // ts),
        in_specs=[
            tok(D_MODEL), tok(D_MODEL),
            pl.BlockSpec((ts, FNET_DIM), lambda b, t: (jnp.minimum(t, nh - 1), b)),
            pl.BlockSpec((ts, FNET_DIM), lambda b, t: (jnp.maximum(t - nh, 0), b)),
            tok(2 * D_MODEL),
            const((D_MODEL, D_MODEL)),
            pl.BlockSpec((1, FNET_DIM, D_MODEL), lambda b, t: (jnp.where(t < nh, 0, 1), 0, 0)),
            const((D_MODEL, D_MODEL)),
            const((1, D_MODEL)), const((D_MODEL, LANES)), const((D_MODEL, LANES)), const((1, LANES)),
        ],
        out_specs=(tok(D_MODEL), tok(LANES)),
        out_shape=(
            jax.ShapeDtypeStruct((B, S, D_MODEL), F32),
            jax.ShapeDtypeStruct((B, S, LANES), F32),
        ),
        compiler_params=_params(("parallel", "parallel")), name="post_mix",
    )(x, att, f_lo, f_hi, gates, w["w_o"], w["w_fo"], w["w_out"], w["norm_moe"], w["r_hi"], w["r_lo"], w["r_b"])


def _moe_body(lo_ref, hi_ref, nv_ref, src_ref, nxt_ref, rt_ref, h1_hbm,
              wgl_ref, wul_ref, wdl_ref, wgh_ref, wuh_ref, wdh_ref, nm_ref, nf_ref,
              out_hbm, xbuf, ybuf, gsem, ssem, *, tb):
    i = pl.program_id(0)
    slot = lax.rem(i, 2)
    other = 1 - slot
    cnt = nv_ref[i]
    prev = jnp.where(i > 0, nv_ref[jnp.maximum(i - 1, 0)], 0)

    def gather(idx_ref, s, r):
        return pltpu.make_async_copy(h1_hbm.at[idx_ref[0, 0, r]], xbuf.at[s, r], gsem.at[s])

    def scatter_wait(s, rows):
        full = pl.multiple_of(lax.shift_left(lax.shift_right_logical(rows, 3), 3), 8)

        @pl.when(full > 0)
        def _():
            pltpu.make_async_copy(ybuf.at[s, pl.ds(0, full)], out_hbm.at[pl.ds(0, full)], ssem.at[s]).wait()

        def one(r, c):
            pltpu.make_async_copy(ybuf.at[s, r], out_hbm.at[0], ssem.at[s]).wait()
            return c

        lax.fori_loop(full, rows, one, 0)

    @pl.when(i == 0)
    def _():
        for r in range(tb):
            gather(src_ref, 0, r).start()

    @pl.when(cnt > 0)
    def _():
        for r in range(tb):
            gather(src_ref, slot, r).wait()
        for r in range(tb):
            gather(nxt_ref, other, r).start()
        x = xbuf[slot]
        xb = _rms(x, nm_ref[...]).astype(BF16)
        rt = rt_ref[0].T

        def expert(wg_ref, wu_ref, wd_ref, wcol):
            gt = _dot(xb, wg_ref[0])
            up = _dot(xb, wu_ref[0])
            act = (gt / (1.0 + jnp.exp(-gt))) * up * wcol
            return _dot(act.astype(BF16), wd_ref[0])

        y = expert(wgl_ref, wul_ref, wdl_ref, rt[:, 0:1]) + expert(wgh_ref, wuh_ref, wdh_ref, rt[:, 1:2])
        ybuf[slot] = _rms(x + y, nf_ref[...])

        scatter_wait(other, prev)

        def issue(r):
            pltpu.make_async_copy(ybuf.at[slot, r], out_hbm.at[src_ref[0, 0, r]], ssem.at[slot]).start()

        def issue8(g, c):
            for u in range(8):
                issue(g * 8 + u)
            return c

        def issue1(r, c):
            issue(r)
            return c

        groups = lax.shift_right_logical(cnt, 3)
        lax.fori_loop(0, groups, issue8, 0)
        lax.fori_loop(groups * 8, cnt, issue1, 0)

    @pl.when((cnt == 0) & (prev > 0))
    def _():
        for r in range(tb):
            gather(src_ref, slot, r).wait()
        scatter_wait(other, prev)


def _moe(h1, route, w, tb):
    N = h1.shape[0]
    t = _route_tables(route, tb)
    nblk = t["src"].shape[0]
    idx_spec = lambda f: pl.BlockSpec((1, 1, tb), f, memory_space=pltpu.SMEM)
    wspec = lambda shape, which: pl.BlockSpec(
        (1,) + shape, (lambda i, lo, hi, nv: (lo[i], 0, 0)) if which == 0 else (lambda i, lo, hi, nv: (hi[i], 0, 0)))
    up_shape, down_shape = (D_MODEL, D_FF_EXPERT), (D_FF_EXPERT, D_MODEL)
    grid_spec = pltpu.PrefetchScalarGridSpec(
        num_scalar_prefetch=3, grid=(nblk,),
        in_specs=[
            idx_spec(lambda i, lo, hi, nv: (i, 0, 0)),
            idx_spec(lambda i, lo, hi, nv: (jnp.minimum(i + 1, nblk - 1), 0, 0)),
            pl.BlockSpec((1, LANES, tb), lambda i, lo, hi, nv: (i, 0, 0)),
            pl.BlockSpec(memory_space=pl.ANY),
            wspec(up_shape, 0), wspec(up_shape, 0), wspec(down_shape, 0),
            wspec(up_shape, 1), wspec(up_shape, 1), wspec(down_shape, 1),
            pl.BlockSpec((1, D_MODEL), lambda i, lo, hi, nv: (0, 0)),
            pl.BlockSpec((1, D_MODEL), lambda i, lo, hi, nv: (0, 0)),
        ],
        out_specs=pl.BlockSpec(memory_space=pl.ANY),
        scratch_shapes=[
            pltpu.VMEM((2, tb, D_MODEL), F32), pltpu.VMEM((2, tb, D_MODEL), F32),
            pltpu.SemaphoreType.DMA((2,)), pltpu.SemaphoreType.DMA((2,)),
        ],
    )
    return pl.pallas_call(
        functools.partial(_moe_body, tb=tb), grid_spec=grid_spec,
        out_shape=jax.ShapeDtypeStruct((N, D_MODEL), F32),
        compiler_params=_params(("arbitrary",)), name="moe",
    )(t["lo"], t["hi"], t["nvalid"], t["src"], t["src"], t["rt"], h1,
      w["w_eg"], w["w_eu"], w["w_ed"], w["w_eg"], w["w_eu"], w["w_ed"], w["norm_moe"], w["norm_final"])


def _route_tables(route, tb):
    N = route.shape[0]
    bucket = route[:, 0].astype(jnp.int32)
    order = jnp.argsort(bucket).astype(jnp.int32)
    ids = jnp.arange(N_BUCKETS, dtype=jnp.int32)
    counts = jnp.sum((bucket[:, None] == ids[None, :]).astype(jnp.int32), axis=0)
    starts = jnp.cumsum(counts) - counts
    padded = ((counts + tb - 1) // tb) * tb
    pend = jnp.cumsum(padded)
    poff = pend - padded
    nblk = N // tb + N_BUCKETS + 1
    blk_start = jnp.arange(nblk, dtype=jnp.int32) * tb
    active = blk_start < pend[-1]
    bb = jnp.minimum(jnp.sum((pend[None, :] <= blk_start[:, None]).astype(jnp.int32), axis=1), N_BUCKETS - 1)
    r0 = blk_start - poff[bb]
    nvalid = jnp.where(active, jnp.clip(counts[bb] - r0, 0, tb), 0).astype(jnp.int32)
    bb_w = jnp.where(active, bb, bb[jnp.maximum(jnp.sum(active.astype(jnp.int32)) - 1, 0)])
    rows = starts[bb][:, None] + r0[:, None] + jnp.arange(tb, dtype=jnp.int32)[None, :]
    src = order[jnp.clip(rows, 0, N - 1)]
    return {
        "lo": jnp.asarray(_BUCKET_LO)[bb_w], "hi": jnp.asarray(_BUCKET_HI)[bb_w], "nvalid": nvalid,
        "src": src[:, None, :],
        "rt": jnp.zeros((nblk, LANES, tb), F32).at[:, 0].set(route[:, 1][src]).at[:, 1].set(route[:, 2][src]),
    }


def _rope_tables(pos):
    inv_freq = 1.0 / (ROPE_THETA ** (jnp.arange(ROPE_HALF, dtype=F32) / ROPE_HALF))
    ang = pos.astype(F32)[:, None] * inv_freq[None, :]
    c, s = jnp.cos(ang), jnp.sin(ang)
    z = jnp.zeros((pos.shape[0], LANES - QK_ROPE_DIM), F32)
    return {
        "cos_nat": jnp.concatenate([c, c, z], axis=1), "sin_nat": jnp.concatenate([s, s, z], axis=1),
        "cosT": c.T, "sinT": s.T,
    }


def _unit(k, n):
    ang = k.astype(F32) * (-2.0 * math.pi / n)
    return jnp.cos(ang), jnp.sin(ang)


def _fnet_tables(S, tkf, freqs):
    L = S + N_META
    j = freqs.astype(jnp.int32)[:, None]
    er, ei = _unit((j * jnp.arange(tkf, dtype=jnp.int32)[None, :]) % L, L)
    col0 = jnp.arange(S // tkf, dtype=jnp.int32)[None, :] * tkf + N_META
    pr, pi = _unit((j * col0) % L, L)
    mr, mi = _unit((j * jnp.arange(N_META, dtype=jnp.int32)[None, :]) % L, L)
    return {"er": er, "ei": ei, "pr": pr, "pi": pi, "wm": jnp.concatenate([mr, mi], axis=1).astype(BF16)}


def _channel_dft(L):
    c = jnp.arange(FNET_GROUP_DIM, dtype=jnp.int32)
    cr, ci = _unit((c[:, None] * c[None, :]) % FNET_GROUP_DIM, FNET_GROUP_DIM)
    norm = 1.0 / math.sqrt(L * FNET_GROUP_DIM)
    eye = jnp.eye(N_FNET_GROUPS, dtype=F32)
    return (jnp.kron(eye, cr * norm).astype(BF16), jnp.kron(eye, -ci * norm).astype(BF16))


def _to_bf16_body(x_ref, o_ref):
    o_ref[...] = x_ref[...].astype(BF16)


def _to_bf16(w):
    e, r, c = w.shape
    spec = pl.BlockSpec((4, r, c), lambda i: (i, 0, 0))
    return pl.pallas_call(
        _to_bf16_body, grid=(e // 4,), in_specs=[spec], out_specs=spec,
        out_shape=jax.ShapeDtypeStruct(w.shape, BF16),
        compiler_params=_params(("parallel",)), name="to_bf16",
    )(w)


def _prep_weights(norm_attn, w_in, g_q, w_uq, g_kv, w_uk, w_uv, w_o_mla, w_fo, w_out, norm_moe,
                  w_rg, b_rg, w_re, b_re, w_eg, w_eu, w_ed, norm_final):
    s1 = Q_LORA_RANK
    s2 = s1 + KV_LORA_RANK
    s3 = s2 + QK_ROPE_DIM
    s4 = s3 + FNET_DIM
    w_in_p = jnp.concatenate(
        [w_in[:, :s1], w_in[:, s1:s2], w_in[:, s3:s4], w_in[:, s4:], w_in[:, s2:s3],
         jnp.zeros((D_MODEL, LANES - QK_ROPE_DIM), w_in.dtype)], axis=1)
    w_uq_p = jnp.pad(w_uq, ((0, 0), (0, 0), (0, 2 * LANES - QK_DIM)))
    r = jnp.concatenate([w_re, w_rg, jnp.zeros((D_MODEL, LANES - N_EXPERTS - N_EXPERT_GROUPS), F32)], axis=1)
    r_hi = r.astype(BF16)
    r_b = jnp.concatenate([b_re, b_rg, jnp.zeros((LANES - N_EXPERTS - N_EXPERT_GROUPS,), F32)])[None, :]
    return {
        "norm_attn": norm_attn[None, :], "w_in": w_in_p.astype(BF16),
        "g_q": g_q[None, :], "w_uq": w_uq_p.reshape(Q_LORA_RANK, N_HEADS * 2 * LANES).astype(BF16),
        "g_kv": g_kv[None, :], "w_ukT": w_uk.reshape(KV_LORA_RANK, N_HEADS * QK_NOPE_DIM).T.astype(BF16),
        "w_uv": w_uv.reshape(KV_LORA_RANK, N_HEADS * V_HEAD_DIM).astype(BF16),
        "w_o": w_o_mla.astype(BF16), "w_out": w_out.astype(BF16),
        "w_fo": jnp.stack([w_fo, w_fo[_CHANNEL_FLIP]]).astype(BF16),
        "norm_moe": norm_moe[None, :], "r_hi": r_hi, "r_lo": (r - r_hi.astype(F32)).astype(BF16), "r_b": r_b,
        "w_eg": _to_bf16(w_eg.reshape(N_EXPERTS, D_MODEL, D_FF_EXPERT)),
        "w_eu": _to_bf16(w_eu.reshape(N_EXPERTS, D_MODEL, D_FF_EXPERT)),
        "w_ed": _to_bf16(w_ed.reshape(N_EXPERTS, D_FF_EXPERT, D_MODEL)),
        "norm_final": norm_final[None, :],
    }


def _tiles(B, S):
    ts = min(512, S // 2)
    return {
        "ts": ts, "tq": min(2048, S), "tm": min(1024, S // 2), "tkf": min(512, S),
        "tn": min(2048, B * FNET_DIM), "tb": 128,
    }


def _trunk(x, meta, w):
    B, S, _ = x.shape
    L = S + N_META
    t = _tiles(B, S)
    wl = dict(w)
    wl["cc"], wl["sc"] = _channel_dft(L)

    meta_pad = jnp.zeros((1, META_PAD, D_MODEL), F32).at[0, :N_META].set(meta)
    _, kT_m, v_m, a_m, b_m, _ = _in_proj(meta_pad, wl, _rope_tables(jnp.arange(META_PAD)), META_PAD)
    q, kT, v, a, b, gates = _in_proj(x, wl, _rope_tables(jnp.arange(S) + N_META), t["ts"])

    v_meta = v_m[0].reshape(META_PAD, N_HEADS, V_HEAD_DIM).transpose(1, 0, 2)
    att = _attention(q, kT, v, kT_m[0, :, 0], v_meta, t["tq"])
    abm = jnp.concatenate([a_m[:N_META], b_m[:N_META]], axis=0)
    half = S // 2
    f_lo = _fnet(a, b, _fnet_tables(S, t["tkf"], jnp.arange(half) + N_META), abm, t["tm"], t["tkf"], t["tn"])
    f_meta = _fnet(a, b, _fnet_tables(S, t["tkf"], jnp.arange(N_META)), abm, N_META, t["tkf"], t["tn"])
    f_hi = _fnet_mirror(f_lo, f_meta, min(512, half), t["tn"])
    h1, route = _post_mix(x, att, f_lo, f_hi, gates, wl, t["ts"])
    y = _moe(h1.reshape(B * S, D_MODEL), route.reshape(B * S, LANES), wl, t["tb"])
    return y.reshape(B, S, D_MODEL)


def kernel(x_prompt, x_sample, meta_tokens, norm_attn, w_in, g_q, w_uq, g_kv, w_uk, w_uv, w_o_mla, w_fo, w_out,
           norm_moe, w_router_group, b_router_group, w_router_expert, b_router_expert, w_exp_gate, w_exp_up,
           w_exp_down, norm_final):
    w = _prep_weights(norm_attn[0], w_in[0], g_q[0], w_uq[0], g_kv[0], w_uk[0], w_uv[0], w_o_mla[0], w_fo[0],
                      w_out[0], norm_moe[0], w_router_group[0], b_router_group[0], w_router_expert[0],
                      b_router_expert[0], w_exp_gate[0], w_exp_up[0], w_exp_down[0], norm_final)
    return _trunk(x_prompt, meta_tokens, w), _trunk(x_sample, meta_tokens, w)
```

```python
import functools
import math

import jax
import jax.numpy as jnp
import numpy as np
from jax import lax
from jax.experimental import pallas as pl
from jax.experimental.pallas import tpu as pltpu

D_MODEL = 1024
N_META = 16
N_HEADS = 8
QK_NOPE_DIM = 128
QK_ROPE_DIM = 64
ROPE_HALF = QK_ROPE_DIM // 2
QK_DIM = QK_NOPE_DIM + QK_ROPE_DIM
QK_PAD = 256
V_HEAD_DIM = 128
Q_LORA_RANK = 384
KV_LORA_RANK = 256
ROPE_THETA = 10000.0
N_FNET_GROUPS = 4
FNET_GROUP_DIM = 128
FNET_DIM = N_FNET_GROUPS * FNET_GROUP_DIM
N_EXPERT_GROUPS = 4
EXPERTS_PER_GROUP = 8
N_EXPERTS = N_EXPERT_GROUPS * EXPERTS_PER_GROUP
D_FF_EXPERT = 256
EPS = 1e-6
PAIRS_PER_GROUP = EXPERTS_PER_GROUP * (EXPERTS_PER_GROUP - 1) // 2
N_BUCKETS = N_EXPERT_GROUPS * PAIRS_PER_GROUP
_CHANNEL_FLIP = np.array([g * FNET_GROUP_DIM + (FNET_GROUP_DIM - c) % FNET_GROUP_DIM
                          for g in range(N_FNET_GROUPS) for c in range(FNET_GROUP_DIM)], np.int32)
_PAIRS = [(a, b) for a in range(EXPERTS_PER_GROUP) for b in range(a + 1, EXPERTS_PER_GROUP)]
_BUCKET_LO = np.array([g * EXPERTS_PER_GROUP + a for g in range(N_EXPERT_GROUPS) for a, _ in _PAIRS], np.int32)
_BUCKET_HI = np.array([g * EXPERTS_PER_GROUP + b for g in range(N_EXPERT_GROUPS) for _, b in _PAIRS], np.int32)

LANES = 128
META_PAD = 128
VMEM_LIMIT_BYTES = 56 * 1024 * 1024
NEG_BIG = -1e30
FAST_ROW_SUM_LIMIT = 2.0 ** 24

_CQ0, _CQ1 = 0, Q_LORA_RANK
_CKV0, _CKV1 = _CQ1, _CQ1 + KV_LORA_RANK
_UF0, _UF1 = _CKV1, _CKV1 + FNET_DIM
_G0, _G1 = _UF1, _UF1 + 2 * D_MODEL
_KR0, _KR1 = _G1, _G1 + LANES
IN_DIM_PADDED = _KR1

F32 = jnp.float32
BF16 = jnp.bfloat16


def _dot(a, b):
    return jnp.dot(a, b, preferred_element_type=F32)


def _rms(x, g):
    ms = jnp.mean(x * x, axis=-1, keepdims=True)
    return x * lax.rsqrt(ms + EPS) * g


def _params(semantics):
    return pltpu.CompilerParams(dimension_semantics=semantics, vmem_limit_bytes=VMEM_LIMIT_BYTES)


def _in_proj_body(x_ref, na_ref, win_ref, gq_ref, wuq_ref, gkv_ref, wukT_ref, wuv_ref,
                  cosn_ref, sinn_ref, cosT_ref, sinT_ref, cc_ref, sc_ref,
                  q_ref, kT_ref, v_ref, a_ref, b_ref, gate_ref):
    x = x_ref[0]
    hb = _rms(x, na_ref[...]).astype(BF16)

    cqn = _rms(_dot(hb, win_ref[:, _CQ0:_CQ1]), gq_ref[...])
    qa = _dot(cqn.astype(BF16), wuq_ref[...])
    cn = cosn_ref[...]
    sn = sinn_ref[...]
    first_half = lax.broadcasted_iota(jnp.int32, cn.shape, 1) < ROPE_HALF
    scale = math.log2(math.e) / math.sqrt(QK_DIM)
    for h in range(N_HEADS):
        c0 = h * 2 * LANES
        qr = qa[:, c0 + LANES:c0 + 2 * LANES]
        rot = jnp.where(first_half, -pltpu.roll(qr, LANES - ROPE_HALF, 1), pltpu.roll(qr, ROPE_HALF, 1))
        q_ref[0, h, :, 0:QK_NOPE_DIM] = (qa[:, c0:c0 + LANES] * scale).astype(BF16)
        q_ref[0, h, :, QK_NOPE_DIM:QK_PAD] = ((qr * cn + rot * sn) * scale).astype(BF16)

    ckvn = _rms(_dot(hb, win_ref[:, _CKV0:_CKV1]), gkv_ref[...])
    knT = _dot(wukT_ref[...], ckvn.T.astype(BF16))
    krT = _dot(hb, win_ref[:, _KR0:_KR1]).T
    x1 = krT[0:ROPE_HALF]
    x2 = krT[ROPE_HALF:QK_ROPE_DIM]
    cT = cosT_ref[...]
    sT = sinT_ref[...]
    r1 = (x1 * cT - x2 * sT).astype(BF16)
    r2 = (x2 * cT + x1 * sT).astype(BF16)
    for h in range(N_HEADS):
        kT_ref[0, h, 0, 0:QK_NOPE_DIM, :] = knT[h * QK_NOPE_DIM:(h + 1) * QK_NOPE_DIM].astype(BF16)
        kT_ref[0, h, 0, QK_NOPE_DIM:QK_NOPE_DIM + ROPE_HALF, :] = r1
        kT_ref[0, h, 0, QK_NOPE_DIM + ROPE_HALF:QK_DIM, :] = r2
        kT_ref[0, h, 0, QK_DIM:QK_PAD, :] = jnp.zeros((QK_PAD - QK_DIM, r2.shape[1]), BF16)
    v_ref[0] = _dot(ckvn.astype(BF16), wuv_ref[...]).astype(BF16)

    ub = _dot(hb, win_ref[:, _UF0:_UF1]).astype(BF16)
    a_ref[...] = _dot(ub, cc_ref[...]).astype(BF16)
    b_ref[...] = _dot(ub, sc_ref[...]).astype(BF16)

    gl = _dot(hb, win_ref[:, _G0:_G1])
    gate_ref[0] = (1.0 / (1.0 + jnp.exp(-gl))).astype(BF16)


def _in_proj(x, w, rope, ts):
    B, S, _ = x.shape
    nt = S // ts
    const = lambda shape: pl.BlockSpec(shape, lambda b, t: (0,) * len(shape))
    out_shape = (
        jax.ShapeDtypeStruct((B, N_HEADS, S, QK_PAD), BF16),
        jax.ShapeDtypeStruct((B, N_HEADS, nt, QK_PAD, ts), BF16),
        jax.ShapeDtypeStruct((B, S, N_HEADS * V_HEAD_DIM), BF16),
        jax.ShapeDtypeStruct((S, B * FNET_DIM), BF16),
        jax.ShapeDtypeStruct((S, B * FNET_DIM), BF16),
        jax.ShapeDtypeStruct((B, S, 2 * D_MODEL), BF16),
    )
    in_specs = [
        pl.BlockSpec((1, ts, D_MODEL), lambda b, t: (b, t, 0)),
        const((1, D_MODEL)),
        const((D_MODEL, IN_DIM_PADDED)),
        const((1, Q_LORA_RANK)),
        const((Q_LORA_RANK, N_HEADS * 2 * LANES)),
        const((1, KV_LORA_RANK)),
        const((N_HEADS * QK_NOPE_DIM, KV_LORA_RANK)),
        const((KV_LORA_RANK, N_HEADS * V_HEAD_DIM)),
        pl.BlockSpec((ts, LANES), lambda b, t: (t, 0)),
        pl.BlockSpec((ts, LANES), lambda b, t: (t, 0)),
        pl.BlockSpec((ROPE_HALF, ts), lambda b, t: (0, t)),
        pl.BlockSpec((ROPE_HALF, ts), lambda b, t: (0, t)),
        const((FNET_DIM, FNET_DIM)),
        const((FNET_DIM, FNET_DIM)),
    ]
    out_specs = (
        pl.BlockSpec((1, N_HEADS, ts, QK_PAD), lambda b, t: (b, 0, t, 0)),
        pl.BlockSpec((1, N_HEADS, 1, QK_PAD, ts), lambda b, t: (b, 0, t, 0, 0)),
        pl.BlockSpec((1, ts, N_HEADS * V_HEAD_DIM), lambda b, t: (b, t, 0)),
        pl.BlockSpec((ts, FNET_DIM), lambda b, t: (t, b)),
        pl.BlockSpec((ts, FNET_DIM), lambda b, t: (t, b)),
        pl.BlockSpec((1, ts, 2 * D_MODEL), lambda b, t: (b, t, 0)),
    )
    return pl.pallas_call(
        _in_proj_body, grid=(B, nt), in_specs=in_specs, out_specs=out_specs, out_shape=out_shape,
        compiler_params=_params(("parallel", "parallel")), name="in_proj",
    )(x, w["norm_attn"], w["w_in"], w["g_q"], w["w_uq"], w["g_kv"], w["w_ukT"], w["w_uv"],
      rope["cos_nat"], rope["sin_nat"], rope["cosT"], rope["sinT"], w["cc"], w["sc"])


def _attn_body(q_ref, kT_ref, v_ref, kmT_ref, vm_ref, o_ref, s_ref, p_ref, acc_ref, accx_ref, *, nk, tk):
    q = q_ref[0, 0]
    col = lax.broadcasted_iota(jnp.int32, (q.shape[0], META_PAD), 1)

    def meta_scores():
        return jnp.where(col < N_META, _dot(q, kmT_ref[0]), NEG_BIG)

    def v_chunk(j):
        return v_ref[0, pl.ds(pl.multiple_of(j * tk, tk), tk), :]

    def ones_column(rows):
        return jnp.where(lax.broadcasted_iota(jnp.int32, (rows, LANES), 1) == 0, 1.0, 0.0).astype(BF16)

    ones_col = ones_column(tk)
    s_meta = meta_scores()
    s_ref[0] = _dot(q, kT_ref[0, 0, 0])
    m0 = jnp.maximum(jnp.max(s_meta, axis=1, keepdims=True), jnp.max(s_ref[0], axis=1, keepdims=True))
    vmx = jnp.concatenate([vm_ref[0], ones_column(META_PAD)], axis=1)
    accx_ref[...] = _dot(jnp.exp2(s_meta - m0).astype(BF16), vmx)
    p_ref[0] = jnp.exp2(s_ref[0] - m0).astype(BF16)

    def probs(j, slot):
        p_ref[slot] = jnp.exp2(_dot(q, kT_ref[0, 0, j]) - m0).astype(BF16)

    def accumulate(j, slot):
        accx_ref[...] += _dot(p_ref[slot], jnp.concatenate([v_chunk(j), ones_col], axis=1))

    probs(1, 1)
    accumulate(0, 0)
    probs(2, 2)
    accumulate(1, 1)
    probs(3, 3)

    def fast_quad(g, c):
        a = 4 * g + 4
        accumulate(a - 2, 2)
        probs(a, 0)
        accumulate(a - 1, 3)
        probs(a + 1, 1)
        accumulate(a, 0)
        probs(a + 2, 2)
        accumulate(a + 1, 1)
        probs(a + 3, 3)
        return c

    lax.fori_loop(0, (nk - 4) // 4, fast_quad, 0)
    accumulate(nk - 2, 2)
    accumulate(nk - 1, 3)
    l_fast = accx_ref[:, V_HEAD_DIM:2 * LANES][:, 0:1]
    fast_ok = jnp.max(l_fast) < FAST_ROW_SUM_LIMIT

    @pl.when(fast_ok)
    def _():
        o_ref[0] = (accx_ref[:, 0:V_HEAD_DIM] / l_fast).astype(BF16)

    def scores(j, slot):
        s_ref[slot] = _dot(q, kT_ref[0, 0, j])

    def values(j, slot, alpha):
        acc_ref[...] = alpha * acc_ref[...] + _dot(p_ref[slot], v_chunk(j))

    def softmax(slot, m, l):
        s = s_ref[slot]
        m_new = jnp.maximum(m, jnp.max(s, axis=1, keepdims=True))
        alpha = jnp.exp2(m - m_new)
        p = jnp.exp2(s - m_new)
        p_ref[slot] = p.astype(BF16)
        return m_new, alpha * l + jnp.sum(p, axis=1, keepdims=True), alpha

    def pair(jj, carry, last):
        m, l, alpha = carry
        a = 2 * jj
        values(jnp.maximum(a - 1, 0), 1, alpha)
        scores(a + 1, 1)
        m, l, alpha = softmax(0, m, l)
        values(a, 0, alpha)
        if not last:
            scores(a + 2, 0)
        return softmax(1, m, l)

    @pl.when(jnp.logical_not(fast_ok))
    def _():
        s = meta_scores()
        m = jnp.max(s, axis=1, keepdims=True)
        p = jnp.exp2(s - m)
        l = jnp.sum(p, axis=1, keepdims=True)
        acc_ref[...] = _dot(p.astype(BF16), vm_ref[0])
        p_ref[1] = jnp.zeros(p_ref.shape[1:], BF16)
        scores(0, 0)
        carry = (m, l, jnp.ones_like(m))
        carry = lax.fori_loop(0, nk // 2 - 1, functools.partial(pair, last=False), carry)
        m, l, alpha = pair(nk // 2 - 1, carry, last=True)
        values(nk - 1, 1, alpha)
        o_ref[0] = (acc_ref[...] / l).astype(BF16)


def _attention(q, kT, v, kT_meta, v_meta, tq):
    B, H, S, _ = q.shape
    nk, tk = kT.shape[2], kT.shape[4]
    assert nk % 4 == 0, "the key loop is unrolled four chunks at a time"
    return pl.pallas_call(
        functools.partial(_attn_body, nk=nk, tk=tk),
        grid=(B, H, S // tq),
        in_specs=[
            pl.BlockSpec((1, 1, tq, QK_PAD), lambda b, h, i: (b, h, i, 0)),
            pl.BlockSpec((1, 1, nk, QK_PAD, tk), lambda b, h, i: (b, h, 0, 0, 0)),
            pl.BlockSpec((1, S, V_HEAD_DIM), lambda b, h, i: (b, 0, h)),
            pl.BlockSpec((1, QK_PAD, META_PAD), lambda b, h, i: (h, 0, 0)),
            pl.BlockSpec((1, META_PAD, V_HEAD_DIM), lambda b, h, i: (h, 0, 0)),
        ],
        out_specs=pl.BlockSpec((1, tq, V_HEAD_DIM), lambda b, h, i: (b, i, h)),
        out_shape=jax.ShapeDtypeStruct((B, S, H * V_HEAD_DIM), BF16),
        scratch_shapes=[
            pltpu.VMEM((2, tq, tk), F32), pltpu.VMEM((4, tq, tk), BF16), pltpu.VMEM((tq, V_HEAD_DIM), F32),
            pltpu.VMEM((tq, 2 * LANES), F32),
        ],
        compiler_params=_params(("parallel", "parallel", "parallel")), name="attention",
    )(q, kT, v, kT_meta, v_meta)


def _fnet_body(a_ref, b_ref, er_ref, ei_ref, pr_ref, pi_ref, wm_ref, abm_ref, o_ref, acc_ref, *, reps):
    ci = pl.program_id(2)

    @pl.when(ci == 0)
    def _():
        mt = _dot(wm_ref[...], abm_ref[...])
        acc_ref[...] = jnp.concatenate([mt] * reps, axis=1)

    lane = lax.broadcasted_iota(jnp.int32, pr_ref.shape, 1)
    sel = lane == ci
    pr = jnp.sum(jnp.where(sel, pr_ref[...], 0.0), axis=1, keepdims=True)
    pi = jnp.sum(jnp.where(sel, pi_ref[...], 0.0), axis=1, keepdims=True)
    er = er_ref[...]
    ei = ei_ref[...]
    wr = (pr * er - pi * ei).astype(BF16)
    wi = (pr * ei + pi * er).astype(BF16)
    acc_ref[...] += _dot(wr, a_ref[...]) + _dot(wi, b_ref[...])

    @pl.when(ci == pl.num_programs(2) - 1)
    def _():
        o_ref[...] = acc_ref[...].astype(BF16)


def _fnet(a, b, tabs, abm, tm, tkf, tn):
    S, NB = a.shape
    rows = tabs["er"].shape[0]
    nkt = S // tkf
    return pl.pallas_call(
        functools.partial(_fnet_body, reps=tn // FNET_DIM),
        grid=(NB // tn, rows // tm, nkt),
        in_specs=[
            pl.BlockSpec((tkf, tn), lambda n, r, c: (c, n)),
            pl.BlockSpec((tkf, tn), lambda n, r, c: (c, n)),
            pl.BlockSpec((tm, tkf), lambda n, r, c: (r, 0)),
            pl.BlockSpec((tm, tkf), lambda n, r, c: (r, 0)),
            pl.BlockSpec((tm, nkt), lambda n, r, c: (r, 0)),
            pl.BlockSpec((tm, nkt), lambda n, r, c: (r, 0)),
            pl.BlockSpec((tm, 2 * N_META), lambda n, r, c: (r, 0)),
            pl.BlockSpec((2 * N_META, FNET_DIM), lambda n, r, c: (0, 0)),
        ],
        out_specs=pl.BlockSpec((tm, tn), lambda n, r, c: (r, n)),
        out_shape=jax.ShapeDtypeStruct((rows, NB), BF16),
        scratch_shapes=[pltpu.VMEM((tm, tn), F32)],
        compiler_params=_params(("parallel", "parallel", "arbitrary")), name="fnet",
    )(a, b, tabs["er"], tabs["ei"], tabs["pr"], tabs["pi"], tabs["wm"], abm)


def _fnet_mirror_body(dk_ref, dkm_ref, x0_ref, j1_ref, j2_ref, m_ref):
    k = pl.num_programs(1) - 1 - pl.program_id(1)
    lower = jnp.where(k == 0, x0_ref[...], dkm_ref[...])
    m_ref[...] = (_dot(j1_ref[...], dk_ref[...]) + _dot(j2_ref[...], lower)).astype(BF16)


def _fnet_mirror(d, x0, tr, tn):
    half, NB = d.shape
    nt = half // tr
    a = jnp.arange(tr, dtype=jnp.int32)[:, None]
    c = jnp.arange(tr, dtype=jnp.int32)[None, :]
    j1 = (c == tr - N_META - a).astype(BF16)
    j2 = (c == 2 * tr - N_META - a).astype(BF16)
    x0_tile = jnp.zeros((tr, NB), BF16).at[tr - N_META:].set(x0)
    return pl.pallas_call(
        _fnet_mirror_body, grid=(NB // tn, nt),
        in_specs=[
            pl.BlockSpec((tr, tn), lambda n, i: (nt - 1 - i, n)),
            pl.BlockSpec((tr, tn), lambda n, i: (jnp.maximum(nt - 2 - i, 0), n)),
            pl.BlockSpec((tr, tn), lambda n, i: (0, n)),
            pl.BlockSpec((tr, tr), lambda n, i: (0, 0)),
            pl.BlockSpec((tr, tr), lambda n, i: (0, 0)),
        ],
        out_specs=pl.BlockSpec((tr, tn), lambda n, i: (i, n)),
        out_shape=jax.ShapeDtypeStruct((half, NB), BF16),
        compiler_params=_params(("parallel", "parallel")), name="fnet_mirror",
    )(d, d, x0_tile, j1, j2)


def _post_mix_body(x_ref, att_ref, flo_ref, fhi_ref, gate_ref, wo_ref, wfo_ref, wout_ref, nm_ref,
                   rhi_ref, rlo_ref, rb_ref, h1_ref, route_ref):
    a = _dot(att_ref[0], wo_ref[...])
    first_half = pl.program_id(1) < pl.num_programs(1) // 2
    f = jnp.where(first_half, flo_ref[...], fhi_ref[...])
    bb = _dot(f, wfo_ref[0])
    g = gate_ref[0].astype(F32)
    merged = g[:, :D_MODEL] * a + g[:, D_MODEL:] * bb
    h1 = x_ref[0] + _dot(merged.astype(BF16), wout_ref[...])
    h1_ref[0] = h1
    xn = _rms(h1, nm_ref[...])
    xh = xn.astype(BF16)

    xl = (xn - xh.astype(F32)).astype(BF16)
    logits = _dot(xh, rhi_ref[...]) + (_dot(xh, rlo_ref[...]) + _dot(xl, rhi_ref[...])) + rb_ref[...]
    lane = lax.broadcasted_iota(jnp.int32, logits.shape, 1)
    lanef = lane.astype(F32)
    far = float(2 * LANES)

    gl = jnp.where((lane >= N_EXPERTS) & (lane < N_EXPERTS + N_EXPERT_GROUPS), logits, NEG_BIG)
    gmax = jnp.max(gl, axis=1, keepdims=True)
    g_w = 1.0 / jnp.sum(jnp.exp(gl - gmax), axis=1, keepdims=True)
    gidx = jnp.min(jnp.where(gl == gmax, lanef, far), axis=1, keepdims=True) - float(N_EXPERTS)

    grp = (lane >> 3).astype(F32)
    el = jnp.where((lane < N_EXPERTS) & (grp == gidx), logits, NEG_BIG)
    m1 = jnp.max(el, axis=1, keepdims=True)
    i1 = jnp.min(jnp.where(el == m1, lanef, far), axis=1, keepdims=True)
    el2 = jnp.where(lanef == i1, NEG_BIG, el)
    m2 = jnp.max(el2, axis=1, keepdims=True)
    i2 = jnp.min(jnp.where(el2 == m2, lanef, far), axis=1, keepdims=True)
    r = jnp.exp(m2 - m1)
    w1 = g_w / (1.0 + r)
    w2 = g_w * r / (1.0 + r)

    first_is_lo = i1 < i2
    a_loc = jnp.minimum(i1, i2) - gidx * EXPERTS_PER_GROUP
    b_loc = jnp.maximum(i1, i2) - gidx * EXPERTS_PER_GROUP
    pair = a_loc * (2 * EXPERTS_PER_GROUP - 1 - a_loc) * 0.5 + (b_loc - a_loc - 1.0)
    bucket = gidx * PAIRS_PER_GROUP + pair
    w_lo = jnp.where(first_is_lo, w1, w2)
    w_hi = jnp.where(first_is_lo, w2, w1)
    route_ref[0] = jnp.where(lane == 0, bucket, jnp.where(lane == 1, w_lo, jnp.where(lane == 2, w_hi, 0.0)))


def _post_mix(x, att, f_lo, f_hi, gates, w, ts):
    B, S, _ = x.shape
    nh = S // ts // 2
    const = lambda shape: pl.BlockSpec(shape, lambda b, t: (0,) * len(shape))
    tok = lambda width: pl.BlockSpec((1, ts, width), lambda b, t: (b, t, 0))
    return pl.pallas_call(
        _post_mix_body, grid=(B, S // ts),
        in_specs=[
            tok(D_MODEL), tok(D_MODEL),
            pl.BlockSpec((ts, FNET_DIM), lambda b, t: (jnp.minimum(t, nh - 1), b)),
            pl.BlockSpec((ts, FNET_DIM), lambda b, t: (jnp.maximum(t - nh, 0), b)),
            tok(2 * D_MODEL),
            const((D_MODEL, D_MODEL)),
            pl.BlockSpec((1, FNET_DIM, D_MODEL), lambda b, t: (jnp.where(t < nh, 0, 1), 0, 0)),
            const((D_MODEL, D_MODEL)),
            const((1, D_MODEL)), const((D_MODEL, LANES)), const((D_MODEL, LANES)), const((1, LANES)),
        ],
        out_specs=(tok(D_MODEL), tok(LANES)),
        out_shape=(
            jax.ShapeDtypeStruct((B, S, D_MODEL), F32),
            jax.ShapeDtypeStruct((B, S, LANES), F32),
        ),
        compiler_params=_params(("parallel", "parallel")), name="post_mix",
    )(x, att, f_lo, f_hi, gates, w["w_o"], w["w_fo"], w["w_out"], w["norm_moe"], w["r_hi"], w["r_lo"], w["r_b"])


def _moe_body(lo_ref, hi_ref, nv_ref, src_ref, nxt_ref, rt_ref, h1_hbm,
              wgl_ref, wul_ref, wdl_ref, wgh_ref, wuh_ref, wdh_ref, nm_ref, nf_ref,
              out_hbm, xbuf, ybuf, gsem, ssem, *, tb):
    i = pl.program_id(0)
    slot = lax.rem(i, 2)
    other = 1 - slot
    cnt = nv_ref[i]
    prev = jnp.where(i > 0, nv_ref[jnp.maximum(i - 1, 0)], 0)

    def gather(idx_ref, s, r):
        return pltpu.make_async_copy(h1_hbm.at[idx_ref[0, 0, r]], xbuf.at[s, r], gsem.at[s])

    def scatter_wait(s, rows):
        full = pl.multiple_of(lax.shift_left(lax.shift_right_logical(rows, 3), 3), 8)

        @pl.when(full > 0)
        def _():
            pltpu.make_async_copy(ybuf.at[s, pl.ds(0, full)], out_hbm.at[pl.ds(0, full)], ssem.at[s]).wait()

        def one(r, c):
            pltpu.make_async_copy(ybuf.at[s, r], out_hbm.at[0], ssem.at[s]).wait()
            return c

        lax.fori_loop(full, rows, one, 0)

    @pl.when(i == 0)
    def _():
        for r in range(tb):
            gather(src_ref, 0, r).start()

    @pl.when(cnt > 0)
    def _():
        for r in range(tb):
            gather(src_ref, slot, r).wait()
        for r in range(tb):
            gather(nxt_ref, other, r).start()
        x = xbuf[slot]
        xb = _rms(x, nm_ref[...]).astype(BF16)
        rt = rt_ref[0].T

        def expert(wg_ref, wu_ref, wd_ref, wcol):
            gt = _dot(xb, wg_ref[0])
            up = _dot(xb, wu_ref[0])
            act = (gt / (1.0 + jnp.exp(-gt))) * up * wcol
            return _dot(act.astype(BF16), wd_ref[0])

        y = expert(wgl_ref, wul_ref, wdl_ref, rt[:, 0:1]) + expert(wgh_ref, wuh_ref, wdh_ref, rt[:, 1:2])
        ybuf[slot] = _rms(x + y, nf_ref[...])

        scatter_wait(other, prev)

        def issue(r):
            pltpu.make_async_copy(ybuf.at[slot, r], out_hbm.at[src_ref[0, 0, r]], ssem.at[slot]).start()

        def issue8(g, c):
            for u in range(8):
                issue(g * 8 + u)
            return c

        def issue1(r, c):
            issue(r)
            return c

        groups = lax.shift_right_logical(cnt, 3)
        lax.fori_loop(0, groups, issue8, 0)
        lax.fori_loop(groups * 8, cnt, issue1, 0)

    @pl.when((cnt == 0) & (prev > 0))
    def _():
        for r in range(tb):
            gather(src_ref, slot, r).wait()
        scatter_wait(other, prev)


def _moe(h1, route, w, tb):
    N = h1.shape[0]
    t = _route_tables(route, tb)
    nblk = t["src"].shape[0]
    idx_spec = lambda f: pl.BlockSpec((1, 1, tb), f, memory_space=pltpu.SMEM)
    wspec = lambda shape, which: pl.BlockSpec(
        (1,) + shape, (lambda i, lo, hi, nv: (lo[i], 0, 0)) if which == 0 else (lambda i, lo, hi, nv: (hi[i], 0, 0)))
    up_shape, down_shape = (D_MODEL, D_FF_EXPERT), (D_FF_EXPERT, D_MODEL)
    grid_spec = pltpu.PrefetchScalarGridSpec(
        num_scalar_prefetch=3, grid=(nblk,),
        in_specs=[
            idx_spec(lambda i, lo, hi, nv: (i, 0, 0)),
            idx_spec(lambda i, lo, hi, nv: (jnp.minimum(i + 1, nblk - 1), 0, 0)),
            pl.BlockSpec((1, LANES, tb), lambda i, lo, hi, nv: (i, 0, 0)),
            pl.BlockSpec(memory_space=pl.ANY),
            wspec(up_shape, 0), wspec(up_shape, 0), wspec(down_shape, 0),
            wspec(up_shape, 1), wspec(up_shape, 1), wspec(down_shape, 1),
            pl.BlockSpec((1, D_MODEL), lambda i, lo, hi, nv: (0, 0)),
            pl.BlockSpec((1, D_MODEL), lambda i, lo, hi, nv: (0, 0)),
        ],
        out_specs=pl.BlockSpec(memory_space=pl.ANY),
        scratch_shapes=[
            pltpu.VMEM((2, tb, D_MODEL), F32), pltpu.VMEM((2, tb, D_MODEL), F32),
            pltpu.SemaphoreType.DMA((2,)), pltpu.SemaphoreType.DMA((2,)),
        ],
    )
    return pl.pallas_call(
        functools.partial(_moe_body, tb=tb), grid_spec=grid_spec,
        out_shape=jax.ShapeDtypeStruct((N, D_MODEL), F32),
        compiler_params=_params(("arbitrary",)), name="moe",
    )(t["lo"], t["hi"], t["nvalid"], t["src"], t["src"], t["rt"], h1,
      w["w_eg"], w["w_eu"], w["w_ed"], w["w_eg"], w["w_eu"], w["w_ed"], w["norm_moe"], w["norm_final"])


def _route_tables(route, tb):
    N = route.shape[0]
    bucket = route[:, 0].astype(jnp.int32)
    order = jnp.argsort(bucket).astype(jnp.int32)
    ids = jnp.arange(N_BUCKETS, dtype=jnp.int32)
    counts = jnp.sum((bucket[:, None] == ids[None, :]).astype(jnp.int32), axis=0)
    starts = jnp.cumsum(counts) - counts
    padded = ((counts + tb - 1) // tb) * tb
    pend = jnp.cumsum(padded)
    poff = pend - padded
    nblk = N // tb + N_BUCKETS + 1
    blk_start = jnp.arange(nblk, dtype=jnp.int32) * tb
    active = blk_start < pend[-1]
    bb = jnp.minimum(jnp.sum((pend[None, :] <= blk_start[:, None]).astype(jnp.int32), axis=1), N_BUCKETS - 1)
    r0 = blk_start - poff[bb]
    nvalid = jnp.where(active, jnp.clip(counts[bb] - r0, 0, tb), 0).astype(jnp.int32)
    bb_w = jnp.where(active, bb, bb[jnp.maximum(jnp.sum(active.astype(jnp.int32)) - 1, 0)])
    rows = starts[bb][:, None] + r0[:, None] + jnp.arange(tb, dtype=jnp.int32)[None, :]
    src = order[jnp.clip(rows, 0, N - 1)]
    return {
        "lo": jnp.asarray(_BUCKET_LO)[bb_w], "hi": jnp.asarray(_BUCKET_HI)[bb_w], "nvalid": nvalid,
        "src": src[:, None, :],
        "rt": jnp.zeros((nblk, LANES, tb), F32).at[:, 0].set(route[:, 1][src]).at[:, 1].set(route[:, 2][src]),
    }


def _rope_tables(pos):
    inv_freq = 1.0 / (ROPE_THETA ** (jnp.arange(ROPE_HALF, dtype=F32) / ROPE_HALF))
    ang = pos.astype(F32)[:, None] * inv_freq[None, :]
    c, s = jnp.cos(ang), jnp.sin(ang)
    z = jnp.zeros((pos.shape[0], LANES - QK_ROPE_DIM), F32)
    return {
        "cos_nat": jnp.concatenate([c, c, z], axis=1), "sin_nat": jnp.concatenate([s, s, z], axis=1),
        "cosT": c.T, "sinT": s.T,
    }


def _unit(k, n):
    ang = k.astype(F32) * (-2.0 * math.pi / n)
    return jnp.cos(ang), jnp.sin(ang)


def _fnet_tables(S, tkf, freqs):
    L = S + N_META
    j = freqs.astype(jnp.int32)[:, None]
    er, ei = _unit((j * jnp.arange(tkf, dtype=jnp.int32)[None, :]) % L, L)
    col0 = jnp.arange(S // tkf, dtype=jnp.int32)[None, :] * tkf + N_META
    pr, pi = _unit((j * col0) % L, L)
    mr, mi = _unit((j * jnp.arange(N_META, dtype=jnp.int32)[None, :]) % L, L)
    return {"er": er, "ei": ei, "pr": pr, "pi": pi, "wm": jnp.concatenate([mr, mi], axis=1).astype(BF16)}


def _channel_dft(L):
    c = jnp.arange(FNET_GROUP_DIM, dtype=jnp.int32)
    cr, ci = _unit((c[:, None] * c[None, :]) % FNET_GROUP_DIM, FNET_GROUP_DIM)
    norm = 1.0 / math.sqrt(L * FNET_GROUP_DIM)
    eye = jnp.eye(N_FNET_GROUPS, dtype=F32)
    return (jnp.kron(eye, cr * norm).astype(BF16), jnp.kron(eye, -ci * norm).astype(BF16))


def _to_bf16_body(x_ref, o_ref):
    o_ref[...] = x_ref[...].astype(BF16)


def _to_bf16(w):
    e, r, c = w.shape
    spec = pl.BlockSpec((4, r, c), lambda i: (i, 0, 0))
    return pl.pallas_call(
        _to_bf16_body, grid=(e // 4,), in_specs=[spec], out_specs=spec,
        out_shape=jax.ShapeDtypeStruct(w.shape, BF16),
        compiler_params=_params(("parallel",)), name="to_bf16",
    )(w)


def _prep_weights(norm_attn, w_in, g_q, w_uq, g_kv, w_uk, w_uv, w_o_mla, w_fo, w_out, norm_moe,
                  w_rg, b_rg, w_re, b_re, w_eg, w_eu, w_ed, norm_final):
    s1 = Q_LORA_RANK
    s2 = s1 + KV_LORA_RANK
    s3 = s2 + QK_ROPE_DIM
    s4 = s3 + FNET_DIM
    w_in_p = jnp.concatenate(
        [w_in[:, :s1], w_in[:, s1:s2], w_in[:, s3:s4], w_in[:, s4:], w_in[:, s2:s3],
         jnp.zeros((D_MODEL, LANES - QK_ROPE_DIM), w_in.dtype)], axis=1)
    w_uq_p = jnp.pad(w_uq, ((0, 0), (0, 0), (0, 2 * LANES - QK_DIM)))
    r = jnp.concatenate([w_re, w_rg, jnp.zeros((D_MODEL, LANES - N_EXPERTS - N_EXPERT_GROUPS), F32)], axis=1)
    r_hi = r.astype(BF16)
    r_b = jnp.concatenate([b_re, b_rg, jnp.zeros((LANES - N_EXPERTS - N_EXPERT_GROUPS,), F32)])[None, :]
    return {
        "norm_attn": norm_attn[None, :], "w_in": w_in_p.astype(BF16),
        "g_q": g_q[None, :], "w_uq": w_uq_p.reshape(Q_LORA_RANK, N_HEADS * 2 * LANES).astype(BF16),
        "g_kv": g_kv[None, :], "w_ukT": w_uk.reshape(KV_LORA_RANK, N_HEADS * QK_NOPE_DIM).T.astype(BF16),
        "w_uv": w_uv.reshape(KV_LORA_RANK, N_HEADS * V_HEAD_DIM).astype(BF16),
        "w_o": w_o_mla.astype(BF16), "w_out": w_out.astype(BF16),
        "w_fo": jnp.stack([w_fo, w_fo[_CHANNEL_FLIP]]).astype(BF16),
        "norm_moe": norm_moe[None, :], "r_hi": r_hi, "r_lo": (r - r_hi.astype(F32)).astype(BF16), "r_b": r_b,
        "w_eg": _to_bf16(w_eg.reshape(N_EXPERTS, D_MODEL, D_FF_EXPERT)),
        "w_eu": _to_bf16(w_eu.reshape(N_EXPERTS, D_MODEL, D_FF_EXPERT)),
        "w_ed": _to_bf16(w_ed.reshape(N_EXPERTS, D_FF_EXPERT, D_MODEL)),
        "norm_final": norm_final[None, :],
    }


def _tiles(B, S):
    ts = min(512, S // 4)
    return {
        "ts": ts, "tq": min(2048, S), "tm": min(1024, S // 2), "tkf": min(512, S),
        "tn": min(2048, B * FNET_DIM), "tb": 128,
    }


def _trunk(x, meta, w):
    B, S, _ = x.shape
    L = S + N_META
    t = _tiles(B, S)
    wl = dict(w)
    wl["cc"], wl["sc"] = _channel_dft(L)

    meta_pad = jnp.zeros((1, META_PAD, D_MODEL), F32).at[0, :N_META].set(meta)
    _, kT_m, v_m, a_m, b_m, _ = _in_proj(meta_pad, wl, _rope_tables(jnp.arange(META_PAD)), META_PAD)
    q, kT, v, a, b, gates = _in_proj(x, wl, _rope_tables(jnp.arange(S) + N_META), t["ts"])

    v_meta = v_m[0].reshape(META_PAD, N_HEADS, V_HEAD_DIM).transpose(1, 0, 2)
    att = _attention(q, kT, v, kT_m[0, :, 0], v_meta, t["tq"])
    abm = jnp.concatenate([a_m[:N_META], b_m[:N_META]], axis=0)
    half = S // 2
    f_lo = _fnet(a, b, _fnet_tables(S, t["tkf"], jnp.arange(half) + N_META), abm, t["tm"], t["tkf"], t["tn"])
    f_meta = _fnet(a, b, _fnet_tables(S, t["tkf"], jnp.arange(N_META)), abm, N_META, t["tkf"], t["tn"])
    f_hi = _fnet_mirror(f_lo, f_meta, min(512, half), t["tn"])
    h1, route = _post_mix(x, att, f_lo, f_hi, gates, wl, t["ts"])
    y = _moe(h1.reshape(B * S, D_MODEL), route.reshape(B * S, LANES), wl, t["tb"])
    return y.reshape(B, S, D_MODEL)


def kernel(x_prompt, x_sample, meta_tokens, norm_attn, w_in, g_q, w_uq, g_kv, w_uk, w_uv, w_o_mla, w_fo, w_out,
           norm_moe, w_router_group, b_router_group, w_router_expert, b_router_expert, w_exp_gate, w_exp_up,
           w_exp_down, norm_final):
    w = _prep_weights(norm_attn[0], w_in[0], g_q[0], w_uq[0], g_kv[0], w_uk[0], w_uv[0], w_o_mla[0], w_fo[0],
                      w_out[0], norm_moe[0], w_router_group[0], b_router_group[0], w_router_expert[0],
                      b_router_expert[0], w_exp_gate[0], w_exp_up[0], w_exp_down[0], norm_final)
    return _trunk(x_prompt, meta_tokens, w), _trunk(x_sample, meta_tokens, w)
```

```python
import functools
import math

import jax
import jax.numpy as jnp
import numpy as np
from jax import lax
from jax.experimental import pallas as pl
from jax.experimental.pallas import tpu as pltpu

D_MODEL = 1024
N_META = 16
N_HEADS = 8
QK_NOPE_DIM = 128
QK_ROPE_DIM = 64
ROPE_HALF = QK_ROPE_DIM // 2
QK_DIM = QK_NOPE_DIM + QK_ROPE_DIM
QK_PAD = 256
V_HEAD_DIM = 128
Q_LORA_RANK = 384
KV_LORA_RANK = 256
ROPE_THETA = 10000.0
N_FNET_GROUPS = 4
FNET_GROUP_DIM = 128
FNET_DIM = N_FNET_GROUPS * FNET_GROUP_DIM
N_EXPERT_GROUPS = 4
EXPERTS_PER_GROUP = 8
N_EXPERTS = N_EXPERT_GROUPS * EXPERTS_PER_GROUP
D_FF_EXPERT = 256
EPS = 1e-6
PAIRS_PER_GROUP = EXPERTS_PER_GROUP * (EXPERTS_PER_GROUP - 1) // 2
N_BUCKETS = N_EXPERT_GROUPS * PAIRS_PER_GROUP
_CHANNEL_FLIP = np.array([g * FNET_GROUP_DIM + (FNET_GROUP_DIM - c) % FNET_GROUP_DIM
                          for g in range(N_FNET_GROUPS) for c in range(FNET_GROUP_DIM)], np.int32)
_PAIRS = [(a, b) for a in range(EXPERTS_PER_GROUP) for b in range(a + 1, EXPERTS_PER_GROUP)]
_BUCKET_LO = np.array([g * EXPERTS_PER_GROUP + a for g in range(N_EXPERT_GROUPS) for a, _ in _PAIRS], np.int32)
_BUCKET_HI = np.array([g * EXPERTS_PER_GROUP + b for g in range(N_EXPERT_GROUPS) for _, b in _PAIRS], np.int32)

LANES = 128
META_PAD = 128
VMEM_LIMIT_BYTES = 56 * 1024 * 1024
NEG_BIG = -1e30
FAST_ROW_SUM_LIMIT = 2.0 ** 24

_CQ0, _CQ1 = 0, Q_LORA_RANK
_CKV0, _CKV1 = _CQ1, _CQ1 + KV_LORA_RANK
_UF0, _UF1 = _CKV1, _CKV1 + FNET_DIM
_G0, _G1 = _UF1, _UF1 + 2 * D_MODEL
_KR0, _KR1 = _G1, _G1 + LANES
IN_DIM_PADDED = _KR1

F32 = jnp.float32
BF16 = jnp.bfloat16


def _dot(a, b):
    return jnp.dot(a, b, preferred_element_type=F32)


def _rms(x, g):
    ms = jnp.mean(x * x, axis=-1, keepdims=True)
    return x * lax.rsqrt(ms + EPS) * g


def _params(semantics):
    return pltpu.CompilerParams(dimension_semantics=semantics, vmem_limit_bytes=VMEM_LIMIT_BYTES)


def _in_proj_body(x_ref, na_ref, win_ref, gq_ref, wuq_ref, gkv_ref, wukT_ref, wuv_ref,
                  cosn_ref, sinn_ref, cosT_ref, sinT_ref, cc_ref, sc_ref,
                  q_ref, kT_ref, v_ref, a_ref, b_ref, gate_ref):
    x = x_ref[0]
    hb = _rms(x, na_ref[...]).astype(BF16)

    cqn = _rms(_dot(hb, win_ref[:, _CQ0:_CQ1]), gq_ref[...])
    qa = _dot(cqn.astype(BF16), wuq_ref[...])
    cn = cosn_ref[...]
    sn = sinn_ref[...]
    first_half = lax.broadcasted_iota(jnp.int32, cn.shape, 1) < ROPE_HALF
    scale = math.log2(math.e) / math.sqrt(QK_DIM)
    for h in range(N_HEADS):
        c0 = h * 2 * LANES
        qr = qa[:, c0 + LANES:c0 + 2 * LANES]
        rot = jnp.where(first_half, -pltpu.roll(qr, LANES - ROPE_HALF, 1), pltpu.roll(qr, ROPE_HALF, 1))
        q_ref[0, h, :, 0:QK_NOPE_DIM] = (qa[:, c0:c0 + LANES] * scale).astype(BF16)
        q_ref[0, h, :, QK_NOPE_DIM:QK_PAD] = ((qr * cn + rot * sn) * scale).astype(BF16)

    ckvn = _rms(_dot(hb, win_ref[:, _CKV0:_CKV1]), gkv_ref[...])
    knT = _dot(wukT_ref[...], ckvn.T.astype(BF16))
    krT = _dot(hb, win_ref[:, _KR0:_KR1]).T
    x1 = krT[0:ROPE_HALF]
    x2 = krT[ROPE_HALF:QK_ROPE_DIM]
    cT = cosT_ref[...]
    sT = sinT_ref[...]
    r1 = (x1 * cT - x2 * sT).astype(BF16)
    r2 = (x2 * cT + x1 * sT).astype(BF16)
    for h in range(N_HEADS):
        kT_ref[0, h, 0, 0:QK_NOPE_DIM, :] = knT[h * QK_NOPE_DIM:(h + 1) * QK_NOPE_DIM].astype(BF16)
        kT_ref[0, h, 0, QK_NOPE_DIM:QK_NOPE_DIM + ROPE_HALF, :] = r1
        kT_ref[0, h, 0, QK_NOPE_DIM + ROPE_HALF:QK_DIM, :] = r2
        kT_ref[0, h, 0, QK_DIM:QK_PAD, :] = jnp.zeros((QK_PAD - QK_DIM, r2.shape[1]), BF16)
    v_ref[0] = _dot(ckvn.astype(BF16), wuv_ref[...]).astype(BF16)

    ub = _dot(hb, win_ref[:, _UF0:_UF1]).astype(BF16)
    a_ref[...] = _dot(ub, cc_ref[...]).astype(BF16)
    b_ref[...] = _dot(ub, sc_ref[...]).astype(BF16)

    gl = _dot(hb, win_ref[:, _G0:_G1])
    gate_ref[0] = (1.0 / (1.0 + jnp.exp(-gl))).astype(BF16)


def _in_proj(x, w, rope, ts):
    B, S, _ = x.shape
    nt = S // ts
    const = lambda shape: pl.BlockSpec(shape, lambda b, t: (0,) * len(shape))
    out_shape = (
        jax.ShapeDtypeStruct((B, N_HEADS, S, QK_PAD), BF16),
        jax.ShapeDtypeStruct((B, N_HEADS, nt, QK_PAD, ts), BF16),
        jax.ShapeDtypeStruct((B, S, N_HEADS * V_HEAD_DIM), BF16),
        jax.ShapeDtypeStruct((S, B * FNET_DIM), BF16),
        jax.ShapeDtypeStruct((S, B * FNET_DIM), BF16),
        jax.ShapeDtypeStruct((B, S, 2 * D_MODEL), BF16),
    )
    in_specs = [
        pl.BlockSpec((1, ts, D_MODEL), lambda b, t: (b, t, 0)),
        const((1, D_MODEL)),
        const((D_MODEL, IN_DIM_PADDED)),
        const((1, Q_LORA_RANK)),
        const((Q_LORA_RANK, N_HEADS * 2 * LANES)),
        const((1, KV_LORA_RANK)),
        const((N_HEADS * QK_NOPE_DIM, KV_LORA_RANK)),
        const((KV_LORA_RANK, N_HEADS * V_HEAD_DIM)),
        pl.BlockSpec((ts, LANES), lambda b, t: (t, 0)),
        pl.BlockSpec((ts, LANES), lambda b, t: (t, 0)),
        pl.BlockSpec((ROPE_HALF, ts), lambda b, t: (0, t)),
        pl.BlockSpec((ROPE_HALF, ts), lambda b, t: (0, t)),
        const((FNET_DIM, FNET_DIM)),
        const((FNET_DIM, FNET_DIM)),
    ]
    out_specs = (
        pl.BlockSpec((1, N_HEADS, ts, QK_PAD), lambda b, t: (b, 0, t, 0)),
        pl.BlockSpec((1, N_HEADS, 1, QK_PAD, ts), lambda b, t: (b, 0, t, 0, 0)),
        pl.BlockSpec((1, ts, N_HEADS * V_HEAD_DIM), lambda b, t: (b, t, 0)),
        pl.BlockSpec((ts, FNET_DIM), lambda b, t: (t, b)),
        pl.BlockSpec((ts, FNET_DIM), lambda b, t: (t, b)),
        pl.BlockSpec((1, ts, 2 * D_MODEL), lambda b, t: (b, t, 0)),
    )
    return pl.pallas_call(
        _in_proj_body, grid=(B, nt), in_specs=in_specs, out_specs=out_specs, out_shape=out_shape,
        compiler_params=_params(("parallel", "parallel")), name="in_proj",
    )(x, w["norm_attn"], w["w_in"], w["g_q"], w["w_uq"], w["g_kv"], w["w_ukT"], w["w_uv"],
      rope["cos_nat"], rope["sin_nat"], rope["cosT"], rope["sinT"], w["cc"], w["sc"])


def _attn_body(q_ref, kT_ref, v_ref, kmT_ref, vm_ref, o_ref, s_ref, p_ref, acc_ref, accx_ref, *, nk, tk):
    q = q_ref[0, 0]
    col = lax.broadcasted_iota(jnp.int32, (q.shape[0], META_PAD), 1)

    def meta_scores():
        return jnp.where(col < N_META, _dot(q, kmT_ref[0]), NEG_BIG)

    def v_chunk(j):
        return v_ref[0, pl.ds(pl.multiple_of(j * tk, tk), tk), :]

    def ones_column(rows):
        return jnp.where(lax.broadcasted_iota(jnp.int32, (rows, LANES), 1) == 0, 1.0, 0.0).astype(BF16)

    ones_col = ones_column(tk)
    s_meta = meta_scores()
    s_ref[0] = _dot(q, kT_ref[0, 0, 0])
    m0 = jnp.maximum(jnp.max(s_meta, axis=1, keepdims=True), jnp.max(s_ref[0], axis=1, keepdims=True))
    vmx = jnp.concatenate([vm_ref[0], ones_column(META_PAD)], axis=1)
    accx_ref[...] = _dot(jnp.exp2(s_meta - m0).astype(BF16), vmx)
    p_ref[0] = jnp.exp2(s_ref[0] - m0).astype(BF16)

    def probs(j, slot):
        p_ref[slot] = jnp.exp2(_dot(q, kT_ref[0, 0, j]) - m0).astype(BF16)

    def accumulate(j, slot):
        accx_ref[...] += _dot(p_ref[slot], jnp.concatenate([v_chunk(j), ones_col], axis=1))

    probs(1, 1)
    accumulate(0, 0)
    probs(2, 2)
    accumulate(1, 1)
    probs(3, 3)

    def fast_quad(g, c):
        a = 4 * g + 4
        accumulate(a - 2, 2)
        probs(a, 0)
        accumulate(a - 1, 3)
        probs(a + 1, 1)
        accumulate(a, 0)
        probs(a + 2, 2)
        accumulate(a + 1, 1)
        probs(a + 3, 3)
        return c

    lax.fori_loop(0, (nk - 4) // 4, fast_quad, 0)
    accumulate(nk - 2, 2)
    accumulate(nk - 1, 3)
    l_fast = accx_ref[:, V_HEAD_DIM:2 * LANES][:, 0:1]
    fast_ok = jnp.max(l_fast) < FAST_ROW_SUM_LIMIT

    @pl.when(fast_ok)
    def _():
        o_ref[0] = (accx_ref[:, 0:V_HEAD_DIM] / l_fast).astype(BF16)

    def scores(j, slot):
        s_ref[slot] = _dot(q, kT_ref[0, 0, j])

    def values(j, slot, alpha):
        acc_ref[...] = alpha * acc_ref[...] + _dot(p_ref[slot], v_chunk(j))

    def softmax(slot, m, l):
        s = s_ref[slot]
        m_new = jnp.maximum(m, jnp.max(s, axis=1, keepdims=True))
        alpha = jnp.exp2(m - m_new)
        p = jnp.exp2(s - m_new)
        p_ref[slot] = p.astype(BF16)
        return m_new, alpha * l + jnp.sum(p, axis=1, keepdims=True), alpha

    def pair(jj, carry, last):
        m, l, alpha = carry
        a = 2 * jj
        values(jnp.maximum(a - 1, 0), 1, alpha)
        scores(a + 1, 1)
        m, l, alpha = softmax(0, m, l)
        values(a, 0, alpha)
        if not last:
            scores(a + 2, 0)
        return softmax(1, m, l)

    @pl.when(jnp.logical_not(fast_ok))
    def _():
        s = meta_scores()
        m = jnp.max(s, axis=1, keepdims=True)
        p = jnp.exp2(s - m)
        l = jnp.sum(p, axis=1, keepdims=True)
        acc_ref[...] = _dot(p.astype(BF16), vm_ref[0])
        p_ref[1] = jnp.zeros(p_ref.shape[1:], BF16)
        scores(0, 0)
        carry = (m, l, jnp.ones_like(m))
        carry = lax.fori_loop(0, nk // 2 - 1, functools.partial(pair, last=False), carry)
        m, l, alpha = pair(nk // 2 - 1, carry, last=True)
        values(nk - 1, 1, alpha)
        o_ref[0] = (acc_ref[...] / l).astype(BF16)


def _attention(q, kT, v, kT_meta, v_meta, tq):
    B, H, S, _ = q.shape
    nk, tk = kT.shape[2], kT.shape[4]
    assert nk % 4 == 0, "the key loop is unrolled four chunks at a time"
    return pl.pallas_call(
        functools.partial(_attn_body, nk=nk, tk=tk),
        grid=(B, H, S // tq),
        in_specs=[
            pl.BlockSpec((1, 1, tq, QK_PAD), lambda b, h, i: (b, h, i, 0)),
            pl.BlockSpec((1, 1, nk, QK_PAD, tk), lambda b, h, i: (b, h, 0, 0, 0)),
            pl.BlockSpec((1, S, V_HEAD_DIM), lambda b, h, i: (b, 0, h)),
            pl.BlockSpec((1, QK_PAD, META_PAD), lambda b, h, i: (h, 0, 0)),
            pl.BlockSpec((1, META_PAD, V_HEAD_DIM), lambda b, h, i: (h, 0, 0)),
        ],
        out_specs=pl.BlockSpec((1, tq, V_HEAD_DIM), lambda b, h, i: (b, i, h)),
        out_shape=jax.ShapeDtypeStruct((B, S, H * V_HEAD_DIM), BF16),
        scratch_shapes=[
            pltpu.VMEM((2, tq, tk), F32), pltpu.VMEM((4, tq, tk), BF16), pltpu.VMEM((tq, V_HEAD_DIM), F32),
            pltpu.VMEM((tq, 2 * LANES), F32),
        ],
        compiler_params=_params(("parallel", "parallel", "parallel")), name="attention",
    )(q, kT, v, kT_meta, v_meta)


def _fnet_body(a_ref, b_ref, er_ref, ei_ref, pr_ref, pi_ref, wm_ref, abm_ref, o_ref, acc_ref, *, reps):
    ci = pl.program_id(2)

    @pl.when(ci == 0)
    def _():
        mt = _dot(wm_ref[...], abm_ref[...])
        acc_ref[...] = jnp.concatenate([mt] * reps, axis=1)

    lane = lax.broadcasted_iota(jnp.int32, pr_ref.shape, 1)
    sel = lane == ci
    pr = jnp.sum(jnp.where(sel, pr_ref[...], 0.0), axis=1, keepdims=True)
    pi = jnp.sum(jnp.where(sel, pi_ref[...], 0.0), axis=1, keepdims=True)
    er = er_ref[...]
    ei = ei_ref[...]
    wr = (pr * er - pi * ei).astype(BF16)
    wi = (pr * ei + pi * er).astype(BF16)
    acc_ref[...] += _dot(wr, a_ref[...]) + _dot(wi, b_ref[...])

    @pl.when(ci == pl.num_programs(2) - 1)
    def _():
        o_ref[...] = acc_ref[...].astype(BF16)


def _fnet(a, b, tabs, abm, tm, tkf, tn):
    S, NB = a.shape
    rows = tabs["er"].shape[0]
    nkt = S // tkf
    return pl.pallas_call(
        functools.partial(_fnet_body, reps=tn // FNET_DIM),
        grid=(NB // tn, rows // tm, nkt),
        in_specs=[
            pl.BlockSpec((tkf, tn), lambda n, r, c: (c, n)),
            pl.BlockSpec((tkf, tn), lambda n, r, c: (c, n)),
            pl.BlockSpec((tm, tkf), lambda n, r, c: (r, 0)),
            pl.BlockSpec((tm, tkf), lambda n, r, c: (r, 0)),
            pl.BlockSpec((tm, nkt), lambda n, r, c: (r, 0)),
            pl.BlockSpec((tm, nkt), lambda n, r, c: (r, 0)),
            pl.BlockSpec((tm, 2 * N_META), lambda n, r, c: (r, 0)),
            pl.BlockSpec((2 * N_META, FNET_DIM), lambda n, r, c: (0, 0)),
        ],
        out_specs=pl.BlockSpec((tm, tn), lambda n, r, c: (r, n)),
        out_shape=jax.ShapeDtypeStruct((rows, NB), BF16),
        scratch_shapes=[pltpu.VMEM((tm, tn), F32)],
        compiler_params=_params(("parallel", "parallel", "arbitrary")), name="fnet",
    )(a, b, tabs["er"], tabs["ei"], tabs["pr"], tabs["pi"], tabs["wm"], abm)


def _fnet_mirror_body(dk_ref, dkm_ref, x0_ref, j1_ref, j2_ref, m_ref):
    k = pl.num_programs(1) - 1 - pl.program_id(1)
    lower = jnp.where(k == 0, x0_ref[...], dkm_ref[...])
    m_ref[...] = (_dot(j1_ref[...], dk_ref[...]) + _dot(j2_ref[...], lower)).astype(BF16)


def _fnet_mirror(d, x0, tr, tn):
    half, NB = d.shape
    nt = half // tr
    a = jnp.arange(tr, dtype=jnp.int32)[:, None]
    c = jnp.arange(tr, dtype=jnp.int32)[None, :]
    j1 = (c == tr - N_META - a).astype(BF16)
    j2 = (c == 2 * tr - N_META - a).astype(BF16)
    x0_tile = jnp.zeros((tr, NB), BF16).at[tr - N_META:].set(x0)
    return pl.pallas_call(
        _fnet_mirror_body, grid=(NB // tn, nt),
        in_specs=[
            pl.BlockSpec((tr, tn), lambda n, i: (nt - 1 - i, n)),
            pl.BlockSpec((tr, tn), lambda n, i: (jnp.maximum(nt - 2 - i, 0), n)),
            pl.BlockSpec((tr, tn), lambda n, i: (0, n)),
            pl.BlockSpec((tr, tr), lambda n, i: (0, 0)),
            pl.BlockSpec((tr, tr), lambda n, i: (0, 0)),
        ],
        out_specs=pl.BlockSpec((tr, tn), lambda n, i: (i, n)),
        out_shape=jax.ShapeDtypeStruct((half, NB), BF16),
        compiler_params=_params(("parallel", "parallel")), name="fnet_mirror",
    )(d, d, x0_tile, j1, j2)


def _post_mix_body(x_ref, att_ref, flo_ref, fhi_ref, gate_ref, wo_ref, wfo_ref, wout_ref, nm_ref,
                   rhi_ref, rlo_ref, rb_ref, h1_ref, route_ref):
    a = _dot(att_ref[0], wo_ref[...])
    first_half = pl.program_id(1) < pl.num_programs(1) // 2
    f = jnp.where(first_half, flo_ref[...], fhi_ref[...])
    bb = _dot(f, wfo_ref[0])
    g = gate_ref[0].astype(F32)
    merged = g[:, :D_MODEL] * a + g[:, D_MODEL:] * bb
    h1 = x_ref[0] + _dot(merged.astype(BF16), wout_ref[...])
    h1_ref[0] = h1
    xn = _rms(h1, nm_ref[...])
    xh = xn.astype(BF16)

    xl = (xn - xh.astype(F32)).astype(BF16)
    logits = _dot(xh, rhi_ref[...]) + (_dot(xh, rlo_ref[...]) + _dot(xl, rhi_ref[...])) + rb_ref[...]
    lane = lax.broadcasted_iota(jnp.int32, logits.shape, 1)
    lanef = lane.astype(F32)
    far = float(2 * LANES)

    gl = jnp.where((lane >= N_EXPERTS) & (lane < N_EXPERTS + N_EXPERT_GROUPS), logits, NEG_BIG)
    gmax = jnp.max(gl, axis=1, keepdims=True)
    g_w = 1.0 / jnp.sum(jnp.exp(gl - gmax), axis=1, keepdims=True)
    gidx = jnp.min(jnp.where(gl == gmax, lanef, far), axis=1, keepdims=True) - float(N_EXPERTS)

    grp = (lane >> 3).astype(F32)
    el = jnp.where((lane < N_EXPERTS) & (grp == gidx), logits, NEG_BIG)
    m1 = jnp.max(el, axis=1, keepdims=True)
    i1 = jnp.min(jnp.where(el == m1, lanef, far), axis=1, keepdims=True)
    el2 = jnp.where(lanef == i1, NEG_BIG, el)
    m2 = jnp.max(el2, axis=1, keepdims=True)
    i2 = jnp.min(jnp.where(el2 == m2, lanef, far), axis=1, keepdims=True)
    r = jnp.exp(m2 - m1)
    w1 = g_w / (1.0 + r)
    w2 = g_w * r / (1.0 + r)

    first_is_lo = i1 < i2
    a_loc = jnp.minimum(i1, i2) - gidx * EXPERTS_PER_GROUP
    b_loc = jnp.maximum(i1, i2) - gidx * EXPERTS_PER_GROUP
    pair = a_loc * (2 * EXPERTS_PER_GROUP - 1 - a_loc) * 0.5 + (b_loc - a_loc - 1.0)
    bucket = gidx * PAIRS_PER_GROUP + pair
    w_lo = jnp.where(first_is_lo, w1, w2)
    w_hi = jnp.where(first_is_lo, w2, w1)
    route_ref[0] = jnp.where(lane == 0, bucket, jnp.where(lane == 1, w_lo, jnp.where(lane == 2, w_hi, 0.0)))


def _post_mix(x, att, f_lo, f_hi, gates, w, ts):
    B, S, _ = x.shape
    nh = S // ts // 2
    const = lambda shape: pl.BlockSpec(shape, lambda b, t: (0,) * len(shape))
    tok = lambda width: pl.BlockSpec((1, ts, width), lambda b, t: (b, t, 0))
    return pl.pallas_call(
        _post_mix_body, grid=(B, S // ts),
        in_specs=[
            tok(D_MODEL), tok(D_MODEL),
            pl.BlockSpec((ts, FNET_DIM), lambda b, t: (jnp.minimum(t, nh - 1), b)),
            pl.BlockSpec((ts, FNET_DIM), lambda b, t: (jnp.maximum(t - nh, 0), b)),
            tok(2 * D_MODEL),
            const((D_MODEL, D_MODEL)),
            pl.BlockSpec((1, FNET_DIM, D_MODEL), lambda b, t: (jnp.where(t < nh, 0, 1), 0, 0)),
            const((D_MODEL, D_MODEL)),
            const((1, D_MODEL)), const((D_MODEL, LANES)), const((D_MODEL, LANES)), const((1, LANES)),
        ],
        out_specs=(tok(D_MODEL), tok(LANES)),
        out_shape=(
            jax.ShapeDtypeStruct((B, S, D_MODEL), F32),
            jax.ShapeDtypeStruct((B, S, LANES), F32),
        ),
        compiler_params=_params(("parallel", "parallel")), name="post_mix",
    )(x, att, f_lo, f_hi, gates, w["w_o"], w["w_fo"], w["w_out"], w["norm_moe"], w["r_hi"], w["r_lo"], w["r_b"])


def _moe_body(lo_ref, hi_ref, nv_ref, src_ref, nxt_ref, rt_ref, h1_hbm,
              wgl_ref, wul_ref, wdl_ref, wgh_ref, wuh_ref, wdh_ref, nm_ref, nf_ref,
              out_hbm, xbuf, ybuf, gsem, ssem, *, tb):
    i = pl.program_id(0)
    slot = lax.rem(i, 2)
    other = 1 - slot
    cnt = nv_ref[i]
    prev = jnp.where(i > 0, nv_ref[jnp.maximum(i - 1, 0)], 0)

    def gather(idx_ref, s, r):
        return pltpu.make_async_copy(h1_hbm.at[idx_ref[0, 0, r]], xbuf.at[s, r], gsem.at[s])

    def scatter_wait(s, rows):
        full = pl.multiple_of(lax.shift_left(lax.shift_right_logical(rows, 3), 3), 8)

        @pl.when(full > 0)
        def _():
            pltpu.make_async_copy(ybuf.at[s, pl.ds(0, full)], out_hbm.at[pl.ds(0, full)], ssem.at[s]).wait()

        def one(r, c):
            pltpu.make_async_copy(ybuf.at[s, r], out_hbm.at[0], ssem.at[s]).wait()
            return c

        lax.fori_loop(full, rows, one, 0)

    @pl.when(i == 0)
    def _():
        for r in range(tb):
            gather(src_ref, 0, r).start()

    @pl.when(cnt > 0)
    def _():
        for r in range(tb):
            gather(src_ref, slot, r).wait()
        for r in range(tb):
            gather(nxt_ref, other, r).start()
        x = xbuf[slot]
        xb = _rms(x, nm_ref[...]).astype(BF16)
        diag = lax.broadcasted_iota(jnp.int32, (tb, tb), 0) == lax.broadcasted_iota(jnp.int32, (tb, tb), 1)
        w_lo, w_hi = (jnp.sum(jnp.where(diag, rt_ref[0, k:k + 1, :], 0.0), axis=1, keepdims=True) for k in range(2))

        def expert(wg_ref, wu_ref, wd_ref, wcol):
            gt = _dot(xb, wg_ref[0])
            up = _dot(xb, wu_ref[0])
            act = (gt / (1.0 + jnp.exp(-gt))) * up * wcol
            return _dot(act.astype(BF16), wd_ref[0])

        y = expert(wgl_ref, wul_ref, wdl_ref, w_lo) + expert(wgh_ref, wuh_ref, wdh_ref, w_hi)
        ybuf[slot] = _rms(x + y, nf_ref[...])

        scatter_wait(other, prev)

        def issue(r):
            pltpu.make_async_copy(ybuf.at[slot, r], out_hbm.at[src_ref[0, 0, r]], ssem.at[slot]).start()

        def issue8(g, c):
            for u in range(8):
                issue(g * 8 + u)
            return c

        def issue1(r, c):
            issue(r)
            return c

        groups = lax.shift_right_logical(cnt, 3)
        lax.fori_loop(0, groups, issue8, 0)
        lax.fori_loop(groups * 8, cnt, issue1, 0)

    @pl.when((cnt == 0) & (prev > 0))
    def _():
        for r in range(tb):
            gather(src_ref, slot, r).wait()
        scatter_wait(other, prev)


def _moe(h1, route, w, tb):
    N = h1.shape[0]
    t = _route_tables(route, tb)
    nblk = t["src"].shape[0]
    idx_spec = lambda f: pl.BlockSpec((1, 1, tb), f, memory_space=pltpu.SMEM)
    wspec = lambda shape, which: pl.BlockSpec(
        (1,) + shape, (lambda i, lo, hi, nv: (lo[i], 0, 0)) if which == 0 else (lambda i, lo, hi, nv: (hi[i], 0, 0)))
    up_shape, down_shape = (D_MODEL, D_FF_EXPERT), (D_FF_EXPERT, D_MODEL)
    grid_spec = pltpu.PrefetchScalarGridSpec(
        num_scalar_prefetch=3, grid=(nblk,),
        in_specs=[
            idx_spec(lambda i, lo, hi, nv: (i, 0, 0)),
            idx_spec(lambda i, lo, hi, nv: (jnp.minimum(i + 1, nblk - 1), 0, 0)),
            pl.BlockSpec((1, 2, tb), lambda i, lo, hi, nv: (i, 0, 0)),
            pl.BlockSpec(memory_space=pl.ANY),
            wspec(up_shape, 0), wspec(up_shape, 0), wspec(down_shape, 0),
            wspec(up_shape, 1), wspec(up_shape, 1), wspec(down_shape, 1),
            pl.BlockSpec((1, D_MODEL), lambda i, lo, hi, nv: (0, 0)),
            pl.BlockSpec((1, D_MODEL), lambda i, lo, hi, nv: (0, 0)),
        ],
        out_specs=pl.BlockSpec(memory_space=pl.ANY),
        scratch_shapes=[
            pltpu.VMEM((2, tb, D_MODEL), F32), pltpu.VMEM((2, tb, D_MODEL), F32),
            pltpu.SemaphoreType.DMA((2,)), pltpu.SemaphoreType.DMA((2,)),
        ],
    )
    return pl.pallas_call(
        functools.partial(_moe_body, tb=tb), grid_spec=grid_spec,
        out_shape=jax.ShapeDtypeStruct((N, D_MODEL), F32),
        compiler_params=_params(("arbitrary",)), name="moe",
    )(t["lo"], t["hi"], t["nvalid"], t["src"], t["src"], t["rt"], h1,
      w["w_eg"], w["w_eu"], w["w_ed"], w["w_eg"], w["w_eu"], w["w_ed"], w["norm_moe"], w["norm_final"])


def _route_tables(route, tb):
    N = route.shape[0]
    bucket = route[:, 0].astype(jnp.int32)
    order = jnp.argsort(bucket).astype(jnp.int32)
    ids = jnp.arange(N_BUCKETS, dtype=jnp.int32)
    counts = jnp.sum((bucket[:, None] == ids[None, :]).astype(jnp.int32), axis=0)
    starts = jnp.cumsum(counts) - counts
    padded = ((counts + tb - 1) // tb) * tb
    pend = jnp.cumsum(padded)
    poff = pend - padded
    nblk = N // tb + N_BUCKETS + 1
    blk_start = jnp.arange(nblk, dtype=jnp.int32) * tb
    active = blk_start < pend[-1]
    bb = jnp.minimum(jnp.sum((pend[None, :] <= blk_start[:, None]).astype(jnp.int32), axis=1), N_BUCKETS - 1)
    r0 = blk_start - poff[bb]
    nvalid = jnp.where(active, jnp.clip(counts[bb] - r0, 0, tb), 0).astype(jnp.int32)
    bb_w = jnp.where(active, bb, bb[jnp.maximum(jnp.sum(active.astype(jnp.int32)) - 1, 0)])
    rows = starts[bb][:, None] + r0[:, None] + jnp.arange(tb, dtype=jnp.int32)[None, :]
    src = order[jnp.clip(rows, 0, N - 1)]
    return {
        "lo": jnp.asarray(_BUCKET_LO)[bb_w], "hi": jnp.asarray(_BUCKET_HI)[bb_w], "nvalid": nvalid,
        "src": src[:, None, :],
        "rt": jnp.stack([route[:, 1][src], route[:, 2][src]], axis=1),
    }


def _rope_tables(pos):
    inv_freq = 1.0 / (ROPE_THETA ** (jnp.arange(ROPE_HALF, dtype=F32) / ROPE_HALF))
    ang = pos.astype(F32)[:, None] * inv_freq[None, :]
    c, s = jnp.cos(ang), jnp.sin(ang)
    z = jnp.zeros((pos.shape[0], LANES - QK_ROPE_DIM), F32)
    return {
        "cos_nat": jnp.concatenate([c, c, z], axis=1), "sin_nat": jnp.concatenate([s, s, z], axis=1),
        "cosT": c.T, "sinT": s.T,
    }


def _unit(k, n):
    ang = k.astype(F32) * (-2.0 * math.pi / n)
    return jnp.cos(ang), jnp.sin(ang)


def _fnet_tables(S, tkf, freqs):
    L = S + N_META
    j = freqs.astype(jnp.int32)[:, None]
    er, ei = _unit((j * jnp.arange(tkf, dtype=jnp.int32)[None, :]) % L, L)
    col0 = jnp.arange(S // tkf, dtype=jnp.int32)[None, :] * tkf + N_META
    pr, pi = _unit((j * col0) % L, L)
    mr, mi = _unit((j * jnp.arange(N_META, dtype=jnp.int32)[None, :]) % L, L)
    return {"er": er, "ei": ei, "pr": pr, "pi": pi, "wm": jnp.concatenate([mr, mi], axis=1).astype(BF16)}


def _channel_dft(L):
    c = jnp.arange(FNET_GROUP_DIM, dtype=jnp.int32)
    cr, ci = _unit((c[:, None] * c[None, :]) % FNET_GROUP_DIM, FNET_GROUP_DIM)
    norm = 1.0 / math.sqrt(L * FNET_GROUP_DIM)
    eye = jnp.eye(N_FNET_GROUPS, dtype=F32)
    return (jnp.kron(eye, cr * norm).astype(BF16), jnp.kron(eye, -ci * norm).astype(BF16))


def _to_bf16_body(x_ref, o_ref):
    o_ref[...] = x_ref[...].astype(BF16)


def _to_bf16(w):
    e, r, c = w.shape
    spec = pl.BlockSpec((4, r, c), lambda i: (i, 0, 0))
    return pl.pallas_call(
        _to_bf16_body, grid=(e // 4,), in_specs=[spec], out_specs=spec,
        out_shape=jax.ShapeDtypeStruct(w.shape, BF16),
        compiler_params=_params(("parallel",)), name="to_bf16",
    )(w)


def _prep_weights(norm_attn, w_in, g_q, w_uq, g_kv, w_uk, w_uv, w_o_mla, w_fo, w_out, norm_moe,
                  w_rg, b_rg, w_re, b_re, w_eg, w_eu, w_ed, norm_final):
    s1 = Q_LORA_RANK
    s2 = s1 + KV_LORA_RANK
    s3 = s2 + QK_ROPE_DIM
    s4 = s3 + FNET_DIM
    w_in_p = jnp.concatenate(
        [w_in[:, :s1], w_in[:, s1:s2], w_in[:, s3:s4], w_in[:, s4:], w_in[:, s2:s3],
         jnp.zeros((D_MODEL, LANES - QK_ROPE_DIM), w_in.dtype)], axis=1)
    w_uq_p = jnp.pad(w_uq, ((0, 0), (0, 0), (0, 2 * LANES - QK_DIM)))
    r = jnp.concatenate([w_re, w_rg, jnp.zeros((D_MODEL, LANES - N_EXPERTS - N_EXPERT_GROUPS), F32)], axis=1)
    r_hi = r.astype(BF16)
    r_b = jnp.concatenate([b_re, b_rg, jnp.zeros((LANES - N_EXPERTS - N_EXPERT_GROUPS,), F32)])[None, :]
    return {
        "norm_attn": norm_attn[None, :], "w_in": w_in_p.astype(BF16),
        "g_q": g_q[None, :], "w_uq": w_uq_p.reshape(Q_LORA_RANK, N_HEADS * 2 * LANES).astype(BF16),
        "g_kv": g_kv[None, :], "w_ukT": w_uk.reshape(KV_LORA_RANK, N_HEADS * QK_NOPE_DIM).T.astype(BF16),
        "w_uv": w_uv.reshape(KV_LORA_RANK, N_HEADS * V_HEAD_DIM).astype(BF16),
        "w_o": w_o_mla.astype(BF16), "w_out": w_out.astype(BF16),
        "w_fo": jnp.stack([w_fo, w_fo[_CHANNEL_FLIP]]).astype(BF16),
        "norm_moe": norm_moe[None, :], "r_hi": r_hi, "r_lo": (r - r_hi.astype(F32)).astype(BF16), "r_b": r_b,
        "w_eg": _to_bf16(w_eg.reshape(N_EXPERTS, D_MODEL, D_FF_EXPERT)),
        "w_eu": _to_bf16(w_eu.reshape(N_EXPERTS, D_MODEL, D_FF_EXPERT)),
        "w_ed": _to_bf16(w_ed.reshape(N_EXPERTS, D_FF_EXPERT, D_MODEL)),
        "norm_final": norm_final[None, :],
    }


def _tiles(B, S):
    ts = min(512, S // 4)
    return {
        "ts": ts, "tq": min(2048, S), "tm": min(1024, S // 2), "tkf": min(512, S),
        "tn": min(2048, B * FNET_DIM), "tb": 128,
    }


def _trunk(x, meta, w):
    B, S, _ = x.shape
    L = S + N_META
    t = _tiles(B, S)
    wl = dict(w)
    wl["cc"], wl["sc"] = _channel_dft(L)

    meta_pad = jnp.zeros((1, META_PAD, D_MODEL), F32).at[0, :N_META].set(meta)
    _, kT_m, v_m, a_m, b_m, _ = _in_proj(meta_pad, wl, _rope_tables(jnp.arange(META_PAD)), META_PAD)
    q, kT, v, a, b, gates = _in_proj(x, wl, _rope_tables(jnp.arange(S) + N_META), t["ts"])

    v_meta = v_m[0].reshape(META_PAD, N_HEADS, V_HEAD_DIM).transpose(1, 0, 2)
    att = _attention(q, kT, v, kT_m[0, :, 0], v_meta, t["tq"])
    abm = jnp.concatenate([a_m[:N_META], b_m[:N_META]], axis=0)
    half = S // 2
    f_lo = _fnet(a, b, _fnet_tables(S, t["tkf"], jnp.arange(half) + N_META), abm, t["tm"], t["tkf"], t["tn"])
    f_meta = _fnet(a, b, _fnet_tables(S, t["tkf"], jnp.arange(N_META)), abm, N_META, t["tkf"], t["tn"])
    f_hi = _fnet_mirror(f_lo, f_meta, min(512, half), t["tn"])
    h1, route = _post_mix(x, att, f_lo, f_hi, gates, wl, t["ts"])
    y = _moe(h1.reshape(B * S, D_MODEL), route.reshape(B * S, LANES), wl, t["tb"])
    return y.reshape(B, S, D_MODEL)


def kernel(x_prompt, x_sample, meta_tokens, norm_attn, w_in, g_q, w_uq, g_kv, w_uk, w_uv, w_o_mla, w_fo, w_out,
           norm_moe, w_router_group, b_router_group, w_router_expert, b_router_expert, w_exp_gate, w_exp_up,
           w_exp_down, norm_final):
    w = _prep_weights(norm_attn[0], w_in[0], g_q[0], w_uq[0], g_kv[0], w_uk[0], w_uv[0], w_o_mla[0], w_fo[0],
                      w_out[0], norm_moe[0], w_router_group[0], b_router_group[0], w_router_expert[0],
                      b_router_expert[0], w_exp_gate[0], w_exp_up[0], w_exp_down[0], norm_final)
    return _trunk(x_prompt, meta_tokens, w), _trunk(x_sample, meta_tokens, w)
```

```python
import functools
import math

import jax
import jax.numpy as jnp
import numpy as np
from jax import lax
from jax.experimental import pallas as pl
from jax.experimental.pallas import tpu as pltpu

D_MODEL = 1024
N_META = 16
N_HEADS = 8
QK_NOPE_DIM = 128
QK_ROPE_DIM = 64
ROPE_HALF = QK_ROPE_DIM // 2
QK_DIM = QK_NOPE_DIM + QK_ROPE_DIM
QK_PAD = 256
V_HEAD_DIM = 128
Q_LORA_RANK = 384
KV_LORA_RANK = 256
ROPE_THETA = 10000.0
N_FNET_GROUPS = 4
FNET_GROUP_DIM = 128
FNET_DIM = N_FNET_GROUPS * FNET_GROUP_DIM
N_EXPERT_GROUPS = 4
EXPERTS_PER_GROUP = 8
N_EXPERTS = N_EXPERT_GROUPS * EXPERTS_PER_GROUP
D_FF_EXPERT = 256
EPS = 1e-6
PAIRS_PER_GROUP = EXPERTS_PER_GROUP * (EXPERTS_PER_GROUP - 1) // 2
N_BUCKETS = N_EXPERT_GROUPS * PAIRS_PER_GROUP
_CHANNEL_FLIP = np.array([g * FNET_GROUP_DIM + (FNET_GROUP_DIM - c) % FNET_GROUP_DIM
                          for g in range(N_FNET_GROUPS) for c in range(FNET_GROUP_DIM)], np.int32)
_PAIRS = [(a, b) for a in range(EXPERTS_PER_GROUP) for b in range(a + 1, EXPERTS_PER_GROUP)]
_BUCKET_LO = np.array([g * EXPERTS_PER_GROUP + a for g in range(N_EXPERT_GROUPS) for a, _ in _PAIRS], np.int32)
_BUCKET_HI = np.array([g * EXPERTS_PER_GROUP + b for g in range(N_EXPERT_GROUPS) for _, b in _PAIRS], np.int32)

LANES = 128
META_PAD = 128
VMEM_LIMIT_BYTES = 56 * 1024 * 1024
NEG_BIG = -1e30
FAST_ROW_SUM_LIMIT = 2.0 ** 40

_CQ0, _CQ1 = 0, Q_LORA_RANK
_CKV0, _CKV1 = _CQ1, _CQ1 + KV_LORA_RANK
_UF0, _UF1 = _CKV1, _CKV1 + FNET_DIM
_G0, _G1 = _UF1, _UF1 + 2 * D_MODEL
_KR0, _KR1 = _G1, _G1 + LANES
IN_DIM_PADDED = _KR1

F32 = jnp.float32
BF16 = jnp.bfloat16


def _dot(a, b):
    return jnp.dot(a, b, preferred_element_type=F32)


def _rms(x, g):
    ms = jnp.mean(x * x, axis=-1, keepdims=True)
    return x * lax.rsqrt(ms + EPS) * g


def _params(semantics):
    return pltpu.CompilerParams(dimension_semantics=semantics, vmem_limit_bytes=VMEM_LIMIT_BYTES)


def _in_proj_body(x_ref, na_ref, win_ref, gq_ref, wuq_ref, gkv_ref, wukT_ref, wuv_ref,
                  cosn_ref, sinn_ref, cosT_ref, sinT_ref, cc_ref, sc_ref,
                  q_ref, kT_ref, v_ref, a_ref, b_ref, gate_ref):
    x = x_ref[0]
    hb = _rms(x, na_ref[...]).astype(BF16)

    cqn = _rms(_dot(hb, win_ref[:, _CQ0:_CQ1]), gq_ref[...])
    qa = _dot(cqn.astype(BF16), wuq_ref[...])
    cn = cosn_ref[...]
    sn = sinn_ref[...]
    first_half = lax.broadcasted_iota(jnp.int32, cn.shape, 1) < ROPE_HALF
    scale = math.log2(math.e) / math.sqrt(QK_DIM)
    for h in range(N_HEADS):
        c0 = h * 2 * LANES
        qr = qa[:, c0 + LANES:c0 + 2 * LANES]
        rot = jnp.where(first_half, -pltpu.roll(qr, LANES - ROPE_HALF, 1), pltpu.roll(qr, ROPE_HALF, 1))
        q_ref[0, h, :, 0:QK_NOPE_DIM] = (qa[:, c0:c0 + LANES] * scale).astype(BF16)
        q_ref[0, h, :, QK_NOPE_DIM:QK_PAD] = ((qr * cn + rot * sn) * scale).astype(BF16)

    ckvn = _rms(_dot(hb, win_ref[:, _CKV0:_CKV1]), gkv_ref[...])
    knT = _dot(wukT_ref[...], ckvn.T.astype(BF16))
    krT = _dot(hb, win_ref[:, _KR0:_KR1]).T
    x1 = krT[0:ROPE_HALF]
    x2 = krT[ROPE_HALF:QK_ROPE_DIM]
    cT = cosT_ref[...]
    sT = sinT_ref[...]
    r1 = (x1 * cT - x2 * sT).astype(BF16)
    r2 = (x2 * cT + x1 * sT).astype(BF16)
    for h in range(N_HEADS):
        kT_ref[0, h, 0, 0:QK_NOPE_DIM, :] = knT[h * QK_NOPE_DIM:(h + 1) * QK_NOPE_DIM].astype(BF16)
        kT_ref[0, h, 0, QK_NOPE_DIM:QK_NOPE_DIM + ROPE_HALF, :] = r1
        kT_ref[0, h, 0, QK_NOPE_DIM + ROPE_HALF:QK_DIM, :] = r2
        kT_ref[0, h, 0, QK_DIM:QK_PAD, :] = jnp.zeros((QK_PAD - QK_DIM, r2.shape[1]), BF16)
    v_ref[0] = _dot(ckvn.astype(BF16), wuv_ref[...]).astype(BF16)

    ub = _dot(hb, win_ref[:, _UF0:_UF1]).astype(BF16)
    a_ref[...] = _dot(ub, cc_ref[...]).astype(BF16)
    b_ref[...] = _dot(ub, sc_ref[...]).astype(BF16)

    gl = _dot(hb, win_ref[:, _G0:_G1])
    gate_ref[0] = (1.0 / (1.0 + jnp.exp(-gl))).astype(BF16)


def _in_proj(x, w, rope, ts):
    B, S, _ = x.shape
    nt = S // ts
    const = lambda shape: pl.BlockSpec(shape, lambda b, t: (0,) * len(shape))
    out_shape = (
        jax.ShapeDtypeStruct((B, N_HEADS, S, QK_PAD), BF16),
        jax.ShapeDtypeStruct((B, N_HEADS, nt, QK_PAD, ts), BF16),
        jax.ShapeDtypeStruct((B, S, N_HEADS * V_HEAD_DIM), BF16),
        jax.ShapeDtypeStruct((S, B * FNET_DIM), BF16),
        jax.ShapeDtypeStruct((S, B * FNET_DIM), BF16),
        jax.ShapeDtypeStruct((B, S, 2 * D_MODEL), BF16),
    )
    in_specs = [
        pl.BlockSpec((1, ts, D_MODEL), lambda b, t: (b, t, 0)),
        const((1, D_MODEL)),
        const((D_MODEL, IN_DIM_PADDED)),
        const((1, Q_LORA_RANK)),
        const((Q_LORA_RANK, N_HEADS * 2 * LANES)),
        const((1, KV_LORA_RANK)),
        const((N_HEADS * QK_NOPE_DIM, KV_LORA_RANK)),
        const((KV_LORA_RANK, N_HEADS * V_HEAD_DIM)),
        pl.BlockSpec((ts, LANES), lambda b, t: (t, 0)),
        pl.BlockSpec((ts, LANES), lambda b, t: (t, 0)),
        pl.BlockSpec((ROPE_HALF, ts), lambda b, t: (0, t)),
        pl.BlockSpec((ROPE_HALF, ts), lambda b, t: (0, t)),
        const((FNET_DIM, FNET_DIM)),
        const((FNET_DIM, FNET_DIM)),
    ]
    out_specs = (
        pl.BlockSpec((1, N_HEADS, ts, QK_PAD), lambda b, t: (b, 0, t, 0)),
        pl.BlockSpec((1, N_HEADS, 1, QK_PAD, ts), lambda b, t: (b, 0, t, 0, 0)),
        pl.BlockSpec((1, ts, N_HEADS * V_HEAD_DIM), lambda b, t: (b, t, 0)),
        pl.BlockSpec((ts, FNET_DIM), lambda b, t: (t, b)),
        pl.BlockSpec((ts, FNET_DIM), lambda b, t: (t, b)),
        pl.BlockSpec((1, ts, 2 * D_MODEL), lambda b, t: (b, t, 0)),
    )
    return pl.pallas_call(
        _in_proj_body, grid=(B, nt), in_specs=in_specs, out_specs=out_specs, out_shape=out_shape,
        compiler_params=_params(("parallel", "parallel")), name="in_proj",
    )(x, w["norm_attn"], w["w_in"], w["g_q"], w["w_uq"], w["g_kv"], w["w_ukT"], w["w_uv"],
      rope["cos_nat"], rope["sin_nat"], rope["cosT"], rope["sinT"], w["cc"], w["sc"])


def _attn_body(q_ref, kT_ref, v_ref, kmT_ref, vm_ref, o_ref, s_ref, p_ref, acc_ref, accx_ref, *, nk, tk):
    q = q_ref[0, 0]
    col = lax.broadcasted_iota(jnp.int32, (q.shape[0], META_PAD), 1)

    def meta_scores():
        return jnp.where(col < N_META, _dot(q, kmT_ref[0]), NEG_BIG)

    def v_chunk(j):
        return v_ref[0, pl.ds(pl.multiple_of(j * tk, tk), tk), :]

    def ones_column(rows):
        return jnp.where(lax.broadcasted_iota(jnp.int32, (rows, LANES), 1) == 0, 1.0, 0.0).astype(BF16)

    ones_col = ones_column(tk)
    s_meta = meta_scores()
    m0 = jnp.max(s_meta, axis=1, keepdims=True)
    vmx = jnp.concatenate([vm_ref[0], ones_column(META_PAD)], axis=1)
    accx_ref[...] = _dot(jnp.exp2(s_meta - m0).astype(BF16), vmx)

    def probs(j, slot):
        p_ref[slot] = jnp.exp2(_dot(q, kT_ref[0, 0, j]) - m0).astype(BF16)

    def accumulate(j, slot):
        accx_ref[...] += _dot(p_ref[slot], jnp.concatenate([v_chunk(j), ones_col], axis=1))

    probs(0, 0)
    probs(1, 1)
    accumulate(0, 0)
    probs(2, 2)
    accumulate(1, 1)
    probs(3, 3)

    def fast_quad(g, c):
        a = 4 * g + 4
        accumulate(a - 2, 2)
        probs(a, 0)
        accumulate(a - 1, 3)
        probs(a + 1, 1)
        accumulate(a, 0)
        probs(a + 2, 2)
        accumulate(a + 1, 1)
        probs(a + 3, 3)
        return c

    lax.fori_loop(0, (nk - 4) // 4, fast_quad, 0)
    accumulate(nk - 2, 2)
    accumulate(nk - 1, 3)
    l_fast = accx_ref[:, V_HEAD_DIM:2 * LANES][:, 0:1]
    fast_ok = jnp.max(l_fast) < FAST_ROW_SUM_LIMIT

    @pl.when(fast_ok)
    def _():
        o_ref[0] = (accx_ref[:, 0:V_HEAD_DIM] / l_fast).astype(BF16)

    def scores(j, slot):
        s_ref[slot] = _dot(q, kT_ref[0, 0, j])

    def values(j, slot, alpha):
        acc_ref[...] = alpha * acc_ref[...] + _dot(p_ref[slot], v_chunk(j))

    def softmax(slot, m, l):
        s = s_ref[slot]
        m_new = jnp.maximum(m, jnp.max(s, axis=1, keepdims=True))
        alpha = jnp.exp2(m - m_new)
        p = jnp.exp2(s - m_new)
        p_ref[slot] = p.astype(BF16)
        return m_new, alpha * l + jnp.sum(p, axis=1, keepdims=True), alpha

    def pair(jj, carry, last):
        m, l, alpha = carry
        a = 2 * jj
        values(jnp.maximum(a - 1, 0), 1, alpha)
        scores(a + 1, 1)
        m, l, alpha = softmax(0, m, l)
        values(a, 0, alpha)
        if not last:
            scores(a + 2, 0)
        return softmax(1, m, l)

    @pl.when(jnp.logical_not(fast_ok))
    def _():
        s = meta_scores()
        m = jnp.max(s, axis=1, keepdims=True)
        p = jnp.exp2(s - m)
        l = jnp.sum(p, axis=1, keepdims=True)
        acc_ref[...] = _dot(p.astype(BF16), vm_ref[0])
        p_ref[1] = jnp.zeros(p_ref.shape[1:], BF16)
        scores(0, 0)
        carry = (m, l, jnp.ones_like(m))
        carry = lax.fori_loop(0, nk // 2 - 1, functools.partial(pair, last=False), carry)
        m, l, alpha = pair(nk // 2 - 1, carry, last=True)
        values(nk - 1, 1, alpha)
        o_ref[0] = (acc_ref[...] / l).astype(BF16)


def _attention(q, kT, v, kT_meta, v_meta, tq):
    B, H, S, _ = q.shape
    nk, tk = kT.shape[2], kT.shape[4]
    assert nk % 4 == 0, "the key loop is unrolled four chunks at a time"
    return pl.pallas_call(
        functools.partial(_attn_body, nk=nk, tk=tk),
        grid=(B, H, S // tq),
        in_specs=[
            pl.BlockSpec((1, 1, tq, QK_PAD), lambda b, h, i: (b, h, i, 0)),
            pl.BlockSpec((1, 1, nk, QK_PAD, tk), lambda b, h, i: (b, h, 0, 0, 0)),
            pl.BlockSpec((1, S, V_HEAD_DIM), lambda b, h, i: (b, 0, h)),
            pl.BlockSpec((1, QK_PAD, META_PAD), lambda b, h, i: (h, 0, 0)),
            pl.BlockSpec((1, META_PAD, V_HEAD_DIM), lambda b, h, i: (h, 0, 0)),
        ],
        out_specs=pl.BlockSpec((1, tq, V_HEAD_DIM), lambda b, h, i: (b, i, h)),
        out_shape=jax.ShapeDtypeStruct((B, S, H * V_HEAD_DIM), BF16),
        scratch_shapes=[
            pltpu.VMEM((2, tq, tk), F32), pltpu.VMEM((4, tq, tk), BF16), pltpu.VMEM((tq, V_HEAD_DIM), F32),
            pltpu.VMEM((tq, 2 * LANES), F32),
        ],
        compiler_params=_params(("parallel", "parallel", "parallel")), name="attention",
    )(q, kT, v, kT_meta, v_meta)


def _fnet_body(a_ref, b_ref, er_ref, ei_ref, pr_ref, pi_ref, wm_ref, abm_ref, o_ref, acc_ref, *, reps):
    ci = pl.program_id(2)

    @pl.when(ci == 0)
    def _():
        mt = _dot(wm_ref[...], abm_ref[...])
        acc_ref[...] = jnp.concatenate([mt] * reps, axis=1)

    lane = lax.broadcasted_iota(jnp.int32, pr_ref.shape, 1)
    sel = lane == ci
    pr = jnp.sum(jnp.where(sel, pr_ref[...], 0.0), axis=1, keepdims=True)
    pi = jnp.sum(jnp.where(sel, pi_ref[...], 0.0), axis=1, keepdims=True)
    er = er_ref[...]
    ei = ei_ref[...]
    wr = (pr * er - pi * ei).astype(BF16)
    wi = (pr * ei + pi * er).astype(BF16)
    acc_ref[...] += _dot(wr, a_ref[...]) + _dot(wi, b_ref[...])

    @pl.when(ci == pl.num_programs(2) - 1)
    def _():
        o_ref[...] = acc_ref[...].astype(BF16)


def _fnet(a, b, tabs, abm, tm, tkf, tn):
    S, NB = a.shape
    rows = tabs["er"].shape[0]
    nkt = S // tkf
    return pl.pallas_call(
        functools.partial(_fnet_body, reps=tn // FNET_DIM),
        grid=(NB // tn, rows // tm, nkt),
        in_specs=[
            pl.BlockSpec((tkf, tn), lambda n, r, c: (c, n)),
            pl.BlockSpec((tkf, tn), lambda n, r, c: (c, n)),
            pl.BlockSpec((tm, tkf), lambda n, r, c: (r, 0)),
            pl.BlockSpec((tm, tkf), lambda n, r, c: (r, 0)),
            pl.BlockSpec((tm, nkt), lambda n, r, c: (r, 0)),
            pl.BlockSpec((tm, nkt), lambda n, r, c: (r, 0)),
            pl.BlockSpec((tm, 2 * N_META), lambda n, r, c: (r, 0)),
            pl.BlockSpec((2 * N_META, FNET_DIM), lambda n, r, c: (0, 0)),
        ],
        out_specs=pl.BlockSpec((tm, tn), lambda n, r, c: (r, n)),
        out_shape=jax.ShapeDtypeStruct((rows, NB), BF16),
        scratch_shapes=[pltpu.VMEM((tm, tn), F32)],
        compiler_params=_params(("parallel", "parallel", "arbitrary")), name="fnet",
    )(a, b, tabs["er"], tabs["ei"], tabs["pr"], tabs["pi"], tabs["wm"], abm)


def _fnet_mirror_body(dk_ref, dkm_ref, x0_ref, j1_ref, j2_ref, m_ref):
    k = pl.num_programs(1) - 1 - pl.program_id(1)
    lower = jnp.where(k == 0, x0_ref[...], dkm_ref[...])
    m_ref[...] = (_dot(j1_ref[...], dk_ref[...]) + _dot(j2_ref[...], lower)).astype(BF16)


def _fnet_mirror(d, x0, tr, tn):
    half, NB = d.shape
    nt = half // tr
    a = jnp.arange(tr, dtype=jnp.int32)[:, None]
    c = jnp.arange(tr, dtype=jnp.int32)[None, :]
    j1 = (c == tr - N_META - a).astype(BF16)
    j2 = (c == 2 * tr - N_META - a).astype(BF16)
    x0_tile = jnp.zeros((tr, NB), BF16).at[tr - N_META:].set(x0)
    return pl.pallas_call(
        _fnet_mirror_body, grid=(NB // tn, nt),
        in_specs=[
            pl.BlockSpec((tr, tn), lambda n, i: (nt - 1 - i, n)),
            pl.BlockSpec((tr, tn), lambda n, i: (jnp.maximum(nt - 2 - i, 0), n)),
            pl.BlockSpec((tr, tn), lambda n, i: (0, n)),
            pl.BlockSpec((tr, tr), lambda n, i: (0, 0)),
            pl.BlockSpec((tr, tr), lambda n, i: (0, 0)),
        ],
        out_specs=pl.BlockSpec((tr, tn), lambda n, i: (i, n)),
        out_shape=jax.ShapeDtypeStruct((half, NB), BF16),
        compiler_params=_params(("parallel", "parallel")), name="fnet_mirror",
    )(d, d, x0_tile, j1, j2)


def _post_mix_body(x_ref, att_ref, flo_ref, fhi_ref, gate_ref, wo_ref, wfo_ref, wout_ref, nm_ref,
                   rhi_ref, rlo_ref, rb_ref, h1_ref, route_ref):
    a = _dot(att_ref[0], wo_ref[...])
    first_half = pl.program_id(1) < pl.num_programs(1) // 2
    f = jnp.where(first_half, flo_ref[...], fhi_ref[...])
    bb = _dot(f, wfo_ref[0])
    g = gate_ref[0].astype(F32)
    merged = g[:, :D_MODEL] * a + g[:, D_MODEL:] * bb
    h1 = x_ref[0] + _dot(merged.astype(BF16), wout_ref[...])
    h1_ref[0] = h1
    xn = _rms(h1, nm_ref[...])
    xh = xn.astype(BF16)

    xl = (xn - xh.astype(F32)).astype(BF16)
    logits = _dot(xh, rhi_ref[...]) + (_dot(xh, rlo_ref[...]) + _dot(xl, rhi_ref[...])) + rb_ref[...]
    lane = lax.broadcasted_iota(jnp.int32, logits.shape, 1)
    lanef = lane.astype(F32)
    far = float(2 * LANES)

    gl = jnp.where((lane >= N_EXPERTS) & (lane < N_EXPERTS + N_EXPERT_GROUPS), logits, NEG_BIG)
    gmax = jnp.max(gl, axis=1, keepdims=True)
    g_w = 1.0 / jnp.sum(jnp.exp(gl - gmax), axis=1, keepdims=True)
    gidx = jnp.min(jnp.where(gl == gmax, lanef, far), axis=1, keepdims=True) - float(N_EXPERTS)

    grp = (lane >> 3).astype(F32)
    el = jnp.where((lane < N_EXPERTS) & (grp == gidx), logits, NEG_BIG)
    m1 = jnp.max(el, axis=1, keepdims=True)
    i1 = jnp.min(jnp.where(el == m1, lanef, far), axis=1, keepdims=True)
    el2 = jnp.where(lanef == i1, NEG_BIG, el)
    m2 = jnp.max(el2, axis=1, keepdims=True)
    i2 = jnp.min(jnp.where(el2 == m2, lanef, far), axis=1, keepdims=True)
    r = jnp.exp(m2 - m1)
    w1 = g_w / (1.0 + r)
    w2 = g_w * r / (1.0 + r)

    first_is_lo = i1 < i2
    a_loc = jnp.minimum(i1, i2) - gidx * EXPERTS_PER_GROUP
    b_loc = jnp.maximum(i1, i2) - gidx * EXPERTS_PER_GROUP
    pair = a_loc * (2 * EXPERTS_PER_GROUP - 1 - a_loc) * 0.5 + (b_loc - a_loc - 1.0)
    bucket = gidx * PAIRS_PER_GROUP + pair
    w_lo = jnp.where(first_is_lo, w1, w2)
    w_hi = jnp.where(first_is_lo, w2, w1)
    route_ref[0] = jnp.where(lane == 0, bucket, jnp.where(lane == 1, w_lo, jnp.where(lane == 2, w_hi, 0.0)))


def _post_mix(x, att, f_lo, f_hi, gates, w, ts):
    B, S, _ = x.shape
    nh = S // ts // 2
    const = lambda shape: pl.BlockSpec(shape, lambda b, t: (0,) * len(shape))
    tok = lambda width: pl.BlockSpec((1, ts, width), lambda b, t: (b, t, 0))
    return pl.pallas_call(
        _post_mix_body, grid=(B, S // ts),
        in_specs=[
            tok(D_MODEL), tok(D_MODEL),
            pl.BlockSpec((ts, FNET_DIM), lambda b, t: (jnp.minimum(t, nh - 1), b)),
            pl.BlockSpec((ts, FNET_DIM), lambda b, t: (jnp.maximum(t - nh, 0), b)),
            tok(2 * D_MODEL),
            const((D_MODEL, D_MODEL)),
            pl.BlockSpec((1, FNET_DIM, D_MODEL), lambda b, t: (jnp.where(t < nh, 0, 1), 0, 0)),
            const((D_MODEL, D_MODEL)),
            const((1, D_MODEL)), const((D_MODEL, LANES)), const((D_MODEL, LANES)), const((1, LANES)),
        ],
        out_specs=(tok(D_MODEL), tok(LANES)),
        out_shape=(
            jax.ShapeDtypeStruct((B, S, D_MODEL), F32),
            jax.ShapeDtypeStruct((B, S, LANES), F32),
        ),
        compiler_params=_params(("parallel", "parallel")), name="post_mix",
    )(x, att, f_lo, f_hi, gates, w["w_o"], w["w_fo"], w["w_out"], w["norm_moe"], w["r_hi"], w["r_lo"], w["r_b"])


def _moe_body(lo_ref, hi_ref, nv_ref, src_ref, nxt_ref, rt_ref, h1_hbm,
              wgl_ref, wul_ref, wdl_ref, wgh_ref, wuh_ref, wdh_ref, nm_ref, nf_ref,
              out_hbm, xbuf, ybuf, gsem, ssem, *, tb):
    i = pl.program_id(0)
    slot = lax.rem(i, 2)
    other = 1 - slot
    cnt = nv_ref[i]
    prev = jnp.where(i > 0, nv_ref[jnp.maximum(i - 1, 0)], 0)

    def gather(idx_ref, s, r):
        return pltpu.make_async_copy(h1_hbm.at[idx_ref[0, 0, r]], xbuf.at[s, r], gsem.at[s])

    def scatter_wait(s, rows):
        full = pl.multiple_of(lax.shift_left(lax.shift_right_logical(rows, 3), 3), 8)

        @pl.when(full > 0)
        def _():
            pltpu.make_async_copy(ybuf.at[s, pl.ds(0, full)], out_hbm.at[pl.ds(0, full)], ssem.at[s]).wait()

        def one(r, c):
            pltpu.make_async_copy(ybuf.at[s, r], out_hbm.at[0], ssem.at[s]).wait()
            return c

        lax.fori_loop(full, rows, one, 0)

    @pl.when(i == 0)
    def _():
        for r in range(tb):
            gather(src_ref, 0, r).start()

    @pl.when(cnt > 0)
    def _():
        for r in range(tb):
            gather(src_ref, slot, r).wait()
        for r in range(tb):
            gather(nxt_ref, other, r).start()
        x = xbuf[slot]
        xb = _rms(x, nm_ref[...]).astype(BF16)
        diag = lax.broadcasted_iota(jnp.int32, (tb, tb), 0) == lax.broadcasted_iota(jnp.int32, (tb, tb), 1)
        w_lo, w_hi = (jnp.sum(jnp.where(diag, rt_ref[0, k:k + 1, :], 0.0), axis=1, keepdims=True) for k in range(2))

        def expert(wg_ref, wu_ref, wd_ref, wcol):
            gt = _dot(xb, wg_ref[0])
            up = _dot(xb, wu_ref[0])
            act = (gt / (1.0 + jnp.exp(-gt))) * up * wcol
            return _dot(act.astype(BF16), wd_ref[0])

        y = expert(wgl_ref, wul_ref, wdl_ref, w_lo) + expert(wgh_ref, wuh_ref, wdh_ref, w_hi)
        ybuf[slot] = _rms(x + y, nf_ref[...])

        scatter_wait(other, prev)

        def issue(r):
            pltpu.make_async_copy(ybuf.at[slot, r], out_hbm.at[src_ref[0, 0, r]], ssem.at[slot]).start()

        def issue8(g, c):
            for u in range(8):
                issue(g * 8 + u)
            return c

        def issue1(r, c):
            issue(r)
            return c

        groups = lax.shift_right_logical(cnt, 3)
        lax.fori_loop(0, groups, issue8, 0)
        lax.fori_loop(groups * 8, cnt, issue1, 0)

    @pl.when((cnt == 0) & (prev > 0))
    def _():
        for r in range(tb):
            gather(src_ref, slot, r).wait()
        scatter_wait(other, prev)


def _moe(h1, route, w, tb):
    N = h1.shape[0]
    t = _route_tables(route, tb)
    nblk = t["src"].shape[0]
    idx_spec = lambda f: pl.BlockSpec((1, 1, tb), f, memory_space=pltpu.SMEM)
    wspec = lambda shape, which: pl.BlockSpec(
        (1,) + shape, (lambda i, lo, hi, nv: (lo[i], 0, 0)) if which == 0 else (lambda i, lo, hi, nv: (hi[i], 0, 0)))
    up_shape, down_shape = (D_MODEL, D_FF_EXPERT), (D_FF_EXPERT, D_MODEL)
    grid_spec = pltpu.PrefetchScalarGridSpec(
        num_scalar_prefetch=3, grid=(nblk,),
        in_specs=[
            idx_spec(lambda i, lo, hi, nv: (i, 0, 0)),
            idx_spec(lambda i, lo, hi, nv: (jnp.minimum(i + 1, nblk - 1), 0, 0)),
            pl.BlockSpec((1, 2, tb), lambda i, lo, hi, nv: (i, 0, 0)),
            pl.BlockSpec(memory_space=pl.ANY),
            wspec(up_shape, 0), wspec(up_shape, 0), wspec(down_shape, 0),
            wspec(up_shape, 1), wspec(up_shape, 1), wspec(down_shape, 1),
            pl.BlockSpec((1, D_MODEL), lambda i, lo, hi, nv: (0, 0)),
            pl.BlockSpec((1, D_MODEL), lambda i, lo, hi, nv: (0, 0)),
        ],
        out_specs=pl.BlockSpec(memory_space=pl.ANY),
        scratch_shapes=[
            pltpu.VMEM((2, tb, D_MODEL), F32), pltpu.VMEM((2, tb, D_MODEL), F32),
            pltpu.SemaphoreType.DMA((2,)), pltpu.SemaphoreType.DMA((2,)),
        ],
    )
    return pl.pallas_call(
        functools.partial(_moe_body, tb=tb), grid_spec=grid_spec,
        out_shape=jax.ShapeDtypeStruct((N, D_MODEL), F32),
        compiler_params=_params(("arbitrary",)), name="moe",
    )(t["lo"], t["hi"], t["nvalid"], t["src"], t["src"], t["rt"], h1,
      w["w_eg"], w["w_eu"], w["w_ed"], w["w_eg"], w["w_eu"], w["w_ed"], w["norm_moe"], w["norm_final"])


def _route_tables(route, tb):
    N = route.shape[0]
    bucket = route[:, 0].astype(jnp.int32)
    order = jnp.argsort(bucket).astype(jnp.int32)
    ids = jnp.arange(N_BUCKETS, dtype=jnp.int32)
    counts = jnp.sum((bucket[:, None] == ids[None, :]).astype(jnp.int32), axis=0)
    starts = jnp.cumsum(counts) - counts
    padded = ((counts + tb - 1) // tb) * tb
    pend = jnp.cumsum(padded)
    poff = pend - padded
    nblk = N // tb + N_BUCKETS + 1
    blk_start = jnp.arange(nblk, dtype=jnp.int32) * tb
    active = blk_start < pend[-1]
    bb = jnp.minimum(jnp.sum((pend[None, :] <= blk_start[:, None]).astype(jnp.int32), axis=1), N_BUCKETS - 1)
    r0 = blk_start - poff[bb]
    nvalid = jnp.where(active, jnp.clip(counts[bb] - r0, 0, tb), 0).astype(jnp.int32)
    bb_w = jnp.where(active, bb, bb[jnp.maximum(jnp.sum(active.astype(jnp.int32)) - 1, 0)])
    rows = starts[bb][:, None] + r0[:, None] + jnp.arange(tb, dtype=jnp.int32)[None, :]
    src = order[jnp.clip(rows, 0, N - 1)]
    return {
        "lo": jnp.asarray(_BUCKET_LO)[bb_w], "hi": jnp.asarray(_BUCKET_HI)[bb_w], "nvalid": nvalid,
        "src": src[:, None, :],
        "rt": jnp.stack([route[:, 1][src], route[:, 2][src]], axis=1),
    }


def _rope_tables(pos):
    inv_freq = 1.0 / (ROPE_THETA ** (jnp.arange(ROPE_HALF, dtype=F32) / ROPE_HALF))
    ang = pos.astype(F32)[:, None] * inv_freq[None, :]
    c, s = jnp.cos(ang), jnp.sin(ang)
    z = jnp.zeros((pos.shape[0], LANES - QK_ROPE_DIM), F32)
    return {
        "cos_nat": jnp.concatenate([c, c, z], axis=1), "sin_nat": jnp.concatenate([s, s, z], axis=1),
        "cosT": c.T, "sinT": s.T,
    }


def _unit(k, n):
    ang = k.astype(F32) * (-2.0 * math.pi / n)
    return jnp.cos(ang), jnp.sin(ang)


def _fnet_tables(S, tkf, freqs):
    L = S + N_META
    j = freqs.astype(jnp.int32)[:, None]
    er, ei = _unit((j * jnp.arange(tkf, dtype=jnp.int32)[None, :]) % L, L)
    col0 = jnp.arange(S // tkf, dtype=jnp.int32)[None, :] * tkf + N_META
    pr, pi = _unit((j * col0) % L, L)
    mr, mi = _unit((j * jnp.arange(N_META, dtype=jnp.int32)[None, :]) % L, L)
    return {"er": er, "ei": ei, "pr": pr, "pi": pi, "wm": jnp.concatenate([mr, mi], axis=1).astype(BF16)}


def _channel_dft(L):
    c = jnp.arange(FNET_GROUP_DIM, dtype=jnp.int32)
    cr, ci = _unit((c[:, None] * c[None, :]) % FNET_GROUP_DIM, FNET_GROUP_DIM)
    norm = 1.0 / math.sqrt(L * FNET_GROUP_DIM)
    eye = jnp.eye(N_FNET_GROUPS, dtype=F32)
    return (jnp.kron(eye, cr * norm).astype(BF16), jnp.kron(eye, -ci * norm).astype(BF16))


def _to_bf16_body(x_ref, o_ref):
    o_ref[...] = x_ref[...].astype(BF16)


def _to_bf16(w):
    e, r, c = w.shape
    spec = pl.BlockSpec((4, r, c), lambda i: (i, 0, 0))
    return pl.pallas_call(
        _to_bf16_body, grid=(e // 4,), in_specs=[spec], out_specs=spec,
        out_shape=jax.ShapeDtypeStruct(w.shape, BF16),
        compiler_params=_params(("parallel",)), name="to_bf16",
    )(w)


def _prep_weights(norm_attn, w_in, g_q, w_uq, g_kv, w_uk, w_uv, w_o_mla, w_fo, w_out, norm_moe,
                  w_rg, b_rg, w_re, b_re, w_eg, w_eu, w_ed, norm_final):
    s1 = Q_LORA_RANK
    s2 = s1 + KV_LORA_RANK
    s3 = s2 + QK_ROPE_DIM
    s4 = s3 + FNET_DIM
    w_in_p = jnp.concatenate(
        [w_in[:, :s1], w_in[:, s1:s2], w_in[:, s3:s4], w_in[:, s4:], w_in[:, s2:s3],
         jnp.zeros((D_MODEL, LANES - QK_ROPE_DIM), w_in.dtype)], axis=1)
    w_uq_p = jnp.pad(w_uq, ((0, 0), (0, 0), (0, 2 * LANES - QK_DIM)))
    r = jnp.concatenate([w_re, w_rg, jnp.zeros((D_MODEL, LANES - N_EXPERTS - N_EXPERT_GROUPS), F32)], axis=1)
    r_hi = r.astype(BF16)
    r_b = jnp.concatenate([b_re, b_rg, jnp.zeros((LANES - N_EXPERTS - N_EXPERT_GROUPS,), F32)])[None, :]
    return {
        "norm_attn": norm_attn[None, :], "w_in": w_in_p.astype(BF16),
        "g_q": g_q[None, :], "w_uq": w_uq_p.reshape(Q_LORA_RANK, N_HEADS * 2 * LANES).astype(BF16),
        "g_kv": g_kv[None, :], "w_ukT": w_uk.reshape(KV_LORA_RANK, N_HEADS * QK_NOPE_DIM).T.astype(BF16),
        "w_uv": w_uv.reshape(KV_LORA_RANK, N_HEADS * V_HEAD_DIM).astype(BF16),
        "w_o": w_o_mla.astype(BF16), "w_out": w_out.astype(BF16),
        "w_fo": jnp.stack([w_fo, w_fo[_CHANNEL_FLIP]]).astype(BF16),
        "norm_moe": norm_moe[None, :], "r_hi": r_hi, "r_lo": (r - r_hi.astype(F32)).astype(BF16), "r_b": r_b,
        "w_eg": _to_bf16(w_eg.reshape(N_EXPERTS, D_MODEL, D_FF_EXPERT)),
        "w_eu": _to_bf16(w_eu.reshape(N_EXPERTS, D_MODEL, D_FF_EXPERT)),
        "w_ed": _to_bf16(w_ed.reshape(N_EXPERTS, D_FF_EXPERT, D_MODEL)),
        "norm_final": norm_final[None, :],
    }


def _tiles(B, S):
    ts = min(512, S // 4)
    return {
        "ts": ts, "tq": min(2048, S), "tm": min(1024, S // 2), "tkf": min(512, S),
        "tn": min(2048, B * FNET_DIM), "tb": 128,
    }


def _trunk(x, meta, w):
    B, S, _ = x.shape
    L = S + N_META
    t = _tiles(B, S)
    wl = dict(w)
    wl["cc"], wl["sc"] = _channel_dft(L)

    meta_pad = jnp.zeros((1, META_PAD, D_MODEL), F32).at[0, :N_META].set(meta)
    _, kT_m, v_m, a_m, b_m, _ = _in_proj(meta_pad, wl, _rope_tables(jnp.arange(META_PAD)), META_PAD)
    q, kT, v, a, b, gates = _in_proj(x, wl, _rope_tables(jnp.arange(S) + N_META), t["ts"])

    v_meta = v_m[0].reshape(META_PAD, N_HEADS, V_HEAD_DIM).transpose(1, 0, 2)
    att = _attention(q, kT, v, kT_m[0, :, 0], v_meta, t["tq"])
    abm = jnp.concatenate([a_m[:N_META], b_m[:N_META]], axis=0)
    half = S // 2
    f_lo = _fnet(a, b, _fnet_tables(S, t["tkf"], jnp.arange(half) + N_META), abm, t["tm"], t["tkf"], t["tn"])
    f_meta = _fnet(a, b, _fnet_tables(S, t["tkf"], jnp.arange(N_META)), abm, N_META, t["tkf"], t["tn"])
    f_hi = _fnet_mirror(f_lo, f_meta, min(512, half), t["tn"])
    h1, route = _post_mix(x, att, f_lo, f_hi, gates, wl, t["ts"])
    y = _moe(h1.reshape(B * S, D_MODEL), route.reshape(B * S, LANES), wl, t["tb"])
    return y.reshape(B, S, D_MODEL)


def kernel(x_prompt, x_sample, meta_tokens, norm_attn, w_in, g_q, w_uq, g_kv, w_uk, w_uv, w_o_mla, w_fo, w_out,
           norm_moe, w_router_group, b_router_group, w_router_expert, b_router_expert, w_exp_gate, w_exp_up,
           w_exp_down, norm_final):
    w = _prep_weights(norm_attn[0], w_in[0], g_q[0], w_uq[0], g_kv[0], w_uk[0], w_uv[0], w_o_mla[0], w_fo[0],
                      w_out[0], norm_moe[0], w_router_group[0], b_router_group[0], w_router_expert[0],
                      b_router_expert[0], w_exp_gate[0], w_exp_up[0], w_exp_down[0], norm_final)
    return _trunk(x_prompt, meta_tokens, w), _trunk(x_sample, meta_tokens, w)
```

```python
import functools
import math

import jax
import jax.numpy as jnp
import numpy as np
from jax import lax
from jax.experimental import pallas as pl
from jax.experimental.pallas import tpu as pltpu

D_MODEL = 1024
N_META = 16
N_HEADS = 8
QK_NOPE_DIM = 128
QK_ROPE_DIM = 64
ROPE_HALF = QK_ROPE_DIM // 2
QK_DIM = QK_NOPE_DIM + QK_ROPE_DIM
QK_PAD = 256
V_HEAD_DIM = 128
Q_LORA_RANK = 384
KV_LORA_RANK = 256
ROPE_THETA = 10000.0
N_FNET_GROUPS = 4
FNET_GROUP_DIM = 128
FNET_DIM = N_FNET_GROUPS * FNET_GROUP_DIM
N_EXPERT_GROUPS = 4
EXPERTS_PER_GROUP = 8
N_EXPERTS = N_EXPERT_GROUPS * EXPERTS_PER_GROUP
D_FF_EXPERT = 256
EPS = 1e-6
PAIRS_PER_GROUP = EXPERTS_PER_GROUP * (EXPERTS_PER_GROUP - 1) // 2
N_BUCKETS = N_EXPERT_GROUPS * PAIRS_PER_GROUP
_CHANNEL_FLIP = np.array([g * FNET_GROUP_DIM + (FNET_GROUP_DIM - c) % FNET_GROUP_DIM
                          for g in range(N_FNET_GROUPS) for c in range(FNET_GROUP_DIM)], np.int32)
_PAIRS = [(a, b) for a in range(EXPERTS_PER_GROUP) for b in range(a + 1, EXPERTS_PER_GROUP)]
_BUCKET_LO = np.array([g * EXPERTS_PER_GROUP + a for g in range(N_EXPERT_GROUPS) for a, _ in _PAIRS], np.int32)
_BUCKET_HI = np.array([g * EXPERTS_PER_GROUP + b for g in range(N_EXPERT_GROUPS) for _, b in _PAIRS], np.int32)

LANES = 128
META_PAD = 128
VMEM_LIMIT_BYTES = 56 * 1024 * 1024
NEG_BIG = -1e30
FAST_ROW_SUM_LIMIT = 2.0 ** 40

_CQ0, _CQ1 = 0, Q_LORA_RANK
_CKV0, _CKV1 = _CQ1, _CQ1 + KV_LORA_RANK
_UF0, _UF1 = _CKV1, _CKV1 + FNET_DIM
_G0, _G1 = _UF1, _UF1 + 2 * D_MODEL
_KR0, _KR1 = _G1, _G1 + LANES
IN_DIM_PADDED = _KR1

F32 = jnp.float32
BF16 = jnp.bfloat16


def _dot(a, b):
    return jnp.dot(a, b, preferred_element_type=F32)


def _rms(x, g):
    ms = jnp.mean(x * x, axis=-1, keepdims=True)
    return x * lax.rsqrt(ms + EPS) * g


def _params(semantics):
    return pltpu.CompilerParams(dimension_semantics=semantics, vmem_limit_bytes=VMEM_LIMIT_BYTES)


def _in_proj_body(x_ref, na_ref, win_ref, gq_ref, wuq_ref, gkv_ref, wukT_ref, wuv_ref,
                  cosn_ref, sinn_ref, cosT_ref, sinT_ref, cc_ref, sc_ref,
                  q_ref, kT_ref, v_ref, a_ref, b_ref):
    x = x_ref[0]
    hb = _rms(x, na_ref[...]).astype(BF16)

    cqn = _rms(_dot(hb, win_ref[:, _CQ0:_CQ1]), gq_ref[...])
    qa = _dot(cqn.astype(BF16), wuq_ref[...])
    cn = cosn_ref[...]
    sn = sinn_ref[...]
    first_half = lax.broadcasted_iota(jnp.int32, cn.shape, 1) < ROPE_HALF
    scale = math.log2(math.e) / math.sqrt(QK_DIM)
    for h in range(N_HEADS):
        c0 = h * 2 * LANES
        qr = qa[:, c0 + LANES:c0 + 2 * LANES]
        rot = jnp.where(first_half, -pltpu.roll(qr, LANES - ROPE_HALF, 1), pltpu.roll(qr, ROPE_HALF, 1))
        q_ref[0, h, :, 0:QK_NOPE_DIM] = (qa[:, c0:c0 + LANES] * scale).astype(BF16)
        q_ref[0, h, :, QK_NOPE_DIM:QK_PAD] = ((qr * cn + rot * sn) * scale).astype(BF16)

    ckvn = _rms(_dot(hb, win_ref[:, _CKV0:_CKV1]), gkv_ref[...])
    knT = _dot(wukT_ref[...], ckvn.T.astype(BF16))
    krT = _dot(hb, win_ref[:, _KR0:_KR1]).T
    x1 = krT[0:ROPE_HALF]
    x2 = krT[ROPE_HALF:QK_ROPE_DIM]
    cT = cosT_ref[...]
    sT = sinT_ref[...]
    r1 = (x1 * cT - x2 * sT).astype(BF16)
    r2 = (x2 * cT + x1 * sT).astype(BF16)
    for h in range(N_HEADS):
        kT_ref[0, h, 0, 0:QK_NOPE_DIM, :] = knT[h * QK_NOPE_DIM:(h + 1) * QK_NOPE_DIM].astype(BF16)
        kT_ref[0, h, 0, QK_NOPE_DIM:QK_NOPE_DIM + ROPE_HALF, :] = r1
        kT_ref[0, h, 0, QK_NOPE_DIM + ROPE_HALF:QK_DIM, :] = r2
        kT_ref[0, h, 0, QK_DIM:QK_PAD, :] = jnp.zeros((QK_PAD - QK_DIM, r2.shape[1]), BF16)
    v_ref[0] = _dot(ckvn.astype(BF16), wuv_ref[...]).astype(BF16)

    ub = _dot(hb, win_ref[:, _UF0:_UF1]).astype(BF16)
    a_ref[...] = _dot(ub, cc_ref[...]).astype(BF16)
    b_ref[...] = _dot(ub, sc_ref[...]).astype(BF16)


def _in_proj(x, w, rope, ts):
    B, S, _ = x.shape
    nt = S // ts
    const = lambda shape: pl.BlockSpec(shape, lambda b, t: (0,) * len(shape))
    out_shape = (
        jax.ShapeDtypeStruct((B, N_HEADS, S, QK_PAD), BF16),
        jax.ShapeDtypeStruct((B, N_HEADS, nt, QK_PAD, ts), BF16),
        jax.ShapeDtypeStruct((B, S, N_HEADS * V_HEAD_DIM), BF16),
        jax.ShapeDtypeStruct((S, B * FNET_DIM), BF16),
        jax.ShapeDtypeStruct((S, B * FNET_DIM), BF16),
    )
    in_specs = [
        pl.BlockSpec((1, ts, D_MODEL), lambda b, t: (b, t, 0)),
        const((1, D_MODEL)),
        const((D_MODEL, IN_DIM_PADDED)),
        const((1, Q_LORA_RANK)),
        const((Q_LORA_RANK, N_HEADS * 2 * LANES)),
        const((1, KV_LORA_RANK)),
        const((N_HEADS * QK_NOPE_DIM, KV_LORA_RANK)),
        const((KV_LORA_RANK, N_HEADS * V_HEAD_DIM)),
        pl.BlockSpec((ts, LANES), lambda b, t: (t, 0)),
        pl.BlockSpec((ts, LANES), lambda b, t: (t, 0)),
        pl.BlockSpec((ROPE_HALF, ts), lambda b, t: (0, t)),
        pl.BlockSpec((ROPE_HALF, ts), lambda b, t: (0, t)),
        const((FNET_DIM, FNET_DIM)),
        const((FNET_DIM, FNET_DIM)),
    ]
    out_specs = (
        pl.BlockSpec((1, N_HEADS, ts, QK_PAD), lambda b, t: (b, 0, t, 0)),
        pl.BlockSpec((1, N_HEADS, 1, QK_PAD, ts), lambda b, t: (b, 0, t, 0, 0)),
        pl.BlockSpec((1, ts, N_HEADS * V_HEAD_DIM), lambda b, t: (b, t, 0)),
        pl.BlockSpec((ts, FNET_DIM), lambda b, t: (t, b)),
        pl.BlockSpec((ts, FNET_DIM), lambda b, t: (t, b)),
    )
    return pl.pallas_call(
        _in_proj_body, grid=(B, nt), in_specs=in_specs, out_specs=out_specs, out_shape=out_shape,
        compiler_params=_params(("parallel", "parallel")), name="in_proj",
    )(x, w["norm_attn"], w["w_in"], w["g_q"], w["w_uq"], w["g_kv"], w["w_ukT"], w["w_uv"],
      rope["cos_nat"], rope["sin_nat"], rope["cosT"], rope["sinT"], w["cc"], w["sc"])


def _attn_body(q_ref, kT_ref, v_ref, kmT_ref, vm_ref, o_ref, s_ref, p_ref, acc_ref, accx_ref, *, nk, tk):
    q = q_ref[0, 0]
    col = lax.broadcasted_iota(jnp.int32, (q.shape[0], META_PAD), 1)

    def meta_scores():
        return jnp.where(col < N_META, _dot(q, kmT_ref[0]), NEG_BIG)

    def v_chunk(j):
        return v_ref[0, pl.ds(pl.multiple_of(j * tk, tk), tk), :]

    def ones_column(rows):
        return jnp.where(lax.broadcasted_iota(jnp.int32, (rows, LANES), 1) == 0, 1.0, 0.0).astype(BF16)

    ones_col = ones_column(tk)
    s_meta = meta_scores()
    m0 = jnp.max(s_meta, axis=1, keepdims=True)
    vmx = jnp.concatenate([vm_ref[0], ones_column(META_PAD)], axis=1)
    accx_ref[...] = _dot(jnp.exp2(s_meta - m0).astype(BF16), vmx)

    def probs(j, slot):
        p_ref[slot] = jnp.exp2(_dot(q, kT_ref[0, 0, j]) - m0).astype(BF16)

    def accumulate(j, slot):
        accx_ref[...] += _dot(p_ref[slot], jnp.concatenate([v_chunk(j), ones_col], axis=1))

    probs(0, 0)
    probs(1, 1)
    accumulate(0, 0)
    probs(2, 2)
    accumulate(1, 1)
    probs(3, 3)

    def fast_quad(g, c):
        a = 4 * g + 4
        accumulate(a - 2, 2)
        probs(a, 0)
        accumulate(a - 1, 3)
        probs(a + 1, 1)
        accumulate(a, 0)
        probs(a + 2, 2)
        accumulate(a + 1, 1)
        probs(a + 3, 3)
        return c

    lax.fori_loop(0, (nk - 4) // 4, fast_quad, 0)
    accumulate(nk - 2, 2)
    accumulate(nk - 1, 3)
    l_fast = accx_ref[:, V_HEAD_DIM:2 * LANES][:, 0:1]
    fast_ok = jnp.max(l_fast) < FAST_ROW_SUM_LIMIT

    @pl.when(fast_ok)
    def _():
        o_ref[0] = (accx_ref[:, 0:V_HEAD_DIM] / l_fast).astype(BF16)

    def scores(j, slot):
        s_ref[slot] = _dot(q, kT_ref[0, 0, j])

    def values(j, slot, alpha):
        acc_ref[...] = alpha * acc_ref[...] + _dot(p_ref[slot], v_chunk(j))

    def softmax(slot, m, l):
        s = s_ref[slot]
        m_new = jnp.maximum(m, jnp.max(s, axis=1, keepdims=True))
        alpha = jnp.exp2(m - m_new)
        p = jnp.exp2(s - m_new)
        p_ref[slot] = p.astype(BF16)
        return m_new, alpha * l + jnp.sum(p, axis=1, keepdims=True), alpha

    def pair(jj, carry, last):
        m, l, alpha = carry
        a = 2 * jj
        values(jnp.maximum(a - 1, 0), 1, alpha)
        scores(a + 1, 1)
        m, l, alpha = softmax(0, m, l)
        values(a, 0, alpha)
        if not last:
            scores(a + 2, 0)
        return softmax(1, m, l)

    @pl.when(jnp.logical_not(fast_ok))
    def _():
        s = meta_scores()
        m = jnp.max(s, axis=1, keepdims=True)
        p = jnp.exp2(s - m)
        l = jnp.sum(p, axis=1, keepdims=True)
        acc_ref[...] = _dot(p.astype(BF16), vm_ref[0])
        p_ref[1] = jnp.zeros(p_ref.shape[1:], BF16)
        scores(0, 0)
        carry = (m, l, jnp.ones_like(m))
        carry = lax.fori_loop(0, nk // 2 - 1, functools.partial(pair, last=False), carry)
        m, l, alpha = pair(nk // 2 - 1, carry, last=True)
        values(nk - 1, 1, alpha)
        o_ref[0] = (acc_ref[...] / l).astype(BF16)


def _attention(q, kT, v, kT_meta, v_meta, tq):
    B, H, S, _ = q.shape
    nk, tk = kT.shape[2], kT.shape[4]
    assert nk % 4 == 0, "the key loop is unrolled four chunks at a time"
    return pl.pallas_call(
        functools.partial(_attn_body, nk=nk, tk=tk),
        grid=(B, H, S // tq),
        in_specs=[
            pl.BlockSpec((1, 1, tq, QK_PAD), lambda b, h, i: (b, h, i, 0)),
            pl.BlockSpec((1, 1, nk, QK_PAD, tk), lambda b, h, i: (b, h, 0, 0, 0)),
            pl.BlockSpec((1, S, V_HEAD_DIM), lambda b, h, i: (b, 0, h)),
            pl.BlockSpec((1, QK_PAD, META_PAD), lambda b, h, i: (h, 0, 0)),
            pl.BlockSpec((1, META_PAD, V_HEAD_DIM), lambda b, h, i: (h, 0, 0)),
        ],
        out_specs=pl.BlockSpec((1, tq, V_HEAD_DIM), lambda b, h, i: (b, i, h)),
        out_shape=jax.ShapeDtypeStruct((B, S, H * V_HEAD_DIM), BF16),
        scratch_shapes=[
            pltpu.VMEM((2, tq, tk), F32), pltpu.VMEM((4, tq, tk), BF16), pltpu.VMEM((tq, V_HEAD_DIM), F32),
            pltpu.VMEM((tq, 2 * LANES), F32),
        ],
        compiler_params=_params(("parallel", "parallel", "parallel")), name="attention",
    )(q, kT, v, kT_meta, v_meta)


def _fnet_body(a_ref, b_ref, er_ref, ei_ref, pr_ref, pi_ref, wm_ref, abm_ref, o_ref, acc_ref, *, reps):
    ci = pl.program_id(2)

    @pl.when(ci == 0)
    def _():
        mt = _dot(wm_ref[...], abm_ref[...])
        acc_ref[...] = jnp.concatenate([mt] * reps, axis=1)

    lane = lax.broadcasted_iota(jnp.int32, pr_ref.shape, 1)
    sel = lane == ci
    pr = jnp.sum(jnp.where(sel, pr_ref[...], 0.0), axis=1, keepdims=True)
    pi = jnp.sum(jnp.where(sel, pi_ref[...], 0.0), axis=1, keepdims=True)
    er = er_ref[...]
    ei = ei_ref[...]
    wr = (pr * er - pi * ei).astype(BF16)
    wi = (pr * ei + pi * er).astype(BF16)
    acc_ref[...] += _dot(wr, a_ref[...]) + _dot(wi, b_ref[...])

    @pl.when(ci == pl.num_programs(2) - 1)
    def _():
        o_ref[...] = acc_ref[...].astype(BF16)


def _fnet(a, b, tabs, abm, tm, tkf, tn):
    S, NB = a.shape
    rows = tabs["er"].shape[0]
    nkt = S // tkf
    return pl.pallas_call(
        functools.partial(_fnet_body, reps=tn // FNET_DIM),
        grid=(NB // tn, rows // tm, nkt),
        in_specs=[
            pl.BlockSpec((tkf, tn), lambda n, r, c: (c, n)),
            pl.BlockSpec((tkf, tn), lambda n, r, c: (c, n)),
            pl.BlockSpec((tm, tkf), lambda n, r, c: (r, 0)),
            pl.BlockSpec((tm, tkf), lambda n, r, c: (r, 0)),
            pl.BlockSpec((tm, nkt), lambda n, r, c: (r, 0)),
            pl.BlockSpec((tm, nkt), lambda n, r, c: (r, 0)),
            pl.BlockSpec((tm, 2 * N_META), lambda n, r, c: (r, 0)),
            pl.BlockSpec((2 * N_META, FNET_DIM), lambda n, r, c: (0, 0)),
        ],
        out_specs=pl.BlockSpec((tm, tn), lambda n, r, c: (r, n)),
        out_shape=jax.ShapeDtypeStruct((rows, NB), BF16),
        scratch_shapes=[pltpu.VMEM((tm, tn), F32)],
        compiler_params=_params(("parallel", "parallel", "arbitrary")), name="fnet",
    )(a, b, tabs["er"], tabs["ei"], tabs["pr"], tabs["pi"], tabs["wm"], abm)


def _fnet_mirror_body(dk_ref, dkm_ref, x0_ref, j1_ref, j2_ref, m_ref):
    k = pl.num_programs(1) - 1 - pl.program_id(1)
    lower = jnp.where(k == 0, x0_ref[...], dkm_ref[...])
    m_ref[...] = (_dot(j1_ref[...], dk_ref[...]) + _dot(j2_ref[...], lower)).astype(BF16)


def _fnet_mirror(d, x0, tr, tn):
    half, NB = d.shape
    nt = half // tr
    a = jnp.arange(tr, dtype=jnp.int32)[:, None]
    c = jnp.arange(tr, dtype=jnp.int32)[None, :]
    j1 = (c == tr - N_META - a).astype(BF16)
    j2 = (c == 2 * tr - N_META - a).astype(BF16)
    x0_tile = jnp.zeros((tr, NB), BF16).at[tr - N_META:].set(x0)
    return pl.pallas_call(
        _fnet_mirror_body, grid=(NB // tn, nt),
        in_specs=[
            pl.BlockSpec((tr, tn), lambda n, i: (nt - 1 - i, n)),
            pl.BlockSpec((tr, tn), lambda n, i: (jnp.maximum(nt - 2 - i, 0), n)),
            pl.BlockSpec((tr, tn), lambda n, i: (0, n)),
            pl.BlockSpec((tr, tr), lambda n, i: (0, 0)),
            pl.BlockSpec((tr, tr), lambda n, i: (0, 0)),
        ],
        out_specs=pl.BlockSpec((tr, tn), lambda n, i: (i, n)),
        out_shape=jax.ShapeDtypeStruct((half, NB), BF16),
        compiler_params=_params(("parallel", "parallel")), name="fnet_mirror",
    )(d, d, x0_tile, j1, j2)


def _post_mix_body(x_ref, att_ref, flo_ref, fhi_ref, na_ref, wg_ref, wo_ref, wfo_ref, wout_ref, nm_ref,
                   rhi_ref, rlo_ref, rb_ref, h1_ref, route_ref):
    a = _dot(att_ref[0], wo_ref[...])
    first_half = pl.program_id(1) < pl.num_programs(1) // 2
    f = jnp.where(first_half, flo_ref[...], fhi_ref[...])
    bb = _dot(f, wfo_ref[0])
    g = 1.0 / (1.0 + jnp.exp(-_dot(_rms(x_ref[0], na_ref[...]).astype(BF16), wg_ref[...])))
    merged = g[:, :D_MODEL] * a + g[:, D_MODEL:] * bb
    h1 = x_ref[0] + _dot(merged.astype(BF16), wout_ref[...])
    h1_ref[0] = h1
    xn = _rms(h1, nm_ref[...])
    xh = xn.astype(BF16)

    xl = (xn - xh.astype(F32)).astype(BF16)
    logits = _dot(xh, rhi_ref[...]) + (_dot(xh, rlo_ref[...]) + _dot(xl, rhi_ref[...])) + rb_ref[...]
    lane = lax.broadcasted_iota(jnp.int32, logits.shape, 1)
    lanef = lane.astype(F32)
    far = float(2 * LANES)

    gl = jnp.where((lane >= N_EXPERTS) & (lane < N_EXPERTS + N_EXPERT_GROUPS), logits, NEG_BIG)
    gmax = jnp.max(gl, axis=1, keepdims=True)
    g_w = 1.0 / jnp.sum(jnp.exp(gl - gmax), axis=1, keepdims=True)
    gidx = jnp.min(jnp.where(gl == gmax, lanef, far), axis=1, keepdims=True) - float(N_EXPERTS)

    grp = (lane >> 3).astype(F32)
    el = jnp.where((lane < N_EXPERTS) & (grp == gidx), logits, NEG_BIG)
    m1 = jnp.max(el, axis=1, keepdims=True)
    i1 = jnp.min(jnp.where(el == m1, lanef, far), axis=1, keepdims=True)
    el2 = jnp.where(lanef == i1, NEG_BIG, el)
    m2 = jnp.max(el2, axis=1, keepdims=True)
    i2 = jnp.min(jnp.where(el2 == m2, lanef, far), axis=1, keepdims=True)
    r = jnp.exp(m2 - m1)
    w1 = g_w / (1.0 + r)
    w2 = g_w * r / (1.0 + r)

    first_is_lo = i1 < i2
    a_loc = jnp.minimum(i1, i2) - gidx * EXPERTS_PER_GROUP
    b_loc = jnp.maximum(i1, i2) - gidx * EXPERTS_PER_GROUP
    pair = a_loc * (2 * EXPERTS_PER_GROUP - 1 - a_loc) * 0.5 + (b_loc - a_loc - 1.0)
    bucket = gidx * PAIRS_PER_GROUP + pair
    w_lo = jnp.where(first_is_lo, w1, w2)
    w_hi = jnp.where(first_is_lo, w2, w1)
    route_ref[0] = jnp.where(lane == 0, bucket, jnp.where(lane == 1, w_lo, jnp.where(lane == 2, w_hi, 0.0)))


def _post_mix(x, att, f_lo, f_hi, w, ts):
    B, S, _ = x.shape
    nh = S // ts // 2
    const = lambda shape: pl.BlockSpec(shape, lambda b, t: (0,) * len(shape))
    tok = lambda width: pl.BlockSpec((1, ts, width), lambda b, t: (b, t, 0))
    return pl.pallas_call(
        _post_mix_body, grid=(B, S // ts),
        in_specs=[
            tok(D_MODEL), tok(D_MODEL),
            pl.BlockSpec((ts, FNET_DIM), lambda b, t: (jnp.minimum(t, nh - 1), b)),
            pl.BlockSpec((ts, FNET_DIM), lambda b, t: (jnp.maximum(t - nh, 0), b)),
            const((1, D_MODEL)), const((D_MODEL, 2 * D_MODEL)),
            const((D_MODEL, D_MODEL)),
            pl.BlockSpec((1, FNET_DIM, D_MODEL), lambda b, t: (jnp.where(t < nh, 0, 1), 0, 0)),
            const((D_MODEL, D_MODEL)),
            const((1, D_MODEL)), const((D_MODEL, LANES)), const((D_MODEL, LANES)), const((1, LANES)),
        ],
        out_specs=(tok(D_MODEL), tok(LANES)),
        out_shape=(
            jax.ShapeDtypeStruct((B, S, D_MODEL), F32),
            jax.ShapeDtypeStruct((B, S, LANES), F32),
        ),
        compiler_params=_params(("parallel", "parallel")), name="post_mix",
    )(x, att, f_lo, f_hi, w["norm_attn"], w["w_gate"], w["w_o"], w["w_fo"], w["w_out"], w["norm_moe"], w["r_hi"], w["r_lo"], w["r_b"])


def _moe_body(lo_ref, hi_ref, nv_ref, src_ref, nxt_ref, rt_ref, h1_hbm,
              wgl_ref, wul_ref, wdl_ref, wgh_ref, wuh_ref, wdh_ref, nm_ref, nf_ref,
              out_hbm, xbuf, ybuf, gsem, ssem, *, tb):
    i = pl.program_id(0)
    slot = lax.rem(i, 2)
    other = 1 - slot
    cnt = nv_ref[i]
    prev = jnp.where(i > 0, nv_ref[jnp.maximum(i - 1, 0)], 0)

    def gather(idx_ref, s, r):
        return pltpu.make_async_copy(h1_hbm.at[idx_ref[0, 0, r]], xbuf.at[s, r], gsem.at[s])

    def scatter_wait(s, rows):
        full = pl.multiple_of(lax.shift_left(lax.shift_right_logical(rows, 3), 3), 8)

        @pl.when(full > 0)
        def _():
            pltpu.make_async_copy(ybuf.at[s, pl.ds(0, full)], out_hbm.at[pl.ds(0, full)], ssem.at[s]).wait()

        def one(r, c):
            pltpu.make_async_copy(ybuf.at[s, r], out_hbm.at[0], ssem.at[s]).wait()
            return c

        lax.fori_loop(full, rows, one, 0)

    @pl.when(i == 0)
    def _():
        for r in range(tb):
            gather(src_ref, 0, r).start()

    @pl.when(cnt > 0)
    def _():
        for r in range(tb):
            gather(src_ref, slot, r).wait()
        for r in range(tb):
            gather(nxt_ref, other, r).start()
        x = xbuf[slot]
        xb = _rms(x, nm_ref[...]).astype(BF16)
        diag = lax.broadcasted_iota(jnp.int32, (tb, tb), 0) == lax.broadcasted_iota(jnp.int32, (tb, tb), 1)
        w_lo, w_hi = (jnp.sum(jnp.where(diag, rt_ref[0, k:k + 1, :], 0.0), axis=1, keepdims=True) for k in range(2))

        def expert(wg_ref, wu_ref, wd_ref, wcol):
            gt = _dot(xb, wg_ref[0])
            up = _dot(xb, wu_ref[0])
            act = (gt / (1.0 + jnp.exp(-gt))) * up * wcol
            return _dot(act.astype(BF16), wd_ref[0])

        y = expert(wgl_ref, wul_ref, wdl_ref, w_lo) + expert(wgh_ref, wuh_ref, wdh_ref, w_hi)
        ybuf[slot] = _rms(x + y, nf_ref[...])

        scatter_wait(other, prev)

        def issue(r):
            pltpu.make_async_copy(ybuf.at[slot, r], out_hbm.at[src_ref[0, 0, r]], ssem.at[slot]).start()

        def issue8(g, c):
            for u in range(8):
                issue(g * 8 + u)
            return c

        def issue1(r, c):
            issue(r)
            return c

        groups = lax.shift_right_logical(cnt, 3)
        lax.fori_loop(0, groups, issue8, 0)
        lax.fori_loop(groups * 8, cnt, issue1, 0)

    @pl.when((cnt == 0) & (prev > 0))
    def _():
        for r in range(tb):
            gather(src_ref, slot, r).wait()
        scatter_wait(other, prev)


def _moe(h1, route, w, tb):
    N = h1.shape[0]
    t = _route_tables(route, tb)
    nblk = t["src"].shape[0]
    idx_spec = lambda f: pl.BlockSpec((1, 1, tb), f, memory_space=pltpu.SMEM)
    wspec = lambda shape, which: pl.BlockSpec(
        (1,) + shape, (lambda i, lo, hi, nv: (lo[i], 0, 0)) if which == 0 else (lambda i, lo, hi, nv: (hi[i], 0, 0)))
    up_shape, down_shape = (D_MODEL, D_FF_EXPERT), (D_FF_EXPERT, D_MODEL)
    grid_spec = pltpu.PrefetchScalarGridSpec(
        num_scalar_prefetch=3, grid=(nblk,),
        in_specs=[
            idx_spec(lambda i, lo, hi, nv: (i, 0, 0)),
            idx_spec(lambda i, lo, hi, nv: (jnp.minimum(i + 1, nblk - 1), 0, 0)),
            pl.BlockSpec((1, 2, tb), lambda i, lo, hi, nv: (i, 0, 0)),
            pl.BlockSpec(memory_space=pl.ANY),
            wspec(up_shape, 0), wspec(up_shape, 0), wspec(down_shape, 0),
            wspec(up_shape, 1), wspec(up_shape, 1), wspec(down_shape, 1),
            pl.BlockSpec((1, D_MODEL), lambda i, lo, hi, nv: (0, 0)),
            pl.BlockSpec((1, D_MODEL), lambda i, lo, hi, nv: (0, 0)),
        ],
        out_specs=pl.BlockSpec(memory_space=pl.ANY),
        scratch_shapes=[
            pltpu.VMEM((2, tb, D_MODEL), F32), pltpu.VMEM((2, tb, D_MODEL), F32),
            pltpu.SemaphoreType.DMA((2,)), pltpu.SemaphoreType.DMA((2,)),
        ],
    )
    return pl.pallas_call(
        functools.partial(_moe_body, tb=tb), grid_spec=grid_spec,
        out_shape=jax.ShapeDtypeStruct((N, D_MODEL), F32),
        compiler_params=_params(("arbitrary",)), name="moe",
    )(t["lo"], t["hi"], t["nvalid"], t["src"], t["src"], t["rt"], h1,
      w["w_eg"], w["w_eu"], w["w_ed"], w["w_eg"], w["w_eu"], w["w_ed"], w["norm_moe"], w["norm_final"])


def _route_tables(route, tb):
    N = route.shape[0]
    bucket = route[:, 0].astype(jnp.int32)
    order = jnp.argsort(bucket).astype(jnp.int32)
    ids = jnp.arange(N_BUCKETS, dtype=jnp.int32)
    counts = jnp.sum((bucket[:, None] == ids[None, :]).astype(jnp.int32), axis=0)
    starts = jnp.cumsum(counts) - counts
    padded = ((counts + tb - 1) // tb) * tb
    pend = jnp.cumsum(padded)
    poff = pend - padded
    nblk = N // tb + N_BUCKETS + 1
    blk_start = jnp.arange(nblk, dtype=jnp.int32) * tb
    active = blk_start < pend[-1]
    bb = jnp.minimum(jnp.sum((pend[None, :] <= blk_start[:, None]).astype(jnp.int32), axis=1), N_BUCKETS - 1)
    r0 = blk_start - poff[bb]
    nvalid = jnp.where(active, jnp.clip(counts[bb] - r0, 0, tb), 0).astype(jnp.int32)
    bb_w = jnp.where(active, bb, bb[jnp.maximum(jnp.sum(active.astype(jnp.int32)) - 1, 0)])
    rows = starts[bb][:, None] + r0[:, None] + jnp.arange(tb, dtype=jnp.int32)[None, :]
    src = order[jnp.clip(rows, 0, N - 1)]
    return {
        "lo": jnp.asarray(_BUCKET_LO)[bb_w], "hi": jnp.asarray(_BUCKET_HI)[bb_w], "nvalid": nvalid,
        "src": src[:, None, :],
        "rt": jnp.stack([route[:, 1][src], route[:, 2][src]], axis=1),
    }


def _rope_tables(pos):
    inv_freq = 1.0 / (ROPE_THETA ** (jnp.arange(ROPE_HALF, dtype=F32) / ROPE_HALF))
    ang = pos.astype(F32)[:, None] * inv_freq[None, :]
    c, s = jnp.cos(ang), jnp.sin(ang)
    z = jnp.zeros((pos.shape[0], LANES - QK_ROPE_DIM), F32)
    return {
        "cos_nat": jnp.concatenate([c, c, z], axis=1), "sin_nat": jnp.concatenate([s, s, z], axis=1),
        "cosT": c.T, "sinT": s.T,
    }


def _unit(k, n):
    ang = k.astype(F32) * (-2.0 * math.pi / n)
    return jnp.cos(ang), jnp.sin(ang)


def _fnet_tables(S, tkf, freqs):
    L = S + N_META
    j = freqs.astype(jnp.int32)[:, None]
    er, ei = _unit((j * jnp.arange(tkf, dtype=jnp.int32)[None, :]) % L, L)
    col0 = jnp.arange(S // tkf, dtype=jnp.int32)[None, :] * tkf + N_META
    pr, pi = _unit((j * col0) % L, L)
    mr, mi = _unit((j * jnp.arange(N_META, dtype=jnp.int32)[None, :]) % L, L)
    return {"er": er, "ei": ei, "pr": pr, "pi": pi, "wm": jnp.concatenate([mr, mi], axis=1).astype(BF16)}


def _channel_dft(L):
    c = jnp.arange(FNET_GROUP_DIM, dtype=jnp.int32)
    cr, ci = _unit((c[:, None] * c[None, :]) % FNET_GROUP_DIM, FNET_GROUP_DIM)
    norm = 1.0 / math.sqrt(L * FNET_GROUP_DIM)
    eye = jnp.eye(N_FNET_GROUPS, dtype=F32)
    return (jnp.kron(eye, cr * norm).astype(BF16), jnp.kron(eye, -ci * norm).astype(BF16))


def _to_bf16_body(x_ref, o_ref):
    o_ref[...] = x_ref[...].astype(BF16)


def _to_bf16(w):
    e, r, c = w.shape
    spec = pl.BlockSpec((4, r, c), lambda i: (i, 0, 0))
    return pl.pallas_call(
        _to_bf16_body, grid=(e // 4,), in_specs=[spec], out_specs=spec,
        out_shape=jax.ShapeDtypeStruct(w.shape, BF16),
        compiler_params=_params(("parallel",)), name="to_bf16",
    )(w)


def _prep_weights(norm_attn, w_in, g_q, w_uq, g_kv, w_uk, w_uv, w_o_mla, w_fo, w_out, norm_moe,
                  w_rg, b_rg, w_re, b_re, w_eg, w_eu, w_ed, norm_final):
    s1 = Q_LORA_RANK
    s2 = s1 + KV_LORA_RANK
    s3 = s2 + QK_ROPE_DIM
    s4 = s3 + FNET_DIM
    w_in_p = jnp.concatenate(
        [w_in[:, :s1], w_in[:, s1:s2], w_in[:, s3:s4], w_in[:, s4:], w_in[:, s2:s3],
         jnp.zeros((D_MODEL, LANES - QK_ROPE_DIM), w_in.dtype)], axis=1)
    w_uq_p = jnp.pad(w_uq, ((0, 0), (0, 0), (0, 2 * LANES - QK_DIM)))
    r = jnp.concatenate([w_re, w_rg, jnp.zeros((D_MODEL, LANES - N_EXPERTS - N_EXPERT_GROUPS), F32)], axis=1)
    r_hi = r.astype(BF16)
    r_b = jnp.concatenate([b_re, b_rg, jnp.zeros((LANES - N_EXPERTS - N_EXPERT_GROUPS,), F32)])[None, :]
    return {
        "norm_attn": norm_attn[None, :], "w_in": w_in_p.astype(BF16), "w_gate": w_in[:, s4:].astype(BF16),
        "g_q": g_q[None, :], "w_uq": w_uq_p.reshape(Q_LORA_RANK, N_HEADS * 2 * LANES).astype(BF16),
        "g_kv": g_kv[None, :], "w_ukT": w_uk.reshape(KV_LORA_RANK, N_HEADS * QK_NOPE_DIM).T.astype(BF16),
        "w_uv": w_uv.reshape(KV_LORA_RANK, N_HEADS * V_HEAD_DIM).astype(BF16),
        "w_o": w_o_mla.astype(BF16), "w_out": w_out.astype(BF16),
        "w_fo": jnp.stack([w_fo, w_fo[_CHANNEL_FLIP]]).astype(BF16),
        "norm_moe": norm_moe[None, :], "r_hi": r_hi, "r_lo": (r - r_hi.astype(F32)).astype(BF16), "r_b": r_b,
        "w_eg": _to_bf16(w_eg.reshape(N_EXPERTS, D_MODEL, D_FF_EXPERT)),
        "w_eu": _to_bf16(w_eu.reshape(N_EXPERTS, D_MODEL, D_FF_EXPERT)),
        "w_ed": _to_bf16(w_ed.reshape(N_EXPERTS, D_FF_EXPERT, D_MODEL)),
        "norm_final": norm_final[None, :],
    }


def _tiles(B, S):
    ts = min(512, S // 4)
    return {
        "ts": ts, "tq": min(2048, S), "tm": min(1024, S // 2), "tkf": min(512, S),
        "tn": min(2048, B * FNET_DIM), "tb": 128,
    }


def _trunk(x, meta, w):
    B, S, _ = x.shape
    L = S + N_META
    t = _tiles(B, S)
    wl = dict(w)
    wl["cc"], wl["sc"] = _channel_dft(L)

    meta_pad = jnp.zeros((1, META_PAD, D_MODEL), F32).at[0, :N_META].set(meta)
    _, kT_m, v_m, a_m, b_m = _in_proj(meta_pad, wl, _rope_tables(jnp.arange(META_PAD)), META_PAD)
    q, kT, v, a, b = _in_proj(x, wl, _rope_tables(jnp.arange(S) + N_META), t["ts"])

    v_meta = v_m[0].reshape(META_PAD, N_HEADS, V_HEAD_DIM).transpose(1, 0, 2)
    att = _attention(q, kT, v, kT_m[0, :, 0], v_meta, t["tq"])
    abm = jnp.concatenate([a_m[:N_META], b_m[:N_META]], axis=0)
    half = S // 2
    f_lo = _fnet(a, b, _fnet_tables(S, t["tkf"], jnp.arange(half) + N_META), abm, t["tm"], t["tkf"], t["tn"])
    f_meta = _fnet(a, b, _fnet_tables(S, t["tkf"], jnp.arange(N_META)), abm, N_META, t["tkf"], t["tn"])
    f_hi = _fnet_mirror(f_lo, f_meta, min(512, half), t["tn"])
    h1, route = _post_mix(x, att, f_lo, f_hi, wl, t["ts"])
    y = _moe(h1.reshape(B * S, D_MODEL), route.reshape(B * S, LANES), wl, t["tb"])
    return y.reshape(B, S, D_MODEL)


def kernel(x_prompt, x_sample, meta_tokens, norm_attn, w_in, g_q, w_uq, g_kv, w_uk, w_uv, w_o_mla, w_fo, w_out,
           norm_moe, w_router_group, b_router_group, w_router_expert, b_router_expert, w_exp_gate, w_exp_up,
           w_exp_down, norm_final):
    w = _prep_weights(norm_attn[0], w_in[0], g_q[0], w_uq[0], g_kv[0], w_uk[0], w_uv[0], w_o_mla[0], w_fo[0],
                      w_out[0], norm_moe[0], w_router_group[0], b_router_group[0], w_router_expert[0],
                      b_router_expert[0], w_exp_gate[0], w_exp_up[0], w_exp_down[0], norm_final)
    return _trunk(x_prompt, meta_tokens, w), _trunk(x_sample, meta_tokens, w)
```
